```python
import jax, jax.numpy as jnp
from jax import lax
import numpy as np


D_MODEL = 4096
BATCH = 4
SEQ = 4096
DEPTH = 1

PLE_DIM = 256
ATT_HEAD_DIM = 128
ATT_HEADS_PER_GROUP = 8
ATT_GROUPS = ((128, 1), (512, 4), (2048, 16))
N_ATT_GROUPS = 3
ATT_BLOCK = 128
ATT_QKV_WIDTH = N_ATT_GROUPS * ATT_HEADS_PER_GROUP * ATT_HEAD_DIM
ATT_OUT_WIDTH = ATT_HEADS_PER_GROUP * ATT_HEAD_DIM
RET_HEADS = 8
RET_QK_DIM = 256
RET_V_DIM = 512
RET_QK_WIDTH = RET_HEADS * RET_QK_DIM
RET_V_WIDTH = RET_HEADS * RET_V_DIM
RET_CHUNK = 128
RET_ROPE_BASE = 10000.0
IN_WIDTH = 3 * ATT_QKV_WIDTH + 2 * RET_QK_WIDTH + 2 * RET_V_WIDTH + 2 * D_MODEL
FFN_HIDDEN = -(-8 * D_MODEL // (3 * 256)) * 256
LN_EPS = 1e-5
GN_EPS = 1e-6
DEEPNORM_ALPHA = (2 * DEPTH) ** 0.25
DEEPNORM_BETA = (8 * DEPTH) ** -0.25
NEG_INF = -1e30

kernel_name = 'hybrid_dilated_attn_retention_deepnorm_block'


def _layer_norm(t, g, b):
    tf = t.astype(jnp.float32)
    mu = jnp.mean(tf, axis=-1, keepdims=True)
    var = jnp.mean(jnp.square(tf - mu), axis=-1, keepdims=True)
    return ((tf - mu) * lax.rsqrt(var + LN_EPS) * g + b).astype(t.dtype)


def _dilated_window_attention(q, k, v, window, dilation):
    b, s, h, e = q.shape
    w_sub = window // dilation
    span = dilation * ATT_BLOCK
    s_pad = -(-s // span) * span
    n_blk = s_pad // span
    pad = ((0, 0), (0, s_pad - s), (0, 0), (0, 0))

    def blocks(t):
        return jnp.pad(t, pad).reshape(b, n_blk, ATT_BLOCK, dilation, h, e)

    def with_prev(t):
        prev = jnp.pad(t[:, :-1], ((0, 0), (1, 0), (0, 0), (0, 0), (0, 0), (0, 0)))
        return jnp.concatenate([prev, t], axis=2)

    qb = blocks(q)
    kk = with_prev(blocks(k))
    vv = with_prev(blocks(v))
    qi = jnp.arange(ATT_BLOCK)[:, None]
    ki = jnp.arange(2 * ATT_BLOCK)[None, :]
    dist = qi + ATT_BLOCK - ki
    nb = jnp.arange(n_blk)[:, None, None]
    valid = (dist >= 0) & (dist <= w_sub) & (nb * ATT_BLOCK + ki - ATT_BLOCK >= 0)
    scores = jnp.einsum('bnqrhe,bnkrhe->bnrhqk', qb, kk).astype(jnp.float32) * (e ** -0.5)
    scores = jnp.where(valid[None, :, None, None], scores, NEG_INF)
    m = jnp.max(scores, axis=-1, keepdims=True)
    pr = jnp.exp(scores - m)
    denom = jnp.sum(pr, axis=-1)
    o = jnp.einsum('bnrhqk,bnkrhe->bnqrhe', pr.astype(v.dtype), vv).astype(jnp.float32)
    o = o / jnp.transpose(denom, (0, 1, 4, 2, 3))[..., None]
    lse = jnp.transpose(m[..., 0] + jnp.log(denom), (0, 1, 4, 2, 3))
    o = o.reshape(b, s_pad, h, e)[:, :s]
    lse = lse.reshape(b, s_pad, h)[:, :s]
    return o, lse


def _rotary(t, pos):
    half = t.shape[-1] // 2
    inv_freq = RET_ROPE_BASE ** (-jnp.arange(half, dtype=jnp.float32) / half)
    ang = pos[:, None] * inv_freq[None, :]
    cos = jnp.cos(ang)[None, :, None, :]
    sin = jnp.sin(ang)[None, :, None, :]
    t1 = t[..., :half].astype(jnp.float32)
    t2 = t[..., half:].astype(jnp.float32)
    return jnp.concatenate([t1 * cos - t2 * sin, t1 * sin + t2 * cos], axis=-1).astype(t.dtype)


def _retention(q, k, v):
    b, s, h, dk = q.shape
    dv = v.shape[-1]
    c = RET_CHUNK
    nc = s // c
    log_gamma = jnp.log(1.0 - 2.0 ** (-5.0 - jnp.arange(h, dtype=jnp.float32)))
    idx = jnp.arange(c, dtype=jnp.float32)
    diff = idx[:, None] - idx[None, :]
    intra = jnp.where(diff >= 0, jnp.exp(jnp.maximum(diff, 0.0)[None] * log_gamma[:, None, None]), 0.0)
    cross_decay = jnp.exp((idx + 1.0)[None, :] * log_gamma[:, None])[..., None]
    state_decay = jnp.exp((c - 1.0 - idx)[None, :] * log_gamma[:, None])[..., None]
    chunk_decay = jnp.exp(c * log_gamma)[:, None, None]

    def chunks(t):
        return t.reshape(b, nc, c, h, t.shape[-1]).transpose(1, 0, 3, 2, 4)

    def step(state, qkv):
        qc, kc, vc = qkv
        att = jnp.einsum('bhnd,bhmd->bhnm', qc, kc) * intra
        inner = jnp.einsum('bhnm,bhme->bhne', att, vc)
        cross = jnp.einsum('bhnd,bhde->bhne', qc, state) * cross_decay
        new_state = chunk_decay * state + jnp.einsum('bhmd,bhme->bhde', kc * state_decay, vc)
        return new_state, inner + cross

    init = jnp.zeros((b, h, dk, dv), jnp.float32)
    _, out = lax.scan(step, init, (chunks(q), chunks(k), chunks(v)))
    return out.transpose(1, 0, 3, 2, 4).reshape(b, s, h, dv)


def setup_inputs(seed: int = 0) -> dict:
    key = jax.random.key(seed)
    ks = jax.random.split(key, 17)

    def nrm(k, shape, scale):
        return jax.random.normal(k, shape, jnp.float32) * scale

    return {
        'x': nrm(ks[0], (BATCH, SEQ, D_MODEL), 1.0),
        'p': nrm(ks[1], (DEPTH, BATCH, SEQ, PLE_DIM), 1.0),
        'w_in': nrm(ks[2], (DEPTH, D_MODEL, IN_WIDTH), D_MODEL ** -0.5),
        'w_attn_out': nrm(ks[3], (DEPTH, ATT_OUT_WIDTH, D_MODEL), ATT_OUT_WIDTH ** -0.5),
        'w_ret_out': nrm(ks[4], (DEPTH, RET_V_WIDTH, D_MODEL), RET_V_WIDTH ** -0.5),
        'ret_gn_g': 1.0 + nrm(ks[5], (DEPTH, RET_V_WIDTH), 0.02),
        'w_o': nrm(ks[6], (DEPTH, D_MODEL, D_MODEL), DEEPNORM_BETA * D_MODEL ** -0.5),
        'ln1_g': 1.0 + nrm(ks[7], (DEPTH, D_MODEL), 0.02),
        'ln1_b': nrm(ks[8], (DEPTH, D_MODEL), 0.02),
        'w_ffn_gate': nrm(ks[9], (DEPTH, D_MODEL, FFN_HIDDEN), D_MODEL ** -0.5),
        'w_ffn_up': nrm(ks[10], (DEPTH, D_MODEL, FFN_HIDDEN), D_MODEL ** -0.5),
        'w_ffn_down': nrm(ks[11], (DEPTH, FFN_HIDDEN, D_MODEL), DEEPNORM_BETA * FFN_HIDDEN ** -0.5),
        'w_ple_gate': nrm(ks[12], (DEPTH, D_MODEL, D_MODEL), D_MODEL ** -0.5),
        'w_ple_up': nrm(ks[13], (DEPTH, PLE_DIM, D_MODEL), DEEPNORM_BETA * PLE_DIM ** -0.5),
        'ln2_g': 1.0 + nrm(ks[14], (DEPTH, D_MODEL), 0.02),
        'ln2_b': nrm(ks[15], (DEPTH, D_MODEL), 0.02),
    }


def reference(x, p, w_in, w_attn_out, w_ret_out, ret_gn_g, w_o, ln1_g, ln1_b, w_ffn_gate, w_ffn_up,
              w_ffn_down, w_ple_gate, w_ple_up, ln2_g, ln2_b):
    b, s, _ = x.shape
    pos = jnp.arange(s, dtype=jnp.float32)
    widths = [ATT_QKV_WIDTH] * 3 + [RET_QK_WIDTH] * 2 + [RET_V_WIDTH] * 2 + [D_MODEL] * 2
    splits = np.cumsum(widths)[:-1].tolist()
    h = x
    for i in range(DEPTH):
        proj = h @ w_in[i]
        qa, ka, va, qr, kr, vr, gr, gate_att, gate_ret = jnp.split(proj, splits, axis=-1)

        qa = qa.reshape(b, s, N_ATT_GROUPS, ATT_HEADS_PER_GROUP, ATT_HEAD_DIM)
        ka = ka.reshape(b, s, N_ATT_GROUPS, ATT_HEADS_PER_GROUP, ATT_HEAD_DIM)
        va = va.reshape(b, s, N_ATT_GROUPS, ATT_HEADS_PER_GROUP, ATT_HEAD_DIM)
        outs, lses = [], []
        for g, (win, dil) in enumerate(ATT_GROUPS):
            o_g, lse_g = _dilated_window_attention(qa[:, :, g], ka[:, :, g], va[:, :, g], win, dil)
            outs.append(o_g)
            lses.append(lse_g)
        wts = jax.nn.softmax(jnp.stack(lses, axis=0), axis=0)
        o_att = jnp.sum(wts[..., None] * jnp.stack(outs, axis=0), axis=0)
        o_att = o_att.astype(x.dtype).reshape(b, s, ATT_OUT_WIDTH)
        y_att = o_att @ w_attn_out[i]

        qr = _rotary(qr.reshape(b, s, RET_HEADS, RET_QK_DIM), pos)
        kr = _rotary(kr.reshape(b, s, RET_HEADS, RET_QK_DIM), pos) * (RET_QK_DIM ** -0.5)
        vr = vr.reshape(b, s, RET_HEADS, RET_V_DIM)
        r = _retention(qr, kr, vr)
        mu = jnp.mean(r, axis=-1, keepdims=True)
        var = jnp.mean(jnp.square(r - mu), axis=-1, keepdims=True)
        r = ((r - mu) * lax.rsqrt(var + GN_EPS)).reshape(b, s, RET_V_WIDTH) * ret_gn_g[i]
        y_ret = (jax.nn.silu(gr) * r.astype(x.dtype)) @ w_ret_out[i]

        mixed = jax.nn.sigmoid(gate_att) * y_att + jax.nn.sigmoid(gate_ret) * y_ret
        h = _layer_norm(DEEPNORM_ALPHA * h + mixed @ w_o[i], ln1_g[i], ln1_b[i])

        ffn = (jax.nn.silu(h @ w_ffn_gate[i]) * (h @ w_ffn_up[i])) @ w_ffn_down[i]
        ple = jax.nn.sigmoid(h @ w_ple_gate[i]) * (p[i] @ w_ple_up[i])
        h = _layer_norm(DEEPNORM_ALPHA * h + ffn + ple, ln2_g[i], ln2_b[i])
    return h
```

```python
import functools

import jax
import jax.numpy as jnp
from jax import lax
from jax.experimental import pallas as pl
from jax.experimental.pallas import tpu as pltpu

ATT_HEAD_DIM = 128
ATT_HEADS = 8
ATT_DILATIONS = (1, 4, 16)
ATT_BLOCK = 128
ATT_GROUP_WIDTH = ATT_HEADS * ATT_HEAD_DIM
ATT_QKV_WIDTH = len(ATT_DILATIONS) * ATT_GROUP_WIDTH
RET_HEADS = 8
RET_QK_DIM = 256
RET_V_DIM = 512
RET_QK_WIDTH = RET_HEADS * RET_QK_DIM
RET_V_WIDTH = RET_HEADS * RET_V_DIM
RET_CHUNK = 128
RET_ROPE_BASE = 10000.0
LN_EPS = 1e-5
GN_EPS = 1e-6
NEG_INF = -1e30

V7X_VMEM_LIMIT_BYTES = 56 * 1024 * 1024
LANE = 128

BF16 = jnp.bfloat16
F32 = jnp.float32


def _params(*semantics):
    return pltpu.CompilerParams(dimension_semantics=semantics, vmem_limit_bytes=V7X_VMEM_LIMIT_BYTES)


def _tile(dim, pref):
    if dim <= pref:
        return dim
    t = (pref // LANE) * LANE
    while t > LANE and dim % t:
        t -= LANE
    assert dim % t == 0, (dim, pref)
    return t


def _sigmoid(v):
    return 1.0 / (1.0 + jnp.exp(-v))


def _mm(a, b):
    return jnp.dot(a, b, preferred_element_type=F32)


def _proj_kernel(x_ref, w_ref, o_ref):
    o_ref[...] = _mm(x_ref[...], w_ref[...]).astype(o_ref.dtype)


def _proj(x_bf, w_bf, col0, width, out_dtype, name):
    n, k = x_bf.shape
    tm, tn = _tile(n, 1024), _tile(width, 1024)
    assert col0 % tn == 0
    off = col0 // tn
    return pl.pallas_call(
        _proj_kernel,
        out_shape=jax.ShapeDtypeStruct((n, width), out_dtype),
        grid=(width // tn, n // tm),
        in_specs=[pl.BlockSpec((tm, k), lambda j, i: (i, 0)),
                  pl.BlockSpec((k, tn), lambda j, i: (0, off + j))],
        out_specs=pl.BlockSpec((tm, tn), lambda j, i: (i, j)),
        compiler_params=_params("parallel", "parallel"),
        name=name,
    )(x_bf, w_bf)


def _proj_rot_kernel(x_ref, w_ref, cos_ref, sin_ref, o_ref, *, n_q_blocks, k_scale):
    j = pl.program_id(0)
    acc = _mm(x_ref[...], w_ref[...])
    scale = jnp.where(j >= n_q_blocks, k_scale, 1.0).astype(F32)
    cos = cos_ref[...]
    sin = sin_ref[...]
    half = RET_QK_DIM // 2
    for h in range(acc.shape[1] // RET_QK_DIM):
        lo = h * RET_QK_DIM
        t1 = acc[:, lo:lo + half]
        t2 = acc[:, lo + half:lo + RET_QK_DIM]
        o_ref[:, lo:lo + half] = (t1 * cos - t2 * sin) * scale
        o_ref[:, lo + half:lo + RET_QK_DIM] = (t1 * sin + t2 * cos) * scale


def _proj_rot(x_bf, w_bf, col0, cos, sin, seq):
    n, k = x_bf.shape
    width = 2 * RET_QK_WIDTH
    tm, tn = _tile(seq, 1024), 1024
    assert col0 % tn == 0 and RET_QK_WIDTH % tn == 0 and n % seq == 0
    off = col0 // tn
    pos_blocks = seq // tm
    kern = functools.partial(_proj_rot_kernel, n_q_blocks=RET_QK_WIDTH // tn, k_scale=RET_QK_DIM ** -0.5)
    half = RET_QK_DIM // 2
    return pl.pallas_call(
        kern,
        out_shape=jax.ShapeDtypeStruct((n, width), F32),
        grid=(width // tn, n // tm),
        in_specs=[pl.BlockSpec((tm, k), lambda j, i: (i, 0)),
                  pl.BlockSpec((k, tn), lambda j, i: (0, off + j)),
                  pl.BlockSpec((tm, half), lambda j, i: (i % pos_blocks, 0)),
                  pl.BlockSpec((tm, half), lambda j, i: (i % pos_blocks, 0))],
        out_specs=pl.BlockSpec((tm, tn), lambda j, i: (i, j)),
        compiler_params=_params("parallel", "parallel"),
        name="proj_rot",
    )(x_bf, w_bf, cos, sin)


def _attn_kernel(q_ref, kp_ref, kc_ref, vp_ref, vc_ref, o_ref, lse_ref, *, dilation, n_blk):
    u = pl.program_id(0)
    no_prev_off = jnp.where(((u // dilation) % n_blk) > 0, 0, ATT_BLOCK)
    qi = lax.broadcasted_iota(jnp.int32, (ATT_BLOCK, ATT_BLOCK), 0)
    ki = lax.broadcasted_iota(jnp.int32, (ATT_BLOCK, ATT_BLOCK), 1)
    valid_prev = ki >= qi + no_prev_off
    valid_cur = ki <= qi
    scale = ATT_HEAD_DIM ** -0.5
    dn = (((1,), (1,)), ((), ()))
    for h in range(ATT_HEADS):
        sl = slice(h * ATT_HEAD_DIM, (h + 1) * ATT_HEAD_DIM)
        q = q_ref[:, sl]
        sp = lax.dot_general(q, kp_ref[:, sl], dn, preferred_element_type=F32) * scale
        sc = lax.dot_general(q, kc_ref[:, sl], dn, preferred_element_type=F32) * scale
        sp = jnp.where(valid_prev, sp, NEG_INF)
        sc = jnp.where(valid_cur, sc, NEG_INF)
        m = jnp.maximum(jnp.max(sp, axis=-1, keepdims=True), jnp.max(sc, axis=-1, keepdims=True))
        pp = jnp.exp(sp - m)
        pc = jnp.exp(sc - m)
        denom = jnp.sum(pp, axis=-1, keepdims=True) + jnp.sum(pc, axis=-1, keepdims=True)
        o = _mm(pp.astype(BF16), vp_ref[:, sl]) + _mm(pc.astype(BF16), vc_ref[:, sl])
        o_ref[:, sl] = o / denom
        lse_ref[:, sl] = jnp.broadcast_to(m + jnp.log(denom), (ATT_BLOCK, ATT_HEAD_DIM))


def _attention_units(q, k, v, dilation, n_blk):
    units = q.shape[0]
    blk = (None, ATT_BLOCK, ATT_GROUP_WIDTH)

    def cur(u):
        return (u, 0, 0)

    def prev(u):
        return (jnp.where((u // dilation) % n_blk > 0, u - dilation, u), 0, 0)

    kern = functools.partial(_attn_kernel, dilation=dilation, n_blk=n_blk)
    out = jax.ShapeDtypeStruct(q.shape, F32)
    return pl.pallas_call(
        kern,
        out_shape=(out, out),
        grid=(units,),
        in_specs=[pl.BlockSpec(blk, cur), pl.BlockSpec(blk, prev), pl.BlockSpec(blk, cur),
                  pl.BlockSpec(blk, prev), pl.BlockSpec(blk, cur)],
        out_specs=(pl.BlockSpec(blk, cur), pl.BlockSpec(blk, cur)),
        compiler_params=_params("parallel"),
        name=f"attn_d{dilation}",
    )(q, k, k, v, v)


def _combine_kernel(o1, o2, o3, l1, l2, l3, out_ref):
    a, b, c = l1[...], l2[...], l3[...]
    m = jnp.maximum(jnp.maximum(a, b), c)
    ea, eb, ec = jnp.exp(a - m), jnp.exp(b - m), jnp.exp(c - m)
    s = ea + eb + ec
    out = (ea / s) * o1[...] + (eb / s) * o2[...] + (ec / s) * o3[...]
    out_ref[...] = out.astype(out_ref.dtype)


def _combine(os_, ls_):
    n, w = os_[0].shape
    tm = _tile(n, 512)
    spec = pl.BlockSpec((tm, w), lambda i: (i, 0))
    return pl.pallas_call(
        _combine_kernel,
        out_shape=jax.ShapeDtypeStruct((n, w), BF16),
        grid=(n // tm,),
        in_specs=[spec] * 6,
        out_specs=spec,
        compiler_params=_params("parallel"),
        name="attn_combine",
    )(*os_, *ls_)


def _ret_kernel(q_ref, k_ref, v_ref, g_ref, gn_ref, intra_ref, cross_ref, sdec_ref, cdec_ref,
                z_ref, state_ref):
    c = pl.program_id(2)

    @pl.when(c == 0)
    def _():
        state_ref[...] = jnp.zeros_like(state_ref)

    q = q_ref[...].astype(BF16)
    k = k_ref[...]
    v = v_ref[...]
    state = state_ref[...]
    att = lax.dot_general(q, k.astype(BF16), (((1,), (1,)), ((), ())), preferred_element_type=F32)
    att = att * intra_ref[...]
    inner = _mm(att.astype(BF16), v)
    cross = _mm(q, state.astype(BF16)) * cross_ref[...]
    kd = (k * sdec_ref[...]).astype(BF16)
    upd = lax.dot_general(kd, v, (((0,), (0,)), ((), ())), preferred_element_type=F32)
    state_ref[...] = cdec_ref[...] * state + upd

    r = inner + cross
    mu = jnp.mean(r, axis=-1, keepdims=True)
    var = jnp.mean(jnp.square(r - mu), axis=-1, keepdims=True)
    rn = (r - mu) * lax.rsqrt(var + GN_EPS) * gn_ref[...]
    g = g_ref[...]
    z_ref[...] = ((g * _sigmoid(g)) * rn).astype(z_ref.dtype)


def _retention(qk, v, gates, gn_g, batch, seq):
    n = qk.shape[0]
    c = RET_CHUNK
    nc = seq // c
    hh = jnp.arange(RET_HEADS, dtype=F32)
    log_gamma = jnp.log(1.0 - 2.0 ** (-5.0 - hh))
    idx = jnp.arange(c, dtype=F32)
    diff = idx[:, None] - idx[None, :]
    intra = jnp.where(diff >= 0, jnp.exp(jnp.maximum(diff, 0.0)[None] * log_gamma[:, None, None]), 0.0)
    cross_decay = jnp.exp((idx + 1.0)[None, :] * log_gamma[:, None])[..., None]
    state_decay = jnp.exp((c - 1.0 - idx)[None, :] * log_gamma[:, None])[..., None]
    chunk_decay = jnp.exp(c * log_gamma)[:, None, None]

    def rows(b, h, ci):
        return b * nc + ci

    return pl.pallas_call(
        _ret_kernel,
        out_shape=jax.ShapeDtypeStruct((n, RET_V_WIDTH), BF16),
        grid=(batch, RET_HEADS, nc),
        in_specs=[
            pl.BlockSpec((c, RET_QK_DIM), lambda b, h, ci: (rows(b, h, ci), h)),
            pl.BlockSpec((c, RET_QK_DIM), lambda b, h, ci: (rows(b, h, ci), RET_HEADS + h)),
            pl.BlockSpec((c, RET_V_DIM), lambda b, h, ci: (rows(b, h, ci), h)),
            pl.BlockSpec((c, RET_V_DIM), lambda b, h, ci: (rows(b, h, ci), h)),
            pl.BlockSpec((1, RET_V_DIM), lambda b, h, ci: (0, h)),
            pl.BlockSpec((None, c, c), lambda b, h, ci: (h, 0, 0)),
            pl.BlockSpec((None, c, 1), lambda b, h, ci: (h, 0, 0)),
            pl.BlockSpec((None, c, 1), lambda b, h, ci: (h, 0, 0)),
            pl.BlockSpec((None, 1, 1), lambda b, h, ci: (h, 0, 0)),
        ],
        out_specs=pl.BlockSpec((c, RET_V_DIM), lambda b, h, ci: (rows(b, h, ci), h)),
        scratch_shapes=[pltpu.VMEM((RET_QK_DIM, RET_V_DIM), F32)],
        compiler_params=_params("parallel", "parallel", "arbitrary"),
        name="retention",
    )(qk, qk, v, gates, gn_g, intra, cross_decay, state_decay, chunk_decay)


def _mix_kernel(oa_ref, z_ref, wa_ref, wr_ref, ga_ref, gr_ref, o_ref):
    ya = _mm(oa_ref[...], wa_ref[...])
    yr = _mm(z_ref[...], wr_ref[...])
    o_ref[...] = (_sigmoid(ga_ref[...]) * ya + _sigmoid(gr_ref[...]) * yr).astype(o_ref.dtype)


def _mix(o_att, z, wa, wr, gates, gate_col0):
    n = o_att.shape[0]
    d = wa.shape[1]
    tm, tn = _tile(n, 512), _tile(d, 1024)
    assert gate_col0 % tn == 0
    ga_off = gate_col0 // tn
    gr_off = ga_off + d // tn
    return pl.pallas_call(
        _mix_kernel,
        out_shape=jax.ShapeDtypeStruct((n, d), BF16),
        grid=(d // tn, n // tm),
        in_specs=[pl.BlockSpec((tm, o_att.shape[1]), lambda j, i: (i, 0)),
                  pl.BlockSpec((tm, z.shape[1]), lambda j, i: (i, 0)),
                  pl.BlockSpec((wa.shape[0], tn), lambda j, i: (0, j)),
                  pl.BlockSpec((wr.shape[0], tn), lambda j, i: (0, j)),
                  pl.BlockSpec((tm, tn), lambda j, i: (i, ga_off + j)),
                  pl.BlockSpec((tm, tn), lambda j, i: (i, gr_off + j))],
        out_specs=pl.BlockSpec((tm, tn), lambda j, i: (i, j)),
        compiler_params=_params("parallel", "parallel"),
        name="mix",
    )(o_att, z, wa, wr, gates, gates)


def _layer_norm_rows(t, g, b):
    mu = jnp.mean(t, axis=-1, keepdims=True)
    var = jnp.mean(jnp.square(t - mu), axis=-1, keepdims=True)
    return (t - mu) * lax.rsqrt(var + LN_EPS) * g + b


LN_ROW_CHUNK = 128


def _layer_norm_inplace(o_ref, g_ref, b_ref, cast_ref=None):
    g, b = g_ref[...], b_ref[...]
    chunk = min(LN_ROW_CHUNK, o_ref.shape[0])

    def body(r, carry):
        rows = pl.ds(pl.multiple_of(r * chunk, chunk), chunk)
        y = _layer_norm_rows(o_ref[rows, :], g, b)
        o_ref[rows, :] = y
        if cast_ref is not None:
            cast_ref[rows, :] = y.astype(cast_ref.dtype)
        return carry

    lax.fori_loop(0, o_ref.shape[0] // chunk, body, 0)


def _proj_ln_kernel(a_ref, w_ref, res_ref, g_ref, b_ref, of_ref, ob_ref, *, alpha, tn):
    j = pl.program_id(1)
    col = pl.multiple_of(j * tn, tn)
    of_ref[:, pl.ds(col, tn)] = alpha * res_ref[...] + _mm(a_ref[...], w_ref[...])

    @pl.when(j == pl.num_programs(1) - 1)
    def _():
        _layer_norm_inplace(of_ref, g_ref, b_ref, ob_ref)


def _proj_ln(a, w, res, g, b, alpha):
    n, k = a.shape
    d = w.shape[1]
    tm, tn = _tile(n, 512), _tile(d, 512)
    kern = functools.partial(_proj_ln_kernel, alpha=alpha, tn=tn)
    row = pl.BlockSpec((tm, d), lambda i, j: (i, 0))
    vec = pl.BlockSpec((1, d), lambda i, j: (0, 0))
    return pl.pallas_call(
        kern,
        out_shape=(jax.ShapeDtypeStruct((n, d), F32), jax.ShapeDtypeStruct((n, d), BF16)),
        grid=(n // tm, d // tn),
        in_specs=[pl.BlockSpec((tm, k), lambda i, j: (i, 0)),
                  pl.BlockSpec((k, tn), lambda i, j: (0, j)),
                  pl.BlockSpec((tm, tn), lambda i, j: (i, j)),
                  vec, vec],
        out_specs=(row, row),
        compiler_params=_params("parallel", "arbitrary"),
        name="proj_ln",
    )(a, w, res, g, b)


def _swiglu_kernel(h_ref, wg_ref, wu_ref, o_ref):
    h = h_ref[...]
    gate = _mm(h, wg_ref[...])
    up = _mm(h, wu_ref[...])
    o_ref[...] = ((gate * _sigmoid(gate)) * up).astype(o_ref.dtype)


def _swiglu(h_bf, wg, wu):
    n, k = h_bf.shape
    f = wg.shape[1]
    tm, tn = _tile(n, 2048), _tile(f, 256)
    w_spec = pl.BlockSpec((k, tn), lambda i, j: (0, j))
    return pl.pallas_call(
        _swiglu_kernel,
        out_shape=jax.ShapeDtypeStruct((n, f), BF16),
        grid=(n // tm, f // tn),
        in_specs=[pl.BlockSpec((tm, k), lambda i, j: (i, 0)), w_spec, w_spec],
        out_specs=pl.BlockSpec((tm, tn), lambda i, j: (i, j)),
        compiler_params=_params("parallel", "parallel"),
        name="swiglu",
    )(h_bf, wg, wu)


def _ple_kernel(hb_ref, wg_ref, p_ref, wu_ref, hf_ref, o_ref, *, alpha):
    gate = _sigmoid(_mm(hb_ref[...], wg_ref[...]))
    up = _mm(p_ref[...], wu_ref[...])
    o_ref[...] = alpha * hf_ref[...] + gate * up


def _ple_base(h_bf, h_f32, p_bf, wpg, wpu, alpha):
    n, k = h_bf.shape
    d = wpg.shape[1]
    tm, tn = _tile(n, 512), _tile(d, 1024)
    kern = functools.partial(_ple_kernel, alpha=alpha)
    return pl.pallas_call(
        kern,
        out_shape=jax.ShapeDtypeStruct((n, d), F32),
        grid=(d // tn, n // tm),
        in_specs=[pl.BlockSpec((tm, k), lambda j, i: (i, 0)),
                  pl.BlockSpec((k, tn), lambda j, i: (0, j)),
                  pl.BlockSpec((tm, p_bf.shape[1]), lambda j, i: (i, 0)),
                  pl.BlockSpec((p_bf.shape[1], tn), lambda j, i: (0, j)),
                  pl.BlockSpec((tm, tn), lambda j, i: (i, j))],
        out_specs=pl.BlockSpec((tm, tn), lambda j, i: (i, j)),
        compiler_params=_params("parallel", "parallel"),
        name="ple_base",
    )(h_bf, wpg, p_bf, wpu, h_f32)


def _down_ln_kernel(a_ref, w_ref, base_ref, g_ref, b_ref, o_ref, *, tn):
    j = pl.program_id(1)
    col = pl.multiple_of(j * tn, tn)
    o_ref[:, pl.ds(col, tn)] = base_ref[...] + _mm(a_ref[...], w_ref[...])

    @pl.when(j == pl.num_programs(1) - 1)
    def _():
        _layer_norm_inplace(o_ref, g_ref, b_ref)


def _down_ln(hidden, wd, base, g, b):
    n, f = hidden.shape
    d = wd.shape[1]
    tm, tn = _tile(n, 512), _tile(d, 256)
    kern = functools.partial(_down_ln_kernel, tn=tn)
    vec = pl.BlockSpec((1, d), lambda i, j: (0, 0))
    return pl.pallas_call(
        kern,
        out_shape=jax.ShapeDtypeStruct((n, d), F32),
        grid=(n // tm, d // tn),
        in_specs=[pl.BlockSpec((tm, f), lambda i, j: (i, 0)),
                  pl.BlockSpec((f, tn), lambda i, j: (0, j)),
                  pl.BlockSpec((tm, tn), lambda i, j: (i, j)),
                  vec, vec],
        out_specs=pl.BlockSpec((tm, d), lambda i, j: (i, 0)),
        compiler_params=_params("parallel", "arbitrary"),
        name="down_ln",
    )(hidden, wd, base, g, b)


def _to_units(t, batch, seq, dilation):
    c = t.shape[-1]
    n_blk = seq // (dilation * ATT_BLOCK)
    t = t.reshape(batch, n_blk, ATT_BLOCK, dilation, c).transpose(0, 1, 3, 2, 4)
    return t.reshape(batch * n_blk * dilation, ATT_BLOCK, c)


def _from_units(t, batch, seq, dilation):
    c = t.shape[-1]
    n_blk = seq // (dilation * ATT_BLOCK)
    t = t.reshape(batch, n_blk, dilation, ATT_BLOCK, c).transpose(0, 1, 3, 2, 4)
    return t.reshape(batch * seq, c)


def _layer(h_f32, h_bf, p_bf, w_in, w_attn_out, w_ret_out, ret_gn_g, w_o, ln1_g, ln1_b,
           w_ffn_gate, w_ffn_up, w_ffn_down, w_ple_gate, w_ple_up, ln2_g, ln2_b,
           cos, sin, batch, seq, alpha):
    d_model = h_f32.shape[1]
    c_qr = 3 * ATT_QKV_WIDTH
    c_vr = c_qr + 2 * RET_QK_WIDTH
    c_gr = c_vr + RET_V_WIDTH
    assert w_in.shape[1] == c_gr + RET_V_WIDTH + 2 * d_model

    qkv_a = _proj(h_bf, w_in, 0, c_qr, BF16, "proj_att")
    qk_r = _proj_rot(h_bf, w_in, c_qr, cos, sin, seq)
    v_r = _proj(h_bf, w_in, c_vr, RET_V_WIDTH, BF16, "proj_vr")
    gates = _proj(h_bf, w_in, c_gr, RET_V_WIDTH + 2 * d_model, F32, "proj_gates")

    outs, lses = [], []
    for g, dil in enumerate(ATT_DILATIONS):
        n_blk = seq // (dil * ATT_BLOCK)
        assert n_blk * dil * ATT_BLOCK == seq
        cols = [qkv_a[:, s * ATT_QKV_WIDTH + g * ATT_GROUP_WIDTH:s * ATT_QKV_WIDTH + (g + 1) * ATT_GROUP_WIDTH]
                for s in range(3)]
        q_u, k_u, v_u = [_to_units(t, batch, seq, dil) for t in cols]
        o_u, l_u = _attention_units(q_u, k_u, v_u, dil, n_blk)
        outs.append(_from_units(o_u, batch, seq, dil))
        lses.append(_from_units(l_u, batch, seq, dil))
    o_att = _combine(outs, lses)

    z = _retention(qk_r, v_r, gates, ret_gn_g, batch, seq)

    mixed = _mix(o_att, z, w_attn_out, w_ret_out, gates, RET_V_WIDTH)
    h1_f32, h1_bf = _proj_ln(mixed, w_o, h_f32, ln1_g, ln1_b, alpha)

    hidden = _swiglu(h1_bf, w_ffn_gate, w_ffn_up)
    base = _ple_base(h1_bf, h1_f32, p_bf, w_ple_gate, w_ple_up, alpha)
    return _down_ln(hidden, w_ffn_down, base, ln2_g, ln2_b)


def kernel(x, p, w_in, w_attn_out, w_ret_out, ret_gn_g, w_o, ln1_g, ln1_b, w_ffn_gate, w_ffn_up,
           w_ffn_down, w_ple_gate, w_ple_up, ln2_g, ln2_b):
    batch, seq, d_model = x.shape
    depth = w_in.shape[0]
    n = batch * seq
    alpha = (2 * depth) ** 0.25

    half = RET_QK_DIM // 2
    pos = jnp.arange(seq, dtype=F32)
    inv_freq = RET_ROPE_BASE ** (-jnp.arange(half, dtype=F32) / half)
    ang = pos[:, None] * inv_freq[None, :]
    cos, sin = jnp.cos(ang), jnp.sin(ang)

    h = x.reshape(n, d_model)
    for i in range(depth):
        bf = lambda w: w[i].astype(BF16)
        row = lambda v: v[i].reshape(1, -1)
        h = _layer(h, h.astype(BF16), p[i].reshape(n, -1).astype(BF16),
                   bf(w_in), bf(w_attn_out), bf(w_ret_out), row(ret_gn_g), bf(w_o), row(ln1_g), row(ln1_b),
                   bf(w_ffn_gate), bf(w_ffn_up), bf(w_ffn_down), bf(w_ple_gate), bf(w_ple_up),
                   row(ln2_g), row(ln2_b), cos, sin, batch, seq, alpha)
    return h.reshape(batch, seq, d_model).astype(x.dtype)
```

```python
import functools

import jax
import jax.numpy as jnp
from jax import lax
from jax.experimental import pallas as pl
from jax.experimental.pallas import tpu as pltpu

ATT_HEAD_DIM = 128
ATT_HEADS = 8
ATT_DILATIONS = (1, 4, 16)
ATT_BLOCK = 128
ATT_GROUP_WIDTH = ATT_HEADS * ATT_HEAD_DIM
ATT_QKV_WIDTH = len(ATT_DILATIONS) * ATT_GROUP_WIDTH
RET_HEADS = 8
RET_QK_DIM = 256
RET_V_DIM = 512
RET_QK_WIDTH = RET_HEADS * RET_QK_DIM
RET_V_WIDTH = RET_HEADS * RET_V_DIM
RET_CHUNK = 128
RET_ROPE_BASE = 10000.0
LN_EPS = 1e-5
GN_EPS = 1e-6
NEG_INF = -1e30

V7X_VMEM_LIMIT_BYTES = 56 * 1024 * 1024
LANE = 128

BF16 = jnp.bfloat16
F32 = jnp.float32


def _params(*semantics):
    return pltpu.CompilerParams(dimension_semantics=semantics, vmem_limit_bytes=V7X_VMEM_LIMIT_BYTES)


def _tile(dim, pref):
    if dim <= pref:
        return dim
    t = (pref // LANE) * LANE
    while t > LANE and dim % t:
        t -= LANE
    assert dim % t == 0, (dim, pref)
    return t


def _sigmoid(v):
    return 1.0 / (1.0 + jnp.exp(-v))


def _mm(a, b):
    return jnp.dot(a, b, preferred_element_type=F32)


def _proj_kernel(x_ref, w_ref, o_ref):
    o_ref[...] = _mm(x_ref[...], w_ref[...]).astype(o_ref.dtype)


def _proj(x_bf, w_bf, col0, width, out_dtype, name):
    n, k = x_bf.shape
    tm, tn = _tile(n, 1024), _tile(width, 1024)
    assert col0 % tn == 0
    off = col0 // tn
    return pl.pallas_call(
        _proj_kernel,
        out_shape=jax.ShapeDtypeStruct((n, width), out_dtype),
        grid=(width // tn, n // tm),
        in_specs=[pl.BlockSpec((tm, k), lambda j, i: (i, 0)),
                  pl.BlockSpec((k, tn), lambda j, i: (0, off + j))],
        out_specs=pl.BlockSpec((tm, tn), lambda j, i: (i, j)),
        compiler_params=_params("parallel", "parallel"),
        name=name,
    )(x_bf, w_bf)


def _proj_rot_kernel(x_ref, w_ref, cos_ref, sin_ref, o_ref, *, n_q_blocks, k_scale):
    j = pl.program_id(0)
    acc = _mm(x_ref[...], w_ref[...])
    scale = jnp.where(j >= n_q_blocks, k_scale, 1.0).astype(F32)
    cos = cos_ref[...]
    sin = sin_ref[...]
    half = RET_QK_DIM // 2
    for h in range(acc.shape[1] // RET_QK_DIM):
        lo = h * RET_QK_DIM
        t1 = acc[:, lo:lo + half]
        t2 = acc[:, lo + half:lo + RET_QK_DIM]
        o_ref[:, lo:lo + half] = (t1 * cos - t2 * sin) * scale
        o_ref[:, lo + half:lo + RET_QK_DIM] = (t1 * sin + t2 * cos) * scale


def _proj_rot(x_bf, w_bf, col0, cos, sin, seq):
    n, k = x_bf.shape
    width = 2 * RET_QK_WIDTH
    tm, tn = _tile(seq, 1024), 1024
    assert col0 % tn == 0 and RET_QK_WIDTH % tn == 0 and n % seq == 0
    off = col0 // tn
    pos_blocks = seq // tm
    kern = functools.partial(_proj_rot_kernel, n_q_blocks=RET_QK_WIDTH // tn, k_scale=RET_QK_DIM ** -0.5)
    half = RET_QK_DIM // 2
    return pl.pallas_call(
        kern,
        out_shape=jax.ShapeDtypeStruct((n, width), F32),
        grid=(width // tn, n // tm),
        in_specs=[pl.BlockSpec((tm, k), lambda j, i: (i, 0)),
                  pl.BlockSpec((k, tn), lambda j, i: (0, off + j)),
                  pl.BlockSpec((tm, half), lambda j, i: (i % pos_blocks, 0)),
                  pl.BlockSpec((tm, half), lambda j, i: (i % pos_blocks, 0))],
        out_specs=pl.BlockSpec((tm, tn), lambda j, i: (i, j)),
        compiler_params=_params("parallel", "parallel"),
        name="proj_rot",
    )(x_bf, w_bf, cos, sin)


ATT_UNITS_PER_STEP = 4


def _attn_kernel(q_ref, k_ref, kp_ref, v_ref, vp_ref, o_ref, lse_ref, *, n_blk, ub):
    first_blk = (pl.program_id(0) * ub) % n_blk
    qi = lax.broadcasted_iota(jnp.int32, (ATT_BLOCK, 2 * ATT_BLOCK), 0)
    kj = lax.broadcasted_iota(jnp.int32, (ATT_BLOCK, 2 * ATT_BLOCK), 1)
    in_window = kj <= qi + ATT_BLOCK
    valid_inner = jnp.logical_and(kj >= qi, in_window)
    lo = jnp.where(first_blk > 0, 0, ATT_BLOCK)
    valid_first = jnp.logical_and(kj >= jnp.maximum(qi, lo), in_window)
    scale = ATT_HEAD_DIM ** -0.5
    dn = (((1,), (1,)), ((), ()))
    ones = jnp.ones((2 * ATT_BLOCK, ATT_HEAD_DIM), BF16)
    units = [(i, h) for i in range(ub) for h in range(ATT_HEADS)]

    def cols(h):
        return slice(h * ATT_HEAD_DIM, (h + 1) * ATT_HEAD_DIM)

    def with_prev(cur_ref, prev_ref, i, h):
        prev = prev_ref[0, :, cols(h)] if i == 0 else cur_ref[i - 1, :, cols(h)]
        return jnp.concatenate([prev, cur_ref[i, :, cols(h)]], axis=0)

    scores = [lax.dot_general(q_ref[i, :, cols(h)], with_prev(k_ref, kp_ref, i, h), dn,
                              preferred_element_type=F32) * scale for i, h in units]
    probs, maxes = [], []
    for (i, h), s in zip(units, scores):
        s = jnp.where(valid_first if i == 0 else valid_inner, s, NEG_INF)
        m = jnp.max(s, axis=-1, keepdims=True)
        probs.append(jnp.exp(s - m).astype(BF16))
        maxes.append(m)
    for (i, h), p, m in zip(units, probs, maxes):
        v_aug = jnp.concatenate([with_prev(v_ref, vp_ref, i, h), ones], axis=1)
        acc = _mm(p, v_aug)
        den = acc[:, ATT_HEAD_DIM:]
        o_ref[i, :, cols(h)] = acc[:, :ATT_HEAD_DIM] / den
        lse_ref[i, :, cols(h)] = m + jnp.log(den)


def _attention_units(q, k, v, n_blk):
    units = q.shape[0]
    ub = min(ATT_UNITS_PER_STEP, n_blk)
    assert n_blk % ub == 0
    blk = (ub, ATT_BLOCK, ATT_GROUP_WIDTH)
    one = (1, ATT_BLOCK, ATT_GROUP_WIDTH)

    def cur(s):
        return (s, 0, 0)

    def prev(s):
        return (jnp.maximum(s * ub - 1, 0), 0, 0)

    kern = functools.partial(_attn_kernel, n_blk=n_blk, ub=ub)
    out = jax.ShapeDtypeStruct(q.shape, F32)
    return pl.pallas_call(
        kern,
        out_shape=(out, out),
        grid=(units // ub,),
        in_specs=[pl.BlockSpec(blk, cur), pl.BlockSpec(blk, cur), pl.BlockSpec(one, prev),
                  pl.BlockSpec(blk, cur), pl.BlockSpec(one, prev)],
        out_specs=(pl.BlockSpec(blk, cur), pl.BlockSpec(blk, cur)),
        compiler_params=_params("parallel"),
        name=f"attn_nblk{n_blk}",
    )(q, k, k, v, v)


def _combine_kernel(o1, o2, o3, l1, l2, l3, out_ref):
    a, b, c = l1[...], l2[...], l3[...]
    m = jnp.maximum(jnp.maximum(a, b), c)
    ea, eb, ec = jnp.exp(a - m), jnp.exp(b - m), jnp.exp(c - m)
    s = ea + eb + ec
    out = (ea / s) * o1[...] + (eb / s) * o2[...] + (ec / s) * o3[...]
    out_ref[...] = out.astype(out_ref.dtype)


def _combine(os_, ls_):
    n, w = os_[0].shape
    tm = _tile(n, 512)
    spec = pl.BlockSpec((tm, w), lambda i: (i, 0))
    return pl.pallas_call(
        _combine_kernel,
        out_shape=jax.ShapeDtypeStruct((n, w), BF16),
        grid=(n // tm,),
        in_specs=[spec] * 6,
        out_specs=spec,
        compiler_params=_params("parallel"),
        name="attn_combine",
    )(*os_, *ls_)


RET_CHUNKS_PER_STEP = 8


def _ret_kernel(q_ref, k_ref, v_ref, g_ref, gn_ref, intra_ref, cross_ref, sdec_ref, cdec_ref,
                z_ref, state_ref, *, cb):
    @pl.when(pl.program_id(2) == 0)
    def _():
        state_ref[...] = jnp.zeros_like(state_ref)

    c = RET_CHUNK
    intra, cross_decay, state_decay, chunk_decay = intra_ref[...], cross_ref[...], sdec_ref[...], cdec_ref[...]
    gn = gn_ref[...]
    qs, inners, updates = [], [], []
    for t in range(cb):
        rows = slice(t * c, (t + 1) * c)
        q = q_ref[rows, :].astype(BF16)
        k = k_ref[rows, :]
        v = v_ref[rows, :]
        att = lax.dot_general(q, k.astype(BF16), (((1,), (1,)), ((), ())), preferred_element_type=F32) * intra
        inners.append(_mm(att.astype(BF16), v))
        kd = (k * state_decay).astype(BF16)
        updates.append(lax.dot_general(kd, v, (((0,), (0,)), ((), ())), preferred_element_type=F32))
        qs.append(q)
    state = state_ref[...]
    for t in range(cb):
        rows = slice(t * c, (t + 1) * c)
        r = inners[t] + _mm(qs[t], state.astype(BF16)) * cross_decay
        state = chunk_decay * state + updates[t]
        mu = jnp.mean(r, axis=-1, keepdims=True)
        var = jnp.mean(jnp.square(r - mu), axis=-1, keepdims=True)
        rn = (r - mu) * lax.rsqrt(var + GN_EPS) * gn
        g = g_ref[rows, :]
        z_ref[rows, :] = ((g * _sigmoid(g)) * rn).astype(z_ref.dtype)
    state_ref[...] = state


def _retention(qk, v, gates, gn_g, batch, seq):
    n = qk.shape[0]
    c = RET_CHUNK
    nc = seq // c
    cb = min(RET_CHUNKS_PER_STEP, nc)
    assert nc % cb == 0
    steps = nc // cb
    hh = jnp.arange(RET_HEADS, dtype=F32)
    log_gamma = jnp.log(1.0 - 2.0 ** (-5.0 - hh))
    idx = jnp.arange(c, dtype=F32)
    diff = idx[:, None] - idx[None, :]
    intra = jnp.where(diff >= 0, jnp.exp(jnp.maximum(diff, 0.0)[None] * log_gamma[:, None, None]), 0.0)
    cross_decay = jnp.exp((idx + 1.0)[None, :] * log_gamma[:, None])[..., None]
    state_decay = jnp.exp((c - 1.0 - idx)[None, :] * log_gamma[:, None])[..., None]
    chunk_decay = jnp.exp(c * log_gamma)[:, None, None]

    def rows(b, h, s):
        return b * steps + s

    kern = functools.partial(_ret_kernel, cb=cb)
    return pl.pallas_call(
        kern,
        out_shape=jax.ShapeDtypeStruct((n, RET_V_WIDTH), BF16),
        grid=(batch, RET_HEADS, steps),
        in_specs=[
            pl.BlockSpec((cb * c, RET_QK_DIM), lambda b, h, s: (rows(b, h, s), h)),
            pl.BlockSpec((cb * c, RET_QK_DIM), lambda b, h, s: (rows(b, h, s), RET_HEADS + h)),
            pl.BlockSpec((cb * c, RET_V_DIM), lambda b, h, s: (rows(b, h, s), h)),
            pl.BlockSpec((cb * c, RET_V_DIM), lambda b, h, s: (rows(b, h, s), h)),
            pl.BlockSpec((1, RET_V_DIM), lambda b, h, s: (0, h)),
            pl.BlockSpec((None, c, c), lambda b, h, s: (h, 0, 0)),
            pl.BlockSpec((None, c, 1), lambda b, h, s: (h, 0, 0)),
            pl.BlockSpec((None, c, 1), lambda b, h, s: (h, 0, 0)),
            pl.BlockSpec((None, 1, 1), lambda b, h, s: (h, 0, 0)),
        ],
        out_specs=pl.BlockSpec((cb * c, RET_V_DIM), lambda b, h, s: (rows(b, h, s), h)),
        scratch_shapes=[pltpu.VMEM((RET_QK_DIM, RET_V_DIM), F32)],
        compiler_params=_params("parallel", "parallel", "arbitrary"),
        name="retention",
    )(qk, qk, v, gates, gn_g, intra, cross_decay, state_decay, chunk_decay)


def _mix_kernel(oa_ref, z_ref, wa_ref, wr_ref, ga_ref, gr_ref, o_ref):
    ya = _mm(oa_ref[...], wa_ref[...])
    yr = _mm(z_ref[...], wr_ref[...])
    o_ref[...] = (_sigmoid(ga_ref[...]) * ya + _sigmoid(gr_ref[...]) * yr).astype(o_ref.dtype)


def _mix(o_att, z, wa, wr, gates, gate_col0):
    n = o_att.shape[0]
    d = wa.shape[1]
    tm, tn = _tile(n, 512), _tile(d, 1024)
    assert gate_col0 % tn == 0
    ga_off = gate_col0 // tn
    gr_off = ga_off + d // tn
    return pl.pallas_call(
        _mix_kernel,
        out_shape=jax.ShapeDtypeStruct((n, d), BF16),
        grid=(d // tn, n // tm),
        in_specs=[pl.BlockSpec((tm, o_att.shape[1]), lambda j, i: (i, 0)),
                  pl.BlockSpec((tm, z.shape[1]), lambda j, i: (i, 0)),
                  pl.BlockSpec((wa.shape[0], tn), lambda j, i: (0, j)),
                  pl.BlockSpec((wr.shape[0], tn), lambda j, i: (0, j)),
                  pl.BlockSpec((tm, tn), lambda j, i: (i, ga_off + j)),
                  pl.BlockSpec((tm, tn), lambda j, i: (i, gr_off + j))],
        out_specs=pl.BlockSpec((tm, tn), lambda j, i: (i, j)),
        compiler_params=_params("parallel", "parallel"),
        name="mix",
    )(o_att, z, wa, wr, gates, gates)


def _layer_norm_rows(t, g, b):
    mu = jnp.mean(t, axis=-1, keepdims=True)
    var = jnp.mean(jnp.square(t - mu), axis=-1, keepdims=True)
    return (t - mu) * lax.rsqrt(var + LN_EPS) * g + b


LN_ROW_CHUNK = 128


def _layer_norm_inplace(o_ref, g_ref, b_ref, cast_ref=None):
    g, b = g_ref[...], b_ref[...]
    chunk = min(LN_ROW_CHUNK, o_ref.shape[0])

    def body(r, carry):
        rows = pl.ds(pl.multiple_of(r * chunk, chunk), chunk)
        y = _layer_norm_rows(o_ref[rows, :], g, b)
        o_ref[rows, :] = y
        if cast_ref is not None:
            cast_ref[rows, :] = y.astype(cast_ref.dtype)
        return carry

    lax.fori_loop(0, o_ref.shape[0] // chunk, body, 0)


def _proj_ln_kernel(a_ref, w_ref, res_ref, g_ref, b_ref, of_ref, ob_ref, *, alpha, tn):
    j = pl.program_id(1)
    col = pl.multiple_of(j * tn, tn)
    of_ref[:, pl.ds(col, tn)] = alpha * res_ref[...] + _mm(a_ref[...], w_ref[...])

    @pl.when(j == pl.num_programs(1) - 1)
    def _():
        _layer_norm_inplace(of_ref, g_ref, b_ref, ob_ref)


def _proj_ln(a, w, res, g, b, alpha):
    n, k = a.shape
    d = w.shape[1]
    tm, tn = _tile(n, 512), _tile(d, 512)
    kern = functools.partial(_proj_ln_kernel, alpha=alpha, tn=tn)
    row = pl.BlockSpec((tm, d), lambda i, j: (i, 0))
    vec = pl.BlockSpec((1, d), lambda i, j: (0, 0))
    return pl.pallas_call(
        kern,
        out_shape=(jax.ShapeDtypeStruct((n, d), F32), jax.ShapeDtypeStruct((n, d), BF16)),
        grid=(n // tm, d // tn),
        in_specs=[pl.BlockSpec((tm, k), lambda i, j: (i, 0)),
                  pl.BlockSpec((k, tn), lambda i, j: (0, j)),
                  pl.BlockSpec((tm, tn), lambda i, j: (i, j)),
                  vec, vec],
        out_specs=(row, row),
        compiler_params=_params("parallel", "arbitrary"),
        name="proj_ln",
    )(a, w, res, g, b)


def _swiglu_kernel(h_ref, wg_ref, wu_ref, o_ref):
    h = h_ref[...]
    gate = _mm(h, wg_ref[...])
    up = _mm(h, wu_ref[...])
    o_ref[...] = ((gate * _sigmoid(gate)) * up).astype(o_ref.dtype)


def _swiglu(h_bf, wg, wu):
    n, k = h_bf.shape
    f = wg.shape[1]
    tm, tn = _tile(n, 2048), _tile(f, 256)
    w_spec = pl.BlockSpec((k, tn), lambda i, j: (0, j))
    return pl.pallas_call(
        _swiglu_kernel,
        out_shape=jax.ShapeDtypeStruct((n, f), BF16),
        grid=(n // tm, f // tn),
        in_specs=[pl.BlockSpec((tm, k), lambda i, j: (i, 0)), w_spec, w_spec],
        out_specs=pl.BlockSpec((tm, tn), lambda i, j: (i, j)),
        compiler_params=_params("parallel", "parallel"),
        name="swiglu",
    )(h_bf, wg, wu)


def _ple_kernel(hb_ref, wg_ref, p_ref, wu_ref, hf_ref, o_ref, *, alpha):
    gate = _sigmoid(_mm(hb_ref[...], wg_ref[...]))
    up = _mm(p_ref[...], wu_ref[...])
    o_ref[...] = alpha * hf_ref[...] + gate * up


def _ple_base(h_bf, h_f32, p_bf, wpg, wpu, alpha):
    n, k = h_bf.shape
    d = wpg.shape[1]
    tm, tn = _tile(n, 512), _tile(d, 1024)
    kern = functools.partial(_ple_kernel, alpha=alpha)
    return pl.pallas_call(
        kern,
        out_shape=jax.ShapeDtypeStruct((n, d), F32),
        grid=(d // tn, n // tm),
        in_specs=[pl.BlockSpec((tm, k), lambda j, i: (i, 0)),
                  pl.BlockSpec((k, tn), lambda j, i: (0, j)),
                  pl.BlockSpec((tm, p_bf.shape[1]), lambda j, i: (i, 0)),
                  pl.BlockSpec((p_bf.shape[1], tn), lambda j, i: (0, j)),
                  pl.BlockSpec((tm, tn), lambda j, i: (i, j))],
        out_specs=pl.BlockSpec((tm, tn), lambda j, i: (i, j)),
        compiler_params=_params("parallel", "parallel"),
        name="ple_base",
    )(h_bf, wpg, p_bf, wpu, h_f32)


def _down_ln_kernel(a_ref, w_ref, base_ref, g_ref, b_ref, o_ref, *, tn):
    j = pl.program_id(1)
    col = pl.multiple_of(j * tn, tn)
    o_ref[:, pl.ds(col, tn)] = base_ref[...] + _mm(a_ref[...], w_ref[...])

    @pl.when(j == pl.num_programs(1) - 1)
    def _():
        _layer_norm_inplace(o_ref, g_ref, b_ref)


def _down_ln(hidden, wd, base, g, b):
    n, f = hidden.shape
    d = wd.shape[1]
    tm, tn = _tile(n, 512), _tile(d, 256)
    kern = functools.partial(_down_ln_kernel, tn=tn)
    vec = pl.BlockSpec((1, d), lambda i, j: (0, 0))
    return pl.pallas_call(
        kern,
        out_shape=jax.ShapeDtypeStruct((n, d), F32),
        grid=(n // tm, d // tn),
        in_specs=[pl.BlockSpec((tm, f), lambda i, j: (i, 0)),
                  pl.BlockSpec((f, tn), lambda i, j: (0, j)),
                  pl.BlockSpec((tm, tn), lambda i, j: (i, j)),
                  vec, vec],
        out_specs=pl.BlockSpec((tm, d), lambda i, j: (i, 0)),
        compiler_params=_params("parallel", "arbitrary"),
        name="down_ln",
    )(hidden, wd, base, g, b)


def _to_units(t, batch, seq, dilation):
    c = t.shape[-1]
    n_blk = seq // (dilation * ATT_BLOCK)
    t = t.reshape(batch, n_blk, ATT_BLOCK, dilation, c).transpose(0, 3, 1, 2, 4)
    return t.reshape(batch * dilation * n_blk, ATT_BLOCK, c)


def _from_units(t, batch, seq, dilation):
    c = t.shape[-1]
    n_blk = seq // (dilation * ATT_BLOCK)
    t = t.reshape(batch, dilation, n_blk, ATT_BLOCK, c).transpose(0, 2, 3, 1, 4)
    return t.reshape(batch * seq, c)


def _layer(h_f32, h_bf, p_bf, w_in, w_attn_out, w_ret_out, ret_gn_g, w_o, ln1_g, ln1_b,
           w_ffn_gate, w_ffn_up, w_ffn_down, w_ple_gate, w_ple_up, ln2_g, ln2_b,
           cos, sin, batch, seq, alpha):
    d_model = h_f32.shape[1]
    c_qr = 3 * ATT_QKV_WIDTH
    c_vr = c_qr + 2 * RET_QK_WIDTH
    c_gr = c_vr + RET_V_WIDTH
    assert w_in.shape[1] == c_gr + RET_V_WIDTH + 2 * d_model

    qkv_a = _proj(h_bf, w_in, 0, c_qr, BF16, "proj_att")
    qk_r = _proj_rot(h_bf, w_in, c_qr, cos, sin, seq)
    v_r = _proj(h_bf, w_in, c_vr, RET_V_WIDTH, BF16, "proj_vr")
    gates = _proj(h_bf, w_in, c_gr, RET_V_WIDTH + 2 * d_model, F32, "proj_gates")

    outs, lses = [], []
    for g, dil in enumerate(ATT_DILATIONS):
        n_blk = seq // (dil * ATT_BLOCK)
        assert n_blk * dil * ATT_BLOCK == seq
        cols = [qkv_a[:, s * ATT_QKV_WIDTH + g * ATT_GROUP_WIDTH:s * ATT_QKV_WIDTH + (g + 1) * ATT_GROUP_WIDTH]
                for s in range(3)]
        q_u, k_u, v_u = [_to_units(t, batch, seq, dil) for t in cols]
        o_u, l_u = _attention_units(q_u, k_u, v_u, n_blk)
        outs.append(_from_units(o_u, batch, seq, dil))
        lses.append(_from_units(l_u, batch, seq, dil))
    o_att = _combine(outs, lses)

    z = _retention(qk_r, v_r, gates, ret_gn_g, batch, seq)

    mixed = _mix(o_att, z, w_attn_out, w_ret_out, gates, RET_V_WIDTH)
    h1_f32, h1_bf = _proj_ln(mixed, w_o, h_f32, ln1_g, ln1_b, alpha)

    hidden = _swiglu(h1_bf, w_ffn_gate, w_ffn_up)
    base = _ple_base(h1_bf, h1_f32, p_bf, w_ple_gate, w_ple_up, alpha)
    return _down_ln(hidden, w_ffn_down, base, ln2_g, ln2_b)


def kernel(x, p, w_in, w_attn_out, w_ret_out, ret_gn_g, w_o, ln1_g, ln1_b, w_ffn_gate, w_ffn_up,
           w_ffn_down, w_ple_gate, w_ple_up, ln2_g, ln2_b):
    batch, seq, d_model = x.shape
    depth = w_in.shape[0]
    n = batch * seq
    alpha = (2 * depth) ** 0.25

    half = RET_QK_DIM // 2
    pos = jnp.arange(seq, dtype=F32)
    inv_freq = RET_ROPE_BASE ** (-jnp.arange(half, dtype=F32) / half)
    ang = pos[:, None] * inv_freq[None, :]
    cos, sin = jnp.cos(ang), jnp.sin(ang)

    h = x.reshape(n, d_model)
    for i in range(depth):
        bf = lambda w: w[i].astype(BF16)
        row = lambda v: v[i].reshape(1, -1)
        h = _layer(h, h.astype(BF16), p[i].reshape(n, -1).astype(BF16),
                   bf(w_in), bf(w_attn_out), bf(w_ret_out), row(ret_gn_g), bf(w_o), row(ln1_g), row(ln1_b),
                   bf(w_ffn_gate), bf(w_ffn_up), bf(w_ffn_down), bf(w_ple_gate), bf(w_ple_up),
                   row(ln2_g), row(ln2_b), cos, sin, batch, seq, alpha)
    return h.reshape(batch, seq, d_model).astype(x.dtype)
```

```python
import functools

import jax
import jax.numpy as jnp
from jax import lax
from jax.experimental import pallas as pl
from jax.experimental.pallas import tpu as pltpu

ATT_HEAD_DIM = 128
ATT_HEADS = 8
ATT_DILATIONS = (1, 4, 16)
ATT_BLOCK = 128
ATT_GROUP_WIDTH = ATT_HEADS * ATT_HEAD_DIM
ATT_QKV_WIDTH = len(ATT_DILATIONS) * ATT_GROUP_WIDTH
RET_HEADS = 8
RET_QK_DIM = 256
RET_V_DIM = 512
RET_QK_WIDTH = RET_HEADS * RET_QK_DIM
RET_V_WIDTH = RET_HEADS * RET_V_DIM
RET_CHUNK = 128
RET_ROPE_BASE = 10000.0
LN_EPS = 1e-5
GN_EPS = 1e-6
NEG_INF = -1e30

V7X_VMEM_LIMIT_BYTES = 56 * 1024 * 1024
LANE = 128

BF16 = jnp.bfloat16
F32 = jnp.float32


def _params(*semantics):
    return pltpu.CompilerParams(dimension_semantics=semantics, vmem_limit_bytes=V7X_VMEM_LIMIT_BYTES)


def _tile(dim, pref):
    if dim <= pref:
        return dim
    t = (pref // LANE) * LANE
    while t > LANE and dim % t:
        t -= LANE
    assert dim % t == 0, (dim, pref)
    return t


def _sigmoid(v):
    return 1.0 / (1.0 + jnp.exp(-v))


def _mm(a, b):
    return jnp.dot(a, b, preferred_element_type=F32)


def _proj_kernel(x_ref, w_ref, o_ref):
    o_ref[...] = _mm(x_ref[...], w_ref[...]).astype(o_ref.dtype)


def _proj(x_bf, w_bf, col0, width, out_dtype, name):
    n, k = x_bf.shape
    tm, tn = _tile(n, 1024), _tile(width, 1024)
    assert col0 % tn == 0
    off = col0 // tn
    return pl.pallas_call(
        _proj_kernel,
        out_shape=jax.ShapeDtypeStruct((n, width), out_dtype),
        grid=(width // tn, n // tm),
        in_specs=[pl.BlockSpec((tm, k), lambda j, i: (i, 0)),
                  pl.BlockSpec((k, tn), lambda j, i: (0, off + j))],
        out_specs=pl.BlockSpec((tm, tn), lambda j, i: (i, j)),
        compiler_params=_params("parallel", "parallel"),
        name=name,
    )(x_bf, w_bf)


def _proj_rot_kernel(x_ref, w_ref, cos_ref, sin_ref, o_ref, *, n_q_blocks, k_scale):
    j = pl.program_id(0)
    acc = _mm(x_ref[...], w_ref[...])
    scale = jnp.where(j >= n_q_blocks, k_scale, 1.0).astype(F32)
    cos = cos_ref[...]
    sin = sin_ref[...]
    half = RET_QK_DIM // 2
    for h in range(acc.shape[1] // RET_QK_DIM):
        lo = h * RET_QK_DIM
        t1 = acc[:, lo:lo + half]
        t2 = acc[:, lo + half:lo + RET_QK_DIM]
        o_ref[:, lo:lo + half] = (t1 * cos - t2 * sin) * scale
        o_ref[:, lo + half:lo + RET_QK_DIM] = (t1 * sin + t2 * cos) * scale


def _proj_rot(x_bf, w_bf, col0, cos, sin, seq):
    n, k = x_bf.shape
    width = 2 * RET_QK_WIDTH
    tm, tn = _tile(seq, 1024), 1024
    assert col0 % tn == 0 and RET_QK_WIDTH % tn == 0 and n % seq == 0
    off = col0 // tn
    pos_blocks = seq // tm
    kern = functools.partial(_proj_rot_kernel, n_q_blocks=RET_QK_WIDTH // tn, k_scale=RET_QK_DIM ** -0.5)
    half = RET_QK_DIM // 2
    return pl.pallas_call(
        kern,
        out_shape=jax.ShapeDtypeStruct((n, width), F32),
        grid=(width // tn, n // tm),
        in_specs=[pl.BlockSpec((tm, k), lambda j, i: (i, 0)),
                  pl.BlockSpec((k, tn), lambda j, i: (0, off + j)),
                  pl.BlockSpec((tm, half), lambda j, i: (i % pos_blocks, 0)),
                  pl.BlockSpec((tm, half), lambda j, i: (i % pos_blocks, 0))],
        out_specs=pl.BlockSpec((tm, tn), lambda j, i: (i, j)),
        compiler_params=_params("parallel", "parallel"),
        name="proj_rot",
    )(x_bf, w_bf, cos, sin)


ATT_SPAN_BLOCKS = 16
ATT_HEADS_PER_STEP = 2


def _proj_units_kernel(x_ref, w_ref, o_ref, acc_ref, *, dilation):
    acc = _mm(x_ref[...], w_ref[...])
    if dilation == 1:
        o_ref[0, 0] = acc.astype(o_ref.dtype)
        return
    rows = acc.shape[0] // dilation
    slabs = acc.shape[1] // LANE
    for s in range(slabs):
        acc_ref[s] = acc[:, s * LANE:(s + 1) * LANE]
    for r in range(dilation):
        for s in range(slabs):
            piece = acc_ref[s, pl.ds(r, rows, stride=dilation), :]
            o_ref[r, 0, :, s * LANE:(s + 1) * LANE] = piece.astype(o_ref.dtype)


def _proj_units(x_bf, w_bf, group, dilation, batch, seq):
    n, k = x_bf.shape
    tm, tn = _tile(seq, 1024), ATT_GROUP_WIDTH
    assert tm % (dilation * 16) == 0 and seq % tm == 0
    rows = tm // dilation
    tiles = seq // tm
    n_groups = len(ATT_DILATIONS)
    kern = functools.partial(_proj_units_kernel, dilation=dilation)
    out = pl.pallas_call(
        kern,
        out_shape=jax.ShapeDtypeStruct((3, batch * dilation, tiles, rows, tn), BF16),
        grid=(3, n // tm),
        in_specs=[pl.BlockSpec((tm, k), lambda j, i: (i, 0)),
                  pl.BlockSpec((k, tn), lambda j, i: (0, j * n_groups + group))],
        out_specs=pl.BlockSpec((None, dilation, 1, rows, tn), lambda j, i: (j, i // tiles, i % tiles, 0, 0)),
        scratch_shapes=[pltpu.VMEM((tn // LANE, tm, LANE), F32)],
        compiler_params=_params("parallel", "parallel"),
        name=f"proj_att_d{dilation}",
    )(x_bf, w_bf)
    n_blk = seq // (dilation * ATT_BLOCK)
    return out.reshape(3, batch * dilation, n_blk, ATT_BLOCK, tn)


def _attn_kernel(q_ref, k_ref, kp_ref, v_ref, vp_ref, o_ref, lse_ref, *, dilation, nb):
    qi = lax.broadcasted_iota(jnp.int32, (ATT_BLOCK, 2 * ATT_BLOCK), 0)
    kj = lax.broadcasted_iota(jnp.int32, (ATT_BLOCK, 2 * ATT_BLOCK), 1)
    in_window = kj <= qi + ATT_BLOCK
    valid_inner = jnp.logical_and(kj >= qi, in_window)
    lo = jnp.where(pl.program_id(1) > 0, 0, ATT_BLOCK)
    valid_first = jnp.logical_and(kj >= jnp.maximum(qi, lo), in_window)
    scale = ATT_HEAD_DIM ** -0.5
    dn = (((1,), (1,)), ((), ()))
    ones = jnp.ones((2 * ATT_BLOCK, ATT_HEAD_DIM), BF16)
    units = [(r, i, h) for r in range(dilation) for i in range(nb) for h in range(ATT_HEADS_PER_STEP)]

    def cols(h):
        return slice(h * ATT_HEAD_DIM, (h + 1) * ATT_HEAD_DIM)

    def with_prev(cur_ref, prev_ref, r, i, h):
        prev = prev_ref[r, 0, :, cols(h)] if i == 0 else cur_ref[r, i - 1, :, cols(h)]
        return jnp.concatenate([prev, cur_ref[r, i, :, cols(h)]], axis=0)

    scores = [lax.dot_general(q_ref[r, i, :, cols(h)], with_prev(k_ref, kp_ref, r, i, h), dn,
                              preferred_element_type=F32) * scale for r, i, h in units]
    probs, maxes = [], []
    for (r, i, h), s in zip(units, scores):
        s = jnp.where(valid_first if i == 0 else valid_inner, s, NEG_INF)
        m = jnp.max(s, axis=-1, keepdims=True)
        probs.append(jnp.exp(s - m).astype(BF16))
        maxes.append(m)
    for (r, i, h), p, m in zip(units, probs, maxes):
        v_aug = jnp.concatenate([with_prev(v_ref, vp_ref, r, i, h), ones], axis=1)
        acc = _mm(p, v_aug)
        den = acc[:, ATT_HEAD_DIM:]
        if dilation == 1:
            rows = pl.ds(i * ATT_BLOCK, ATT_BLOCK)
        else:
            rows = pl.ds(i * ATT_BLOCK * dilation + r, ATT_BLOCK, stride=dilation)
        o_ref[h, rows, :] = acc[:, :ATT_HEAD_DIM] / den
        lse_ref[h, rows, :] = m + jnp.log(den)


def _attention(qkv, dilation, batch, seq):
    n_blk = qkv.shape[2]
    assert ATT_SPAN_BLOCKS % dilation == 0
    nb = ATT_SPAN_BLOCKS // dilation
    assert n_blk % nb == 0
    spans = n_blk // nb
    span_tokens = ATT_SPAN_BLOCKS * ATT_BLOCK
    hw = ATT_HEADS_PER_STEP * ATT_HEAD_DIM
    cur = (None, dilation, nb, ATT_BLOCK, hw)
    one = (None, dilation, 1, ATT_BLOCK, hw)

    def at(part):
        return lambda b, s, hp: (part, b, s, 0, hp)

    def before(part):
        return lambda b, s, hp: (part, b, jnp.maximum(s * nb - 1, 0), 0, hp)

    kern = functools.partial(_attn_kernel, dilation=dilation, nb=nb)
    out = jax.ShapeDtypeStruct((ATT_HEADS, batch * seq, ATT_HEAD_DIM), F32)
    out_spec = pl.BlockSpec((ATT_HEADS_PER_STEP, span_tokens, ATT_HEAD_DIM), lambda b, s, hp: (hp, b * spans + s, 0))
    return pl.pallas_call(
        kern,
        out_shape=(out, out),
        grid=(batch, spans, ATT_HEADS // ATT_HEADS_PER_STEP),
        in_specs=[pl.BlockSpec(cur, at(0)), pl.BlockSpec(cur, at(1)), pl.BlockSpec(one, before(1)),
                  pl.BlockSpec(cur, at(2)), pl.BlockSpec(one, before(2))],
        out_specs=(out_spec, out_spec),
        compiler_params=_params("parallel", "parallel", "parallel"),
        name=f"attn_d{dilation}",
    )(qkv, qkv, qkv, qkv, qkv)


def _combine_kernel(o1, o2, o3, l1, l2, l3, out_ref):
    for h in range(ATT_HEADS):
        a, b, c = l1[h], l2[h], l3[h]
        m = jnp.maximum(jnp.maximum(a, b), c)
        ea, eb, ec = jnp.exp(a - m), jnp.exp(b - m), jnp.exp(c - m)
        s = ea + eb + ec
        out = (ea / s) * o1[h] + (eb / s) * o2[h] + (ec / s) * o3[h]
        out_ref[:, h * ATT_HEAD_DIM:(h + 1) * ATT_HEAD_DIM] = out.astype(out_ref.dtype)


def _combine(os_, ls_):
    heads, n, hd = os_[0].shape
    tm = _tile(n, 512)
    spec = pl.BlockSpec((heads, tm, hd), lambda i: (0, i, 0))
    return pl.pallas_call(
        _combine_kernel,
        out_shape=jax.ShapeDtypeStruct((n, heads * hd), BF16),
        grid=(n // tm,),
        in_specs=[spec] * 6,
        out_specs=pl.BlockSpec((tm, heads * hd), lambda i: (i, 0)),
        compiler_params=_params("parallel"),
        name="attn_combine",
    )(*os_, *ls_)


RET_CHUNKS_PER_STEP = 8


def _ret_kernel(q_ref, k_ref, v_ref, g_ref, gn_ref, intra_ref, cross_ref, sdec_ref, cdec_ref,
                z_ref, state_ref, *, cb):
    @pl.when(pl.program_id(2) == 0)
    def _():
        state_ref[...] = jnp.zeros_like(state_ref)

    c = RET_CHUNK
    intra, cross_decay, state_decay, chunk_decay = intra_ref[...], cross_ref[...], sdec_ref[...], cdec_ref[...]
    gn = gn_ref[...]
    qs, inners, updates = [], [], []
    for t in range(cb):
        rows = slice(t * c, (t + 1) * c)
        q = q_ref[rows, :].astype(BF16)
        k = k_ref[rows, :]
        v = v_ref[rows, :]
        att = lax.dot_general(q, k.astype(BF16), (((1,), (1,)), ((), ())), preferred_element_type=F32) * intra
        inners.append(_mm(att.astype(BF16), v))
        kd = (k * state_decay).astype(BF16)
        updates.append(lax.dot_general(kd, v, (((0,), (0,)), ((), ())), preferred_element_type=F32))
        qs.append(q)
    state = state_ref[...]
    for t in range(cb):
        rows = slice(t * c, (t + 1) * c)
        r = inners[t] + _mm(qs[t], state.astype(BF16)) * cross_decay
        state = chunk_decay * state + updates[t]
        mu = jnp.mean(r, axis=-1, keepdims=True)
        var = jnp.mean(jnp.square(r - mu), axis=-1, keepdims=True)
        rn = (r - mu) * lax.rsqrt(var + GN_EPS) * gn
        g = g_ref[rows, :]
        z_ref[rows, :] = ((g * _sigmoid(g)) * rn).astype(z_ref.dtype)
    state_ref[...] = state


def _retention(qk, v, gates, gn_g, batch, seq):
    n = qk.shape[0]
    c = RET_CHUNK
    nc = seq // c
    cb = min(RET_CHUNKS_PER_STEP, nc)
    assert nc % cb == 0
    steps = nc // cb
    hh = jnp.arange(RET_HEADS, dtype=F32)
    log_gamma = jnp.log(1.0 - 2.0 ** (-5.0 - hh))
    idx = jnp.arange(c, dtype=F32)
    diff = idx[:, None] - idx[None, :]
    intra = jnp.where(diff >= 0, jnp.exp(jnp.maximum(diff, 0.0)[None] * log_gamma[:, None, None]), 0.0)
    cross_decay = jnp.exp((idx + 1.0)[None, :] * log_gamma[:, None])[..., None]
    state_decay = jnp.exp((c - 1.0 - idx)[None, :] * log_gamma[:, None])[..., None]
    chunk_decay = jnp.exp(c * log_gamma)[:, None, None]

    def rows(b, h, s):
        return b * steps + s

    kern = functools.partial(_ret_kernel, cb=cb)
    return pl.pallas_call(
        kern,
        out_shape=jax.ShapeDtypeStruct((n, RET_V_WIDTH), BF16),
        grid=(batch, RET_HEADS, steps),
        in_specs=[
            pl.BlockSpec((cb * c, RET_QK_DIM), lambda b, h, s: (rows(b, h, s), h)),
            pl.BlockSpec((cb * c, RET_QK_DIM), lambda b, h, s: (rows(b, h, s), RET_HEADS + h)),
            pl.BlockSpec((cb * c, RET_V_DIM), lambda b, h, s: (rows(b, h, s), h)),
            pl.BlockSpec((cb * c, RET_V_DIM), lambda b, h, s: (rows(b, h, s), h)),
            pl.BlockSpec((1, RET_V_DIM), lambda b, h, s: (0, h)),
            pl.BlockSpec((None, c, c), lambda b, h, s: (h, 0, 0)),
            pl.BlockSpec((None, c, 1), lambda b, h, s: (h, 0, 0)),
            pl.BlockSpec((None, c, 1), lambda b, h, s: (h, 0, 0)),
            pl.BlockSpec((None, 1, 1), lambda b, h, s: (h, 0, 0)),
        ],
        out_specs=pl.BlockSpec((cb * c, RET_V_DIM), lambda b, h, s: (rows(b, h, s), h)),
        scratch_shapes=[pltpu.VMEM((RET_QK_DIM, RET_V_DIM), F32)],
        compiler_params=_params("parallel", "parallel", "arbitrary"),
        name="retention",
    )(qk, qk, v, gates, gn_g, intra, cross_decay, state_decay, chunk_decay)


def _mix_kernel(oa_ref, z_ref, wa_ref, wr_ref, ga_ref, gr_ref, o_ref):
    ya = _mm(oa_ref[...], wa_ref[...])
    yr = _mm(z_ref[...], wr_ref[...])
    o_ref[...] = (_sigmoid(ga_ref[...]) * ya + _sigmoid(gr_ref[...]) * yr).astype(o_ref.dtype)


def _mix(o_att, z, wa, wr, gates, gate_col0):
    n = o_att.shape[0]
    d = wa.shape[1]
    tm, tn = _tile(n, 512), _tile(d, 1024)
    assert gate_col0 % tn == 0
    ga_off = gate_col0 // tn
    gr_off = ga_off + d // tn
    return pl.pallas_call(
        _mix_kernel,
        out_shape=jax.ShapeDtypeStruct((n, d), BF16),
        grid=(d // tn, n // tm),
        in_specs=[pl.BlockSpec((tm, o_att.shape[1]), lambda j, i: (i, 0)),
                  pl.BlockSpec((tm, z.shape[1]), lambda j, i: (i, 0)),
                  pl.BlockSpec((wa.shape[0], tn), lambda j, i: (0, j)),
                  pl.BlockSpec((wr.shape[0], tn), lambda j, i: (0, j)),
                  pl.BlockSpec((tm, tn), lambda j, i: (i, ga_off + j)),
                  pl.BlockSpec((tm, tn), lambda j, i: (i, gr_off + j))],
        out_specs=pl.BlockSpec((tm, tn), lambda j, i: (i, j)),
        compiler_params=_params("parallel", "parallel"),
        name="mix",
    )(o_att, z, wa, wr, gates, gates)


def _layer_norm_rows(t, g, b):
    mu = jnp.mean(t, axis=-1, keepdims=True)
    var = jnp.mean(jnp.square(t - mu), axis=-1, keepdims=True)
    return (t - mu) * lax.rsqrt(var + LN_EPS) * g + b


LN_ROW_CHUNK = 128


def _layer_norm_inplace(o_ref, g_ref, b_ref, cast_ref=None):
    g, b = g_ref[...], b_ref[...]
    chunk = min(LN_ROW_CHUNK, o_ref.shape[0])

    def body(r, carry):
        rows = pl.ds(pl.multiple_of(r * chunk, chunk), chunk)
        y = _layer_norm_rows(o_ref[rows, :], g, b)
        o_ref[rows, :] = y
        if cast_ref is not None:
            cast_ref[rows, :] = y.astype(cast_ref.dtype)
        return carry

    lax.fori_loop(0, o_ref.shape[0] // chunk, body, 0)


def _proj_ln_kernel(a_ref, w_ref, res_ref, g_ref, b_ref, of_ref, ob_ref, *, alpha, tn):
    j = pl.program_id(1)
    col = pl.multiple_of(j * tn, tn)
    of_ref[:, pl.ds(col, tn)] = alpha * res_ref[...] + _mm(a_ref[...], w_ref[...])

    @pl.when(j == pl.num_programs(1) - 1)
    def _():
        _layer_norm_inplace(of_ref, g_ref, b_ref, ob_ref)


def _proj_ln(a, w, res, g, b, alpha):
    n, k = a.shape
    d = w.shape[1]
    tm, tn = _tile(n, 512), _tile(d, 512)
    kern = functools.partial(_proj_ln_kernel, alpha=alpha, tn=tn)
    row = pl.BlockSpec((tm, d), lambda i, j: (i, 0))
    vec = pl.BlockSpec((1, d), lambda i, j: (0, 0))
    return pl.pallas_call(
        kern,
        out_shape=(jax.ShapeDtypeStruct((n, d), F32), jax.ShapeDtypeStruct((n, d), BF16)),
        grid=(n // tm, d // tn),
        in_specs=[pl.BlockSpec((tm, k), lambda i, j: (i, 0)),
                  pl.BlockSpec((k, tn), lambda i, j: (0, j)),
                  pl.BlockSpec((tm, tn), lambda i, j: (i, j)),
                  vec, vec],
        out_specs=(row, row),
        compiler_params=_params("parallel", "arbitrary"),
        name="proj_ln",
    )(a, w, res, g, b)


def _swiglu_kernel(h_ref, wg_ref, wu_ref, o_ref):
    h = h_ref[...]
    gate = _mm(h, wg_ref[...])
    up = _mm(h, wu_ref[...])
    o_ref[...] = ((gate * _sigmoid(gate)) * up).astype(o_ref.dtype)


def _swiglu(h_bf, wg, wu):
    n, k = h_bf.shape
    f = wg.shape[1]
    tm, tn = _tile(n, 2048), _tile(f, 256)
    w_spec = pl.BlockSpec((k, tn), lambda i, j: (0, j))
    return pl.pallas_call(
        _swiglu_kernel,
        out_shape=jax.ShapeDtypeStruct((n, f), BF16),
        grid=(n // tm, f // tn),
        in_specs=[pl.BlockSpec((tm, k), lambda i, j: (i, 0)), w_spec, w_spec],
        out_specs=pl.BlockSpec((tm, tn), lambda i, j: (i, j)),
        compiler_params=_params("parallel", "parallel"),
        name="swiglu",
    )(h_bf, wg, wu)


def _ple_kernel(hb_ref, wg_ref, p_ref, wu_ref, hf_ref, o_ref, *, alpha):
    gate = _sigmoid(_mm(hb_ref[...], wg_ref[...]))
    up = _mm(p_ref[...], wu_ref[...])
    o_ref[...] = alpha * hf_ref[...] + gate * up


def _ple_base(h_bf, h_f32, p_bf, wpg, wpu, alpha):
    n, k = h_bf.shape
    d = wpg.shape[1]
    tm, tn = _tile(n, 512), _tile(d, 1024)
    kern = functools.partial(_ple_kernel, alpha=alpha)
    return pl.pallas_call(
        kern,
        out_shape=jax.ShapeDtypeStruct((n, d), F32),
        grid=(d // tn, n // tm),
        in_specs=[pl.BlockSpec((tm, k), lambda j, i: (i, 0)),
                  pl.BlockSpec((k, tn), lambda j, i: (0, j)),
                  pl.BlockSpec((tm, p_bf.shape[1]), lambda j, i: (i, 0)),
                  pl.BlockSpec((p_bf.shape[1], tn), lambda j, i: (0, j)),
                  pl.BlockSpec((tm, tn), lambda j, i: (i, j))],
        out_specs=pl.BlockSpec((tm, tn), lambda j, i: (i, j)),
        compiler_params=_params("parallel", "parallel"),
        name="ple_base",
    )(h_bf, wpg, p_bf, wpu, h_f32)


def _down_ln_kernel(a_ref, w_ref, base_ref, g_ref, b_ref, o_ref, *, tn):
    j = pl.program_id(1)
    col = pl.multiple_of(j * tn, tn)
    o_ref[:, pl.ds(col, tn)] = base_ref[...] + _mm(a_ref[...], w_ref[...])

    @pl.when(j == pl.num_programs(1) - 1)
    def _():
        _layer_norm_inplace(o_ref, g_ref, b_ref)


def _down_ln(hidden, wd, base, g, b):
    n, f = hidden.shape
    d = wd.shape[1]
    tm, tn = _tile(n, 512), _tile(d, 256)
    kern = functools.partial(_down_ln_kernel, tn=tn)
    vec = pl.BlockSpec((1, d), lambda i, j: (0, 0))
    return pl.pallas_call(
        kern,
        out_shape=jax.ShapeDtypeStruct((n, d), F32),
        grid=(n // tm, d // tn),
        in_specs=[pl.BlockSpec((tm, f), lambda i, j: (i, 0)),
                  pl.BlockSpec((f, tn), lambda i, j: (0, j)),
                  pl.BlockSpec((tm, tn), lambda i, j: (i, j)),
                  vec, vec],
        out_specs=pl.BlockSpec((tm, d), lambda i, j: (i, 0)),
        compiler_params=_params("parallel", "arbitrary"),
        name="down_ln",
    )(hidden, wd, base, g, b)


def _layer(h_f32, h_bf, p_bf, w_in, w_attn_out, w_ret_out, ret_gn_g, w_o, ln1_g, ln1_b,
           w_ffn_gate, w_ffn_up, w_ffn_down, w_ple_gate, w_ple_up, ln2_g, ln2_b,
           cos, sin, batch, seq, alpha):
    d_model = h_f32.shape[1]
    c_qr = 3 * ATT_QKV_WIDTH
    c_vr = c_qr + 2 * RET_QK_WIDTH
    c_gr = c_vr + RET_V_WIDTH
    assert w_in.shape[1] == c_gr + RET_V_WIDTH + 2 * d_model

    qk_r = _proj_rot(h_bf, w_in, c_qr, cos, sin, seq)
    v_r = _proj(h_bf, w_in, c_vr, RET_V_WIDTH, BF16, "proj_vr")
    gates = _proj(h_bf, w_in, c_gr, RET_V_WIDTH + 2 * d_model, F32, "proj_gates")

    outs, lses = [], []
    for g, dil in enumerate(ATT_DILATIONS):
        o_g, l_g = _attention(_proj_units(h_bf, w_in, g, dil, batch, seq), dil, batch, seq)
        outs.append(o_g)
        lses.append(l_g)
    o_att = _combine(outs, lses)

    z = _retention(qk_r, v_r, gates, ret_gn_g, batch, seq)

    mixed = _mix(o_att, z, w_attn_out, w_ret_out, gates, RET_V_WIDTH)
    h1_f32, h1_bf = _proj_ln(mixed, w_o, h_f32, ln1_g, ln1_b, alpha)

    hidden = _swiglu(h1_bf, w_ffn_gate, w_ffn_up)
    base = _ple_base(h1_bf, h1_f32, p_bf, w_ple_gate, w_ple_up, alpha)
    return _down_ln(hidden, w_ffn_down, base, ln2_g, ln2_b)


def kernel(x, p, w_in, w_attn_out, w_ret_out, ret_gn_g, w_o, ln1_g, ln1_b, w_ffn_gate, w_ffn_up,
           w_ffn_down, w_ple_gate, w_ple_up, ln2_g, ln2_b):
    batch, seq, d_model = x.shape
    depth = w_in.shape[0]
    n = batch * seq
    alpha = (2 * depth) ** 0.25

    half = RET_QK_DIM // 2
    pos = jnp.arange(seq, dtype=F32)
    inv_freq = RET_ROPE_BASE ** (-jnp.arange(half, dtype=F32) / half)
    ang = pos[:, None] * inv_freq[None, :]
    cos, sin = jnp.cos(ang), jnp.sin(ang)

    h = x.reshape(n, d_model)
    for i in range(depth):
        bf = lambda w: w[i].astype(BF16)
        row = lambda v: v[i].reshape(1, -1)
        h = _layer(h, h.astype(BF16), p[i].reshape(n, -1).astype(BF16),
                   bf(w_in), bf(w_attn_out), bf(w_ret_out), row(ret_gn_g), bf(w_o), row(ln1_g), row(ln1_b),
                   bf(w_ffn_gate), bf(w_ffn_up), bf(w_ffn_down), bf(w_ple_gate), bf(w_ple_up),
                   row(ln2_g), row(ln2_b), cos, sin, batch, seq, alpha)
    return h.reshape(batch, seq, d_model).astype(x.dtype)
```

```python
import functools

import jax
import jax.numpy as jnp
from jax import lax
from jax.experimental import pallas as pl
from jax.experimental.pallas import tpu as pltpu

ATT_HEAD_DIM = 128
ATT_HEADS = 8
ATT_DILATIONS = (1, 4, 16)
ATT_BLOCK = 128
ATT_GROUP_WIDTH = ATT_HEADS * ATT_HEAD_DIM
ATT_QKV_WIDTH = len(ATT_DILATIONS) * ATT_GROUP_WIDTH
RET_HEADS = 8
RET_QK_DIM = 256
RET_V_DIM = 512
RET_QK_WIDTH = RET_HEADS * RET_QK_DIM
RET_V_WIDTH = RET_HEADS * RET_V_DIM
RET_CHUNK = 128
RET_ROPE_BASE = 10000.0
LN_EPS = 1e-5
GN_EPS = 1e-6
NEG_INF = -1e30

V7X_VMEM_LIMIT_BYTES = 56 * 1024 * 1024
LANE = 128

BF16 = jnp.bfloat16
F32 = jnp.float32


def _params(*semantics):
    return pltpu.CompilerParams(dimension_semantics=semantics, vmem_limit_bytes=V7X_VMEM_LIMIT_BYTES)


def _tile(dim, pref):
    if dim <= pref:
        return dim
    t = (pref // LANE) * LANE
    while t > LANE and dim % t:
        t -= LANE
    assert dim % t == 0, (dim, pref)
    return t


def _sigmoid(v):
    return 1.0 / (1.0 + jnp.exp(-v))


def _mm(a, b):
    return lax.dot_general(a, b, (((1,), (0,)), ((), ())), preferred_element_type=F32)


def _proj_kernel(x_ref, w_ref, o_ref):
    o_ref[...] = _mm(x_ref[...], w_ref[...]).astype(o_ref.dtype)


def _proj(x_bf, w_bf, col0, width, out_dtype, name):
    n, k = x_bf.shape
    tm, tn = _tile(n, 1024), _tile(width, 1024)
    assert col0 % tn == 0
    off = col0 // tn
    return pl.pallas_call(
        _proj_kernel,
        out_shape=jax.ShapeDtypeStruct((n, width), out_dtype),
        grid=(width // tn, n // tm),
        in_specs=[pl.BlockSpec((tm, k), lambda j, i: (i, 0)),
                  pl.BlockSpec((k, tn), lambda j, i: (0, off + j))],
        out_specs=pl.BlockSpec((tm, tn), lambda j, i: (i, j)),
        compiler_params=_params("parallel", "parallel"),
        name=name,
    )(x_bf, w_bf)


def _proj_rot_kernel(x_ref, w_ref, cos_ref, sin_ref, o_ref, *, n_q_blocks, k_scale):
    j = pl.program_id(0)
    acc = _mm(x_ref[...], w_ref[...])
    scale = jnp.where(j >= n_q_blocks, k_scale, 1.0).astype(F32)
    cos = cos_ref[...]
    sin = sin_ref[...]
    half = RET_QK_DIM // 2
    for h in range(acc.shape[1] // RET_QK_DIM):
        lo = h * RET_QK_DIM
        t1 = acc[:, lo:lo + half]
        t2 = acc[:, lo + half:lo + RET_QK_DIM]
        o_ref[:, lo:lo + half] = (t1 * cos - t2 * sin) * scale
        o_ref[:, lo + half:lo + RET_QK_DIM] = (t1 * sin + t2 * cos) * scale


def _proj_rot(x_bf, w_bf, col0, cos, sin, seq):
    n, k = x_bf.shape
    width = 2 * RET_QK_WIDTH
    tm, tn = _tile(seq, 1024), 1024
    assert col0 % tn == 0 and RET_QK_WIDTH % tn == 0 and n % seq == 0
    off = col0 // tn
    pos_blocks = seq // tm
    kern = functools.partial(_proj_rot_kernel, n_q_blocks=RET_QK_WIDTH // tn, k_scale=RET_QK_DIM ** -0.5)
    half = RET_QK_DIM // 2
    return pl.pallas_call(
        kern,
        out_shape=jax.ShapeDtypeStruct((n, width), F32),
        grid=(width // tn, n // tm),
        in_specs=[pl.BlockSpec((tm, k), lambda j, i: (i, 0)),
                  pl.BlockSpec((k, tn), lambda j, i: (0, off + j)),
                  pl.BlockSpec((tm, half), lambda j, i: (i % pos_blocks, 0)),
                  pl.BlockSpec((tm, half), lambda j, i: (i % pos_blocks, 0))],
        out_specs=pl.BlockSpec((tm, tn), lambda j, i: (i, j)),
        compiler_params=_params("parallel", "parallel"),
        name="proj_rot",
    )(x_bf, w_bf, cos, sin)


ATT_SPAN_BLOCKS = 16
ATT_HEADS_PER_STEP = 2


def _proj_units_kernel(x_ref, w_ref, o_ref, acc_ref, *, dilation):
    acc = _mm(x_ref[...], w_ref[...])
    if dilation == 1:
        o_ref[0, 0] = acc.astype(o_ref.dtype)
        return
    rows = acc.shape[0] // dilation
    slabs = acc.shape[1] // LANE
    for s in range(slabs):
        acc_ref[s] = acc[:, s * LANE:(s + 1) * LANE]
    for r in range(dilation):
        for s in range(slabs):
            piece = acc_ref[s, pl.ds(r, rows, stride=dilation), :]
            o_ref[r, 0, :, s * LANE:(s + 1) * LANE] = piece.astype(o_ref.dtype)


def _proj_units(x_bf, w_bf, group, dilation, batch, seq):
    n, k = x_bf.shape
    tm, tn = _tile(seq, 1024), ATT_GROUP_WIDTH
    assert tm % (dilation * 16) == 0 and seq % tm == 0
    rows = tm // dilation
    tiles = seq // tm
    n_groups = len(ATT_DILATIONS)
    kern = functools.partial(_proj_units_kernel, dilation=dilation)
    out = pl.pallas_call(
        kern,
        out_shape=jax.ShapeDtypeStruct((3, batch * dilation, tiles, rows, tn), BF16),
        grid=(3, n // tm),
        in_specs=[pl.BlockSpec((tm, k), lambda j, i: (i, 0)),
                  pl.BlockSpec((k, tn), lambda j, i: (0, j * n_groups + group))],
        out_specs=pl.BlockSpec((None, dilation, 1, rows, tn), lambda j, i: (j, i // tiles, i % tiles, 0, 0)),
        scratch_shapes=[pltpu.VMEM((tn // LANE, tm, LANE), F32)],
        compiler_params=_params("parallel", "parallel"),
        name=f"proj_att_d{dilation}",
    )(x_bf, w_bf)
    n_blk = seq // (dilation * ATT_BLOCK)
    return out.reshape(3, batch * dilation, n_blk, ATT_BLOCK, tn)


def _attn_kernel(q_ref, k_ref, kp_ref, v_ref, vp_ref, o_ref, lse_ref, *, dilation, nb):
    qi = lax.broadcasted_iota(jnp.int32, (ATT_BLOCK, 2 * ATT_BLOCK), 0)
    kj = lax.broadcasted_iota(jnp.int32, (ATT_BLOCK, 2 * ATT_BLOCK), 1)
    in_window = kj <= qi + ATT_BLOCK
    valid_inner = jnp.logical_and(kj >= qi, in_window)
    lo = jnp.where(pl.program_id(1) > 0, 0, ATT_BLOCK)
    valid_first = jnp.logical_and(kj >= jnp.maximum(qi, lo), in_window)
    scale = ATT_HEAD_DIM ** -0.5
    dn = (((1,), (1,)), ((), ()))
    ones = jnp.ones((2 * ATT_BLOCK, ATT_HEAD_DIM), BF16)
    units = [(r, i, h) for r in range(dilation) for i in range(nb) for h in range(ATT_HEADS_PER_STEP)]

    def cols(h):
        return slice(h * ATT_HEAD_DIM, (h + 1) * ATT_HEAD_DIM)

    def with_prev(cur_ref, prev_ref, r, i, h):
        prev = prev_ref[r, 0, :, cols(h)] if i == 0 else cur_ref[r, i - 1, :, cols(h)]
        return jnp.concatenate([prev, cur_ref[r, i, :, cols(h)]], axis=0)

    scores = [lax.dot_general(q_ref[r, i, :, cols(h)], with_prev(k_ref, kp_ref, r, i, h), dn,
                              preferred_element_type=F32) * scale for r, i, h in units]
    probs, maxes = [], []
    for (r, i, h), s in zip(units, scores):
        s = jnp.where(valid_first if i == 0 else valid_inner, s, NEG_INF)
        m = jnp.max(s, axis=-1, keepdims=True)
        probs.append(jnp.exp(s - m).astype(BF16))
        maxes.append(m)
    for (r, i, h), p, m in zip(units, probs, maxes):
        v_aug = jnp.concatenate([with_prev(v_ref, vp_ref, r, i, h), ones], axis=1)
        acc = _mm(p, v_aug)
        den = acc[:, ATT_HEAD_DIM:]
        if dilation == 1:
            rows = pl.ds(i * ATT_BLOCK, ATT_BLOCK)
        else:
            rows = pl.ds(i * ATT_BLOCK * dilation + r, ATT_BLOCK, stride=dilation)
        o_ref[h, rows, :] = acc[:, :ATT_HEAD_DIM] / den
        lse_ref[h, rows, :] = m + jnp.log(den)


def _attention(qkv, dilation, batch, seq):
    n_blk = qkv.shape[2]
    assert ATT_SPAN_BLOCKS % dilation == 0
    nb = ATT_SPAN_BLOCKS // dilation
    assert n_blk % nb == 0
    spans = n_blk // nb
    span_tokens = ATT_SPAN_BLOCKS * ATT_BLOCK
    hw = ATT_HEADS_PER_STEP * ATT_HEAD_DIM
    cur = (None, dilation, nb, ATT_BLOCK, hw)
    one = (None, dilation, 1, ATT_BLOCK, hw)

    def at(part):
        return lambda b, s, hp: (part, b, s, 0, hp)

    def before(part):
        return lambda b, s, hp: (part, b, jnp.maximum(s * nb - 1, 0), 0, hp)

    kern = functools.partial(_attn_kernel, dilation=dilation, nb=nb)
    out = jax.ShapeDtypeStruct((ATT_HEADS, batch * seq, ATT_HEAD_DIM), F32)
    out_spec = pl.BlockSpec((ATT_HEADS_PER_STEP, span_tokens, ATT_HEAD_DIM), lambda b, s, hp: (hp, b * spans + s, 0))
    return pl.pallas_call(
        kern,
        out_shape=(out, out),
        grid=(batch, spans, ATT_HEADS // ATT_HEADS_PER_STEP),
        in_specs=[pl.BlockSpec(cur, at(0)), pl.BlockSpec(cur, at(1)), pl.BlockSpec(one, before(1)),
                  pl.BlockSpec(cur, at(2)), pl.BlockSpec(one, before(2))],
        out_specs=(out_spec, out_spec),
        compiler_params=_params("parallel", "parallel", "parallel"),
        name=f"attn_d{dilation}",
    )(qkv, qkv, qkv, qkv, qkv)


def _combine_kernel(o1, o2, o3, l1, l2, l3, out_ref):
    for h in range(ATT_HEADS):
        a, b, c = l1[h], l2[h], l3[h]
        m = jnp.maximum(jnp.maximum(a, b), c)
        ea, eb, ec = jnp.exp(a - m), jnp.exp(b - m), jnp.exp(c - m)
        s = ea + eb + ec
        out = (ea / s) * o1[h] + (eb / s) * o2[h] + (ec / s) * o3[h]
        out_ref[:, h * ATT_HEAD_DIM:(h + 1) * ATT_HEAD_DIM] = out.astype(out_ref.dtype)


def _combine(os_, ls_):
    heads, n, hd = os_[0].shape
    tm = _tile(n, 512)
    spec = pl.BlockSpec((heads, tm, hd), lambda i: (0, i, 0))
    return pl.pallas_call(
        _combine_kernel,
        out_shape=jax.ShapeDtypeStruct((n, heads * hd), BF16),
        grid=(n // tm,),
        in_specs=[spec] * 6,
        out_specs=pl.BlockSpec((tm, heads * hd), lambda i: (i, 0)),
        compiler_params=_params("parallel"),
        name="attn_combine",
    )(*os_, *ls_)


RET_CHUNKS_PER_STEP = 8


def _ret_kernel(q_ref, k_ref, v_ref, g_ref, gn_ref, intra_ref, cross_ref, sdec_ref, cdec_ref,
                z_ref, state_ref, *, cb):
    @pl.when(pl.program_id(2) == 0)
    def _():
        state_ref[...] = jnp.zeros_like(state_ref)

    c = RET_CHUNK
    intra, cross_decay, state_decay, chunk_decay = intra_ref[...], cross_ref[...], sdec_ref[...], cdec_ref[...]
    gn = gn_ref[...]
    qs, inners, updates = [], [], []
    for t in range(cb):
        rows = slice(t * c, (t + 1) * c)
        q = q_ref[rows, :].astype(BF16)
        k = k_ref[rows, :]
        v = v_ref[rows, :]
        att = lax.dot_general(q, k.astype(BF16), (((1,), (1,)), ((), ())), preferred_element_type=F32) * intra
        inners.append(_mm(att.astype(BF16), v))
        kd = (k * state_decay).astype(BF16)
        updates.append(lax.dot_general(kd, v, (((0,), (0,)), ((), ())), preferred_element_type=F32))
        qs.append(q)
    state = state_ref[...]
    for t in range(cb):
        rows = slice(t * c, (t + 1) * c)
        r = inners[t] + _mm(qs[t], state.astype(BF16)) * cross_decay
        state = chunk_decay * state + updates[t]
        mu = jnp.mean(r, axis=-1, keepdims=True)
        var = jnp.mean(jnp.square(r - mu), axis=-1, keepdims=True)
        rn = (r - mu) * lax.rsqrt(var + GN_EPS) * gn
        g = g_ref[rows, :]
        z_ref[rows, :] = ((g * _sigmoid(g)) * rn).astype(z_ref.dtype)
    state_ref[...] = state


def _retention(qk, v, gates, gn_g, batch, seq):
    n = qk.shape[0]
    c = RET_CHUNK
    nc = seq // c
    cb = min(RET_CHUNKS_PER_STEP, nc)
    assert nc % cb == 0
    steps = nc // cb
    hh = jnp.arange(RET_HEADS, dtype=F32)
    log_gamma = jnp.log(1.0 - 2.0 ** (-5.0 - hh))
    idx = jnp.arange(c, dtype=F32)
    diff = idx[:, None] - idx[None, :]
    intra = jnp.where(diff >= 0, jnp.exp(jnp.maximum(diff, 0.0)[None] * log_gamma[:, None, None]), 0.0)
    cross_decay = jnp.exp((idx + 1.0)[None, :] * log_gamma[:, None])[..., None]
    state_decay = jnp.exp((c - 1.0 - idx)[None, :] * log_gamma[:, None])[..., None]
    chunk_decay = jnp.exp(c * log_gamma)[:, None, None]

    def rows(b, h, s):
        return b * steps + s

    kern = functools.partial(_ret_kernel, cb=cb)
    return pl.pallas_call(
        kern,
        out_shape=jax.ShapeDtypeStruct((n, RET_V_WIDTH), BF16),
        grid=(batch, RET_HEADS, steps),
        in_specs=[
            pl.BlockSpec((cb * c, RET_QK_DIM), lambda b, h, s: (rows(b, h, s), h)),
            pl.BlockSpec((cb * c, RET_QK_DIM), lambda b, h, s: (rows(b, h, s), RET_HEADS + h)),
            pl.BlockSpec((cb * c, RET_V_DIM), lambda b, h, s: (rows(b, h, s), h)),
            pl.BlockSpec((cb * c, RET_V_DIM), lambda b, h, s: (rows(b, h, s), h)),
            pl.BlockSpec((1, RET_V_DIM), lambda b, h, s: (0, h)),
            pl.BlockSpec((None, c, c), lambda b, h, s: (h, 0, 0)),
            pl.BlockSpec((None, c, 1), lambda b, h, s: (h, 0, 0)),
            pl.BlockSpec((None, c, 1), lambda b, h, s: (h, 0, 0)),
            pl.BlockSpec((None, 1, 1), lambda b, h, s: (h, 0, 0)),
        ],
        out_specs=pl.BlockSpec((cb * c, RET_V_DIM), lambda b, h, s: (rows(b, h, s), h)),
        scratch_shapes=[pltpu.VMEM((RET_QK_DIM, RET_V_DIM), F32)],
        compiler_params=_params("parallel", "parallel", "arbitrary"),
        name="retention",
    )(qk, qk, v, gates, gn_g, intra, cross_decay, state_decay, chunk_decay)


def _mix_kernel(oa_ref, z_ref, wa_ref, wr_ref, ga_ref, gr_ref, o_ref):
    ya = _mm(oa_ref[...], wa_ref[...])
    yr = _mm(z_ref[...], wr_ref[...])
    o_ref[...] = (_sigmoid(ga_ref[...]) * ya + _sigmoid(gr_ref[...]) * yr).astype(o_ref.dtype)


def _mix(o_att, z, wa, wr, gates, gate_col0):
    n = o_att.shape[0]
    d = wa.shape[1]
    tm, tn = _tile(n, 512), _tile(d, 1024)
    assert gate_col0 % tn == 0
    ga_off = gate_col0 // tn
    gr_off = ga_off + d // tn
    return pl.pallas_call(
        _mix_kernel,
        out_shape=jax.ShapeDtypeStruct((n, d), BF16),
        grid=(d // tn, n // tm),
        in_specs=[pl.BlockSpec((tm, o_att.shape[1]), lambda j, i: (i, 0)),
                  pl.BlockSpec((tm, z.shape[1]), lambda j, i: (i, 0)),
                  pl.BlockSpec((wa.shape[0], tn), lambda j, i: (0, j)),
                  pl.BlockSpec((wr.shape[0], tn), lambda j, i: (0, j)),
                  pl.BlockSpec((tm, tn), lambda j, i: (i, ga_off + j)),
                  pl.BlockSpec((tm, tn), lambda j, i: (i, gr_off + j))],
        out_specs=pl.BlockSpec((tm, tn), lambda j, i: (i, j)),
        compiler_params=_params("parallel", "parallel"),
        name="mix",
    )(o_att, z, wa, wr, gates, gates)


def _layer_norm_rows(t, g, b):
    mu = jnp.mean(t, axis=-1, keepdims=True)
    var = jnp.mean(jnp.square(t - mu), axis=-1, keepdims=True)
    return (t - mu) * lax.rsqrt(var + LN_EPS) * g + b


LN_ROW_CHUNK = 128


def _layer_norm_inplace(o_ref, g_ref, b_ref, cast_ref=None):
    g, b = g_ref[...], b_ref[...]
    chunk = min(LN_ROW_CHUNK, o_ref.shape[0])

    def body(r, carry):
        rows = pl.ds(pl.multiple_of(r * chunk, chunk), chunk)
        y = _layer_norm_rows(o_ref[rows, :], g, b)
        o_ref[rows, :] = y
        if cast_ref is not None:
            cast_ref[rows, :] = y.astype(cast_ref.dtype)
        return carry

    lax.fori_loop(0, o_ref.shape[0] // chunk, body, 0)


def _proj_ln_kernel(a_ref, w_ref, res_ref, g_ref, b_ref, of_ref, ob_ref, *, alpha):
    y = _layer_norm_rows(alpha * res_ref[...] + _mm(a_ref[...], w_ref[...]), g_ref[...], b_ref[...])
    of_ref[...] = y
    ob_ref[...] = y.astype(ob_ref.dtype)


def _proj_ln(a, w, res, g, b, alpha):
    n, k = a.shape
    d = w.shape[1]
    tm = _tile(n, 128)
    kern = functools.partial(_proj_ln_kernel, alpha=alpha)
    row = pl.BlockSpec((tm, d), lambda i: (i, 0))
    vec = pl.BlockSpec((1, d), lambda i: (0, 0))
    return pl.pallas_call(
        kern,
        out_shape=(jax.ShapeDtypeStruct((n, d), F32), jax.ShapeDtypeStruct((n, d), BF16)),
        grid=(n // tm,),
        in_specs=[pl.BlockSpec((tm, k), lambda i: (i, 0)),
                  pl.BlockSpec((k, d), lambda i: (0, 0), pipeline_mode=pl.Buffered(1)),
                  row, vec, vec],
        out_specs=(row, row),
        compiler_params=_params("parallel"),
        name="proj_ln",
    )(a, w, res, g, b)


def _swiglu_kernel(h_ref, wg_ref, wu_ref, o_ref):
    h = h_ref[...]
    gate = _mm(h, wg_ref[...])
    up = _mm(h, wu_ref[...])
    o_ref[...] = ((gate * _sigmoid(gate)) * up).astype(o_ref.dtype)


def _swiglu(h_bf, wg, wu):
    n, k = h_bf.shape
    f = wg.shape[1]
    tm, tn = _tile(n, 2048), _tile(f, 256)
    w_spec = pl.BlockSpec((k, tn), lambda i, j: (0, j))
    return pl.pallas_call(
        _swiglu_kernel,
        out_shape=jax.ShapeDtypeStruct((n, f), BF16),
        grid=(n // tm, f // tn),
        in_specs=[pl.BlockSpec((tm, k), lambda i, j: (i, 0)), w_spec, w_spec],
        out_specs=pl.BlockSpec((tm, tn), lambda i, j: (i, j)),
        compiler_params=_params("parallel", "parallel"),
        name="swiglu",
    )(h_bf, wg, wu)


def _ple_kernel(hb_ref, wg_ref, p_ref, wu_ref, hf_ref, o_ref, *, alpha):
    gate = _sigmoid(_mm(hb_ref[...], wg_ref[...]))
    up = _mm(p_ref[...], wu_ref[...])
    o_ref[...] = alpha * hf_ref[...] + gate * up


def _ple_base(h_bf, h_f32, p_bf, wpg, wpu, alpha):
    n, k = h_bf.shape
    d = wpg.shape[1]
    tm, tn = _tile(n, 512), _tile(d, 1024)
    kern = functools.partial(_ple_kernel, alpha=alpha)
    return pl.pallas_call(
        kern,
        out_shape=jax.ShapeDtypeStruct((n, d), F32),
        grid=(d // tn, n // tm),
        in_specs=[pl.BlockSpec((tm, k), lambda j, i: (i, 0)),
                  pl.BlockSpec((k, tn), lambda j, i: (0, j)),
                  pl.BlockSpec((tm, p_bf.shape[1]), lambda j, i: (i, 0)),
                  pl.BlockSpec((p_bf.shape[1], tn), lambda j, i: (0, j)),
                  pl.BlockSpec((tm, tn), lambda j, i: (i, j))],
        out_specs=pl.BlockSpec((tm, tn), lambda j, i: (i, j)),
        compiler_params=_params("parallel", "parallel"),
        name="ple_base",
    )(h_bf, wpg, p_bf, wpu, h_f32)


def _down_ln_kernel(a_ref, w_ref, base_ref, g_ref, b_ref, o_ref, *, tn):
    j = pl.program_id(1)
    col = pl.multiple_of(j * tn, tn)
    o_ref[:, pl.ds(col, tn)] = base_ref[...] + _mm(a_ref[...], w_ref[...])

    @pl.when(j == pl.num_programs(1) - 1)
    def _():
        _layer_norm_inplace(o_ref, g_ref, b_ref)


def _down_ln(hidden, wd, base, g, b):
    n, f = hidden.shape
    d = wd.shape[1]
    tm, tn = _tile(n, 512), _tile(d, 256)
    kern = functools.partial(_down_ln_kernel, tn=tn)
    vec = pl.BlockSpec((1, d), lambda i, j: (0, 0))
    return pl.pallas_call(
        kern,
        out_shape=jax.ShapeDtypeStruct((n, d), F32),
        grid=(n // tm, d // tn),
        in_specs=[pl.BlockSpec((tm, f), lambda i, j: (i, 0)),
                  pl.BlockSpec((f, tn), lambda i, j: (0, j)),
                  pl.BlockSpec((tm, tn), lambda i, j: (i, j)),
                  vec, vec],
        out_specs=pl.BlockSpec((tm, d), lambda i, j: (i, 0)),
        compiler_params=_params("parallel", "arbitrary"),
        name="down_ln",
    )(hidden, wd, base, g, b)


def _layer(h_f32, h_bf, p_bf, w_in, w_attn_out, w_ret_out, ret_gn_g, w_o, ln1_g, ln1_b,
           w_ffn_gate, w_ffn_up, w_ffn_down, w_ple_gate, w_ple_up, ln2_g, ln2_b,
           cos, sin, batch, seq, alpha):
    d_model = h_f32.shape[1]
    c_qr = 3 * ATT_QKV_WIDTH
    c_vr = c_qr + 2 * RET_QK_WIDTH
    c_gr = c_vr + RET_V_WIDTH
    assert w_in.shape[1] == c_gr + RET_V_WIDTH + 2 * d_model

    qk_r = _proj_rot(h_bf, w_in, c_qr, cos, sin, seq)
    v_r = _proj(h_bf, w_in, c_vr, RET_V_WIDTH, BF16, "proj_vr")
    gates = _proj(h_bf, w_in, c_gr, RET_V_WIDTH + 2 * d_model, F32, "proj_gates")

    outs, lses = [], []
    for g, dil in enumerate(ATT_DILATIONS):
        o_g, l_g = _attention(_proj_units(h_bf, w_in, g, dil, batch, seq), dil, batch, seq)
        outs.append(o_g)
        lses.append(l_g)
    o_att = _combine(outs, lses)

    z = _retention(qk_r, v_r, gates, ret_gn_g, batch, seq)

    mixed = _mix(o_att, z, w_attn_out, w_ret_out, gates, RET_V_WIDTH)
    h1_f32, h1_bf = _proj_ln(mixed, w_o, h_f32, ln1_g, ln1_b, alpha)

    hidden = _swiglu(h1_bf, w_ffn_gate, w_ffn_up)
    base = _ple_base(h1_bf, h1_f32, p_bf, w_ple_gate, w_ple_up, alpha)
    return _down_ln(hidden, w_ffn_down, base, ln2_g, ln2_b)


def kernel(x, p, w_in, w_attn_out, w_ret_out, ret_gn_g, w_o, ln1_g, ln1_b, w_ffn_gate, w_ffn_up,
           w_ffn_down, w_ple_gate, w_ple_up, ln2_g, ln2_b):
    batch, seq, d_model = x.shape
    depth = w_in.shape[0]
    n = batch * seq
    alpha = (2 * depth) ** 0.25

    half = RET_QK_DIM // 2
    pos = jnp.arange(seq, dtype=F32)
    inv_freq = RET_ROPE_BASE ** (-jnp.arange(half, dtype=F32) / half)
    ang = pos[:, None] * inv_freq[None, :]
    cos, sin = jnp.cos(ang), jnp.sin(ang)

    h = x.reshape(n, d_model)
    for i in range(depth):
        bf = lambda w: w[i].astype(BF16)
        row = lambda v: v[i].reshape(1, -1)
        h = _layer(h, h.astype(BF16), p[i].reshape(n, -1).astype(BF16),
                   bf(w_in), bf(w_attn_out), bf(w_ret_out), row(ret_gn_g), bf(w_o), row(ln1_g), row(ln1_b),
                   w_ffn_gate[i], w_ffn_up[i], bf(w_ffn_down), bf(w_ple_gate), bf(w_ple_up),
                   row(ln2_g), row(ln2_b), cos, sin, batch, seq, alpha)
    return h.reshape(batch, seq, d_model).astype(x.dtype)
```

```python
import functools

import jax
import jax.numpy as jnp
from jax import lax
from jax.experimental import pallas as pl
from jax.experimental.pallas import tpu as pltpu

ATT_HEAD_DIM = 128
ATT_HEADS = 8
ATT_DILATIONS = (1, 4, 16)
ATT_BLOCK = 128
ATT_GROUP_WIDTH = ATT_HEADS * ATT_HEAD_DIM
ATT_QKV_WIDTH = len(ATT_DILATIONS) * ATT_GROUP_WIDTH
RET_HEADS = 8
RET_QK_DIM = 256
RET_V_DIM = 512
RET_QK_WIDTH = RET_HEADS * RET_QK_DIM
RET_V_WIDTH = RET_HEADS * RET_V_DIM
RET_CHUNK = 128
RET_ROPE_BASE = 10000.0
LN_EPS = 1e-5
GN_EPS = 1e-6
NEG_INF = -1e30

V7X_VMEM_LIMIT_BYTES = 56 * 1024 * 1024
LANE = 128

BF16 = jnp.bfloat16
F32 = jnp.float32


def _params(*semantics):
    return pltpu.CompilerParams(dimension_semantics=semantics, vmem_limit_bytes=V7X_VMEM_LIMIT_BYTES)


def _tile(dim, pref):
    if dim <= pref:
        return dim
    t = (pref // LANE) * LANE
    while t > LANE and dim % t:
        t -= LANE
    assert dim % t == 0, (dim, pref)
    return t


def _sigmoid(v):
    return 1.0 / (1.0 + jnp.exp(-v))


def _mm(a, b):
    return lax.dot_general(a, b, (((1,), (0,)), ((), ())), preferred_element_type=F32)


def _proj_kernel(x_ref, w_ref, o_ref):
    o_ref[...] = _mm(x_ref[...], w_ref[...]).astype(o_ref.dtype)


def _proj(x_bf, w_bf, col0, width, out_dtype, name):
    n, k = x_bf.shape
    tm, tn = _tile(n, 1024), _tile(width, 1024)
    assert col0 % tn == 0
    off = col0 // tn
    return pl.pallas_call(
        _proj_kernel,
        out_shape=jax.ShapeDtypeStruct((n, width), out_dtype),
        grid=(width // tn, n // tm),
        in_specs=[pl.BlockSpec((tm, k), lambda j, i: (i, 0)),
                  pl.BlockSpec((k, tn), lambda j, i: (0, off + j))],
        out_specs=pl.BlockSpec((tm, tn), lambda j, i: (i, j)),
        compiler_params=_params("parallel", "parallel"),
        name=name,
    )(x_bf, w_bf)


def _proj_rot_kernel(x_ref, w_ref, cos_ref, sin_ref, o_ref, *, n_q_blocks, k_scale):
    j = pl.program_id(0)
    acc = _mm(x_ref[...], w_ref[...])
    scale = jnp.where(j >= n_q_blocks, k_scale, 1.0).astype(F32)
    cos = cos_ref[...]
    sin = sin_ref[...]
    half = RET_QK_DIM // 2
    for h in range(acc.shape[1] // RET_QK_DIM):
        lo = h * RET_QK_DIM
        t1 = acc[:, lo:lo + half]
        t2 = acc[:, lo + half:lo + RET_QK_DIM]
        o_ref[:, lo:lo + half] = (t1 * cos - t2 * sin) * scale
        o_ref[:, lo + half:lo + RET_QK_DIM] = (t1 * sin + t2 * cos) * scale


def _proj_rot(x_bf, w_bf, col0, cos, sin, seq):
    n, k = x_bf.shape
    width = 2 * RET_QK_WIDTH
    tm, tn = _tile(seq, 1024), 1024
    assert col0 % tn == 0 and RET_QK_WIDTH % tn == 0 and n % seq == 0
    off = col0 // tn
    pos_blocks = seq // tm
    kern = functools.partial(_proj_rot_kernel, n_q_blocks=RET_QK_WIDTH // tn, k_scale=RET_QK_DIM ** -0.5)
    half = RET_QK_DIM // 2
    return pl.pallas_call(
        kern,
        out_shape=jax.ShapeDtypeStruct((n, width), F32),
        grid=(width // tn, n // tm),
        in_specs=[pl.BlockSpec((tm, k), lambda j, i: (i, 0)),
                  pl.BlockSpec((k, tn), lambda j, i: (0, off + j)),
                  pl.BlockSpec((tm, half), lambda j, i: (i % pos_blocks, 0)),
                  pl.BlockSpec((tm, half), lambda j, i: (i % pos_blocks, 0))],
        out_specs=pl.BlockSpec((tm, tn), lambda j, i: (i, j)),
        compiler_params=_params("parallel", "parallel"),
        name="proj_rot",
    )(x_bf, w_bf, cos, sin)


ATT_SPAN_BLOCKS = 16
ATT_HEADS_PER_STEP = 2


def _proj_units_kernel(x_ref, w_ref, o_ref, acc_ref, *, dilation):
    acc = _mm(x_ref[...], w_ref[...])
    if dilation == 1:
        o_ref[0, 0] = acc.astype(o_ref.dtype)
        return
    rows = acc.shape[0] // dilation
    slabs = acc.shape[1] // LANE
    for s in range(slabs):
        acc_ref[s] = acc[:, s * LANE:(s + 1) * LANE]
    for r in range(dilation):
        for s in range(slabs):
            piece = acc_ref[s, pl.ds(r, rows, stride=dilation), :]
            o_ref[r, 0, :, s * LANE:(s + 1) * LANE] = piece.astype(o_ref.dtype)


def _proj_units(x_bf, w_bf, group, dilation, batch, seq):
    n, k = x_bf.shape
    tm, tn = _tile(seq, 1024), ATT_GROUP_WIDTH
    assert tm % (dilation * 16) == 0 and seq % tm == 0
    rows = tm // dilation
    tiles = seq // tm
    n_groups = len(ATT_DILATIONS)
    kern = functools.partial(_proj_units_kernel, dilation=dilation)
    out = pl.pallas_call(
        kern,
        out_shape=jax.ShapeDtypeStruct((3, batch * dilation, tiles, rows, tn), BF16),
        grid=(3, n // tm),
        in_specs=[pl.BlockSpec((tm, k), lambda j, i: (i, 0)),
                  pl.BlockSpec((k, tn), lambda j, i: (0, j * n_groups + group))],
        out_specs=pl.BlockSpec((None, dilation, 1, rows, tn), lambda j, i: (j, i // tiles, i % tiles, 0, 0)),
        scratch_shapes=[pltpu.VMEM((tn // LANE, tm, LANE), F32)],
        compiler_params=_params("parallel", "parallel"),
        name=f"proj_att_d{dilation}",
    )(x_bf, w_bf)
    n_blk = seq // (dilation * ATT_BLOCK)
    return out.reshape(3, batch * dilation, n_blk, ATT_BLOCK, tn)


def _attn_span(q_ref, k_ref, kp_ref, v_ref, vp_ref, o_ref, lse_ref, *, dilation, nb):
    qi = lax.broadcasted_iota(jnp.int32, (ATT_BLOCK, 2 * ATT_BLOCK), 0)
    kj = lax.broadcasted_iota(jnp.int32, (ATT_BLOCK, 2 * ATT_BLOCK), 1)
    in_window = kj <= qi + ATT_BLOCK
    valid_inner = jnp.logical_and(kj >= qi, in_window)
    lo = jnp.where(pl.program_id(1) > 0, 0, ATT_BLOCK)
    valid_first = jnp.logical_and(kj >= jnp.maximum(qi, lo), in_window)
    scale = ATT_HEAD_DIM ** -0.5
    dn = (((1,), (1,)), ((), ()))
    ones = jnp.ones((2 * ATT_BLOCK, ATT_HEAD_DIM), BF16)
    units = [(r, i, h) for r in range(dilation) for i in range(nb) for h in range(ATT_HEADS_PER_STEP)]

    def cols(h):
        return slice(h * ATT_HEAD_DIM, (h + 1) * ATT_HEAD_DIM)

    def with_prev(cur_ref, prev_ref, r, i, h):
        prev = prev_ref[r, 0, :, cols(h)] if i == 0 else cur_ref[r, i - 1, :, cols(h)]
        return jnp.concatenate([prev, cur_ref[r, i, :, cols(h)]], axis=0)

    scores = [lax.dot_general(q_ref[r, i, :, cols(h)], with_prev(k_ref, kp_ref, r, i, h), dn,
                              preferred_element_type=F32) * scale for r, i, h in units]
    probs, maxes = [], []
    for (r, i, h), s in zip(units, scores):
        s = jnp.where(valid_first if i == 0 else valid_inner, s, NEG_INF)
        m = jnp.max(s, axis=-1, keepdims=True)
        probs.append(jnp.exp(s - m).astype(BF16))
        maxes.append(m)
    for (r, i, h), p, m in zip(units, probs, maxes):
        v_aug = jnp.concatenate([with_prev(v_ref, vp_ref, r, i, h), ones], axis=1)
        acc = _mm(p, v_aug)
        den = acc[:, ATT_HEAD_DIM:]
        if dilation == 1:
            rows = pl.ds(i * ATT_BLOCK, ATT_BLOCK)
        else:
            rows = pl.ds(i * ATT_BLOCK * dilation + r, ATT_BLOCK, stride=dilation)
        o_ref[h, rows, :] = acc[:, :ATT_HEAD_DIM] / den
        lse_ref[h, rows, :] = m + jnp.log(den)


def _attn_kernel(q_ref, k_ref, kp_ref, v_ref, vp_ref, o_ref, lse_ref, *, dilation, nb):
    _attn_span(q_ref, k_ref, kp_ref, v_ref, vp_ref, o_ref, lse_ref, dilation=dilation, nb=nb)


ATT_COMBINE_ROWS = 256


def _attn_combine_kernel(q_ref, k_ref, kp_ref, v_ref, vp_ref, o1, l1, o2, l2, out_ref, o3, l3, *, dilation, nb):
    _attn_span(q_ref, k_ref, kp_ref, v_ref, vp_ref, o3, l3, dilation=dilation, nb=nb)

    def body(c, carry):
        rows = pl.ds(pl.multiple_of(c * ATT_COMBINE_ROWS, ATT_COMBINE_ROWS), ATT_COMBINE_ROWS)
        for h in range(ATT_HEADS_PER_STEP):
            a, b, d = l1[h, rows, :], l2[h, rows, :], l3[h, rows, :]
            m = jnp.maximum(jnp.maximum(a, b), d)
            ea, eb, ed = jnp.exp(a - m), jnp.exp(b - m), jnp.exp(d - m)
            s = ea + eb + ed
            out = (ea / s) * o1[h, rows, :] + (eb / s) * o2[h, rows, :] + (ed / s) * o3[h, rows, :]
            out_ref[rows, h * ATT_HEAD_DIM:(h + 1) * ATT_HEAD_DIM] = out.astype(out_ref.dtype)
        return carry

    lax.fori_loop(0, out_ref.shape[0] // ATT_COMBINE_ROWS, body, 0)


def _attention(qkv, dilation, batch, seq, others=None):
    n_blk = qkv.shape[2]
    assert ATT_SPAN_BLOCKS % dilation == 0
    nb = ATT_SPAN_BLOCKS // dilation
    assert n_blk % nb == 0
    spans = n_blk // nb
    span_tokens = ATT_SPAN_BLOCKS * ATT_BLOCK
    hw = ATT_HEADS_PER_STEP * ATT_HEAD_DIM
    cur = (None, dilation, nb, ATT_BLOCK, hw)
    one = (None, dilation, 1, ATT_BLOCK, hw)

    def at(part):
        return lambda b, s, hp: (part, b, s, 0, hp)

    def before(part):
        return lambda b, s, hp: (part, b, jnp.maximum(s * nb - 1, 0), 0, hp)

    qkv_specs = [pl.BlockSpec(cur, at(0)), pl.BlockSpec(cur, at(1)), pl.BlockSpec(one, before(1)),
                 pl.BlockSpec(cur, at(2)), pl.BlockSpec(one, before(2))]
    per_head = jax.ShapeDtypeStruct((ATT_HEADS, batch * seq, ATT_HEAD_DIM), F32)
    head_blk = (ATT_HEADS_PER_STEP, span_tokens, ATT_HEAD_DIM)
    head_spec = pl.BlockSpec(head_blk, lambda b, s, hp: (hp, b * spans + s, 0))
    grid = (batch, spans, ATT_HEADS // ATT_HEADS_PER_STEP)
    if others is None:
        return pl.pallas_call(
            functools.partial(_attn_kernel, dilation=dilation, nb=nb),
            out_shape=(per_head, per_head),
            grid=grid,
            in_specs=qkv_specs,
            out_specs=(head_spec, head_spec),
            compiler_params=_params("parallel", "parallel", "parallel"),
            name=f"attn_d{dilation}",
        )(qkv, qkv, qkv, qkv, qkv)
    return pl.pallas_call(
        functools.partial(_attn_combine_kernel, dilation=dilation, nb=nb),
        out_shape=jax.ShapeDtypeStruct((batch * seq, ATT_GROUP_WIDTH), BF16),
        grid=grid,
        in_specs=qkv_specs + [head_spec] * 4,
        out_specs=pl.BlockSpec((span_tokens, hw), lambda b, s, hp: (b * spans + s, hp)),
        scratch_shapes=[pltpu.VMEM(head_blk, F32), pltpu.VMEM(head_blk, F32)],
        compiler_params=_params("parallel", "parallel", "parallel"),
        name=f"attn_d{dilation}_combine",
    )(qkv, qkv, qkv, qkv, qkv, *others)


RET_CHUNKS_PER_STEP = 8


def _ret_kernel(q_ref, k_ref, v_ref, g_ref, gn_ref, intra_ref, cross_ref, sdec_ref, cdec_ref,
                z_ref, state_ref, *, cb):
    @pl.when(pl.program_id(2) == 0)
    def _():
        state_ref[...] = jnp.zeros_like(state_ref)

    c = RET_CHUNK
    intra, cross_decay, state_decay, chunk_decay = intra_ref[...], cross_ref[...], sdec_ref[...], cdec_ref[...]
    gn = gn_ref[...]
    qs, inners, updates = [], [], []
    for t in range(cb):
        rows = slice(t * c, (t + 1) * c)
        q = q_ref[rows, :].astype(BF16)
        k = k_ref[rows, :]
        v = v_ref[rows, :]
        att = lax.dot_general(q, k.astype(BF16), (((1,), (1,)), ((), ())), preferred_element_type=F32) * intra
        inners.append(_mm(att.astype(BF16), v))
        kd = (k * state_decay).astype(BF16)
        updates.append(lax.dot_general(kd, v, (((0,), (0,)), ((), ())), preferred_element_type=F32))
        qs.append(q)
    state = state_ref[...]
    for t in range(cb):
        rows = slice(t * c, (t + 1) * c)
        r = inners[t] + _mm(qs[t], state.astype(BF16)) * cross_decay
        state = chunk_decay * state + updates[t]
        mu = jnp.mean(r, axis=-1, keepdims=True)
        var = jnp.mean(jnp.square(r - mu), axis=-1, keepdims=True)
        rn = (r - mu) * lax.rsqrt(var + GN_EPS) * gn
        g = g_ref[rows, :]
        z_ref[rows, :] = ((g * _sigmoid(g)) * rn).astype(z_ref.dtype)
    state_ref[...] = state


def _retention(qk, v, gates, gn_g, batch, seq):
    n = qk.shape[0]
    c = RET_CHUNK
    nc = seq // c
    cb = min(RET_CHUNKS_PER_STEP, nc)
    assert nc % cb == 0
    steps = nc // cb
    hh = jnp.arange(RET_HEADS, dtype=F32)
    log_gamma = jnp.log(1.0 - 2.0 ** (-5.0 - hh))
    idx = jnp.arange(c, dtype=F32)
    diff = idx[:, None] - idx[None, :]
    intra = jnp.where(diff >= 0, jnp.exp(jnp.maximum(diff, 0.0)[None] * log_gamma[:, None, None]), 0.0)
    cross_decay = jnp.exp((idx + 1.0)[None, :] * log_gamma[:, None])[..., None]
    state_decay = jnp.exp((c - 1.0 - idx)[None, :] * log_gamma[:, None])[..., None]
    chunk_decay = jnp.exp(c * log_gamma)[:, None, None]

    def rows(b, h, s):
        return b * steps + s

    kern = functools.partial(_ret_kernel, cb=cb)
    return pl.pallas_call(
        kern,
        out_shape=jax.ShapeDtypeStruct((n, RET_V_WIDTH), BF16),
        grid=(batch, RET_HEADS, steps),
        in_specs=[
            pl.BlockSpec((cb * c, RET_QK_DIM), lambda b, h, s: (rows(b, h, s), h)),
            pl.BlockSpec((cb * c, RET_QK_DIM), lambda b, h, s: (rows(b, h, s), RET_HEADS + h)),
            pl.BlockSpec((cb * c, RET_V_DIM), lambda b, h, s: (rows(b, h, s), h)),
            pl.BlockSpec((cb * c, RET_V_DIM), lambda b, h, s: (rows(b, h, s), h)),
            pl.BlockSpec((1, RET_V_DIM), lambda b, h, s: (0, h)),
            pl.BlockSpec((None, c, c), lambda b, h, s: (h, 0, 0)),
            pl.BlockSpec((None, c, 1), lambda b, h, s: (h, 0, 0)),
            pl.BlockSpec((None, c, 1), lambda b, h, s: (h, 0, 0)),
            pl.BlockSpec((None, 1, 1), lambda b, h, s: (h, 0, 0)),
        ],
        out_specs=pl.BlockSpec((cb * c, RET_V_DIM), lambda b, h, s: (rows(b, h, s), h)),
        scratch_shapes=[pltpu.VMEM((RET_QK_DIM, RET_V_DIM), F32)],
        compiler_params=_params("parallel", "parallel", "arbitrary"),
        name="retention",
    )(qk, qk, v, gates, gn_g, intra, cross_decay, state_decay, chunk_decay)


def _mix_kernel(oa_ref, z_ref, wa_ref, wr_ref, ga_ref, gr_ref, o_ref):
    ya = _mm(oa_ref[...], wa_ref[...])
    yr = _mm(z_ref[...], wr_ref[...])
    o_ref[...] = (_sigmoid(ga_ref[...]) * ya + _sigmoid(gr_ref[...]) * yr).astype(o_ref.dtype)


def _mix(o_att, z, wa, wr, gates, gate_col0):
    n = o_att.shape[0]
    d = wa.shape[1]
    tm, tn = _tile(n, 512), _tile(d, 1024)
    assert gate_col0 % tn == 0
    ga_off = gate_col0 // tn
    gr_off = ga_off + d // tn
    return pl.pallas_call(
        _mix_kernel,
        out_shape=jax.ShapeDtypeStruct((n, d), BF16),
        grid=(d // tn, n // tm),
        in_specs=[pl.BlockSpec((tm, o_att.shape[1]), lambda j, i: (i, 0)),
                  pl.BlockSpec((tm, z.shape[1]), lambda j, i: (i, 0)),
                  pl.BlockSpec((wa.shape[0], tn), lambda j, i: (0, j)),
                  pl.BlockSpec((wr.shape[0], tn), lambda j, i: (0, j)),
                  pl.BlockSpec((tm, tn), lambda j, i: (i, ga_off + j)),
                  pl.BlockSpec((tm, tn), lambda j, i: (i, gr_off + j))],
        out_specs=pl.BlockSpec((tm, tn), lambda j, i: (i, j)),
        compiler_params=_params("parallel", "parallel"),
        name="mix",
    )(o_att, z, wa, wr, gates, gates)


def _layer_norm_rows(t, g, b):
    mu = jnp.mean(t, axis=-1, keepdims=True)
    var = jnp.mean(jnp.square(t - mu), axis=-1, keepdims=True)
    return (t - mu) * lax.rsqrt(var + LN_EPS) * g + b


LN_ROW_CHUNK = 128


def _layer_norm_inplace(o_ref, g_ref, b_ref, cast_ref=None):
    g, b = g_ref[...], b_ref[...]
    chunk = min(LN_ROW_CHUNK, o_ref.shape[0])

    def body(r, carry):
        rows = pl.ds(pl.multiple_of(r * chunk, chunk), chunk)
        y = _layer_norm_rows(o_ref[rows, :], g, b)
        o_ref[rows, :] = y
        if cast_ref is not None:
            cast_ref[rows, :] = y.astype(cast_ref.dtype)
        return carry

    lax.fori_loop(0, o_ref.shape[0] // chunk, body, 0)


def _proj_ln_kernel(a_ref, w_ref, res_ref, g_ref, b_ref, of_ref, ob_ref, *, alpha):
    y = _layer_norm_rows(alpha * res_ref[...] + _mm(a_ref[...], w_ref[...]), g_ref[...], b_ref[...])
    of_ref[...] = y
    ob_ref[...] = y.astype(ob_ref.dtype)


def _proj_ln(a, w, res, g, b, alpha):
    n, k = a.shape
    d = w.shape[1]
    tm = _tile(n, 128)
    kern = functools.partial(_proj_ln_kernel, alpha=alpha)
    row = pl.BlockSpec((tm, d), lambda i: (i, 0))
    vec = pl.BlockSpec((1, d), lambda i: (0, 0))
    return pl.pallas_call(
        kern,
        out_shape=(jax.ShapeDtypeStruct((n, d), F32), jax.ShapeDtypeStruct((n, d), BF16)),
        grid=(n // tm,),
        in_specs=[pl.BlockSpec((tm, k), lambda i: (i, 0)),
                  pl.BlockSpec((k, d), lambda i: (0, 0), pipeline_mode=pl.Buffered(1)),
                  row, vec, vec],
        out_specs=(row, row),
        compiler_params=_params("parallel"),
        name="proj_ln",
    )(a, w, res, g, b)


def _swiglu_kernel(h_ref, wg_ref, wu_ref, o_ref):
    h = h_ref[...]
    gate = _mm(h, wg_ref[...])
    up = _mm(h, wu_ref[...])
    o_ref[...] = ((gate * _sigmoid(gate)) * up).astype(o_ref.dtype)


def _swiglu(h_bf, wg, wu):
    n, k = h_bf.shape
    f = wg.shape[1]
    tm, tn = _tile(n, 2048), _tile(f, 256)
    w_spec = pl.BlockSpec((k, tn), lambda i, j: (0, j))
    return pl.pallas_call(
        _swiglu_kernel,
        out_shape=jax.ShapeDtypeStruct((n, f), BF16),
        grid=(n // tm, f // tn),
        in_specs=[pl.BlockSpec((tm, k), lambda i, j: (i, 0)), w_spec, w_spec],
        out_specs=pl.BlockSpec((tm, tn), lambda i, j: (i, j)),
        compiler_params=_params("parallel", "parallel"),
        name="swiglu",
    )(h_bf, wg, wu)


def _ple_kernel(hb_ref, wg_ref, p_ref, wu_ref, hf_ref, o_ref, *, alpha):
    gate = _sigmoid(_mm(hb_ref[...], wg_ref[...]))
    up = _mm(p_ref[...], wu_ref[...])
    o_ref[...] = alpha * hf_ref[...] + gate * up


def _ple_base(h_bf, h_f32, p_bf, wpg, wpu, alpha):
    n, k = h_bf.shape
    d = wpg.shape[1]
    tm, tn = _tile(n, 512), _tile(d, 1024)
    kern = functools.partial(_ple_kernel, alpha=alpha)
    return pl.pallas_call(
        kern,
        out_shape=jax.ShapeDtypeStruct((n, d), F32),
        grid=(d // tn, n // tm),
        in_specs=[pl.BlockSpec((tm, k), lambda j, i: (i, 0)),
                  pl.BlockSpec((k, tn), lambda j, i: (0, j)),
                  pl.BlockSpec((tm, p_bf.shape[1]), lambda j, i: (i, 0)),
                  pl.BlockSpec((p_bf.shape[1], tn), lambda j, i: (0, j)),
                  pl.BlockSpec((tm, tn), lambda j, i: (i, j))],
        out_specs=pl.BlockSpec((tm, tn), lambda j, i: (i, j)),
        compiler_params=_params("parallel", "parallel"),
        name="ple_base",
    )(h_bf, wpg, p_bf, wpu, h_f32)


def _down_ln_kernel(a_ref, w_ref, base_ref, g_ref, b_ref, o_ref, *, tn):
    j = pl.program_id(1)
    col = pl.multiple_of(j * tn, tn)
    o_ref[:, pl.ds(col, tn)] = base_ref[...] + _mm(a_ref[...], w_ref[...])

    @pl.when(j == pl.num_programs(1) - 1)
    def _():
        _layer_norm_inplace(o_ref, g_ref, b_ref)


def _down_ln(hidden, wd, base, g, b):
    n, f = hidden.shape
    d = wd.shape[1]
    tm, tn = _tile(n, 512), _tile(d, 256)
    kern = functools.partial(_down_ln_kernel, tn=tn)
    vec = pl.BlockSpec((1, d), lambda i, j: (0, 0))
    return pl.pallas_call(
        kern,
        out_shape=jax.ShapeDtypeStruct((n, d), F32),
        grid=(n // tm, d // tn),
        in_specs=[pl.BlockSpec((tm, f), lambda i, j: (i, 0)),
                  pl.BlockSpec((f, tn), lambda i, j: (0, j)),
                  pl.BlockSpec((tm, tn), lambda i, j: (i, j)),
                  vec, vec],
        out_specs=pl.BlockSpec((tm, d), lambda i, j: (i, 0)),
        compiler_params=_params("parallel", "arbitrary"),
        name="down_ln",
    )(hidden, wd, base, g, b)


def _layer(h_f32, h_bf, p_bf, w_in, w_attn_out, w_ret_out, ret_gn_g, w_o, ln1_g, ln1_b,
           w_ffn_gate, w_ffn_up, w_ffn_down, w_ple_gate, w_ple_up, ln2_g, ln2_b,
           cos, sin, batch, seq, alpha):
    d_model = h_f32.shape[1]
    c_qr = 3 * ATT_QKV_WIDTH
    c_vr = c_qr + 2 * RET_QK_WIDTH
    c_gr = c_vr + RET_V_WIDTH
    assert w_in.shape[1] == c_gr + RET_V_WIDTH + 2 * d_model

    qk_r = _proj_rot(h_bf, w_in, c_qr, cos, sin, seq)
    v_r = _proj(h_bf, w_in, c_vr, RET_V_WIDTH, BF16, "proj_vr")
    gates = _proj(h_bf, w_in, c_gr, RET_V_WIDTH + 2 * d_model, F32, "proj_gates")

    others = []
    for g, dil in enumerate(ATT_DILATIONS[:-1]):
        others.extend(_attention(_proj_units(h_bf, w_in, g, dil, batch, seq), dil, batch, seq))
    g, dil = len(ATT_DILATIONS) - 1, ATT_DILATIONS[-1]
    o_att = _attention(_proj_units(h_bf, w_in, g, dil, batch, seq), dil, batch, seq, others=others)

    z = _retention(qk_r, v_r, gates, ret_gn_g, batch, seq)

    mixed = _mix(o_att, z, w_attn_out, w_ret_out, gates, RET_V_WIDTH)
    h1_f32, h1_bf = _proj_ln(mixed, w_o, h_f32, ln1_g, ln1_b, alpha)

    hidden = _swiglu(h1_bf, w_ffn_gate, w_ffn_up)
    base = _ple_base(h1_bf, h1_f32, p_bf, w_ple_gate, w_ple_up, alpha)
    return _down_ln(hidden, w_ffn_down, base, ln2_g, ln2_b)


def kernel(x, p, w_in, w_attn_out, w_ret_out, ret_gn_g, w_o, ln1_g, ln1_b, w_ffn_gate, w_ffn_up,
           w_ffn_down, w_ple_gate, w_ple_up, ln2_g, ln2_b):
    batch, seq, d_model = x.shape
    depth = w_in.shape[0]
    n = batch * seq
    alpha = (2 * depth) ** 0.25

    half = RET_QK_DIM // 2
    pos = jnp.arange(seq, dtype=F32)
    inv_freq = RET_ROPE_BASE ** (-jnp.arange(half, dtype=F32) / half)
    ang = pos[:, None] * inv_freq[None, :]
    cos, sin = jnp.cos(ang), jnp.sin(ang)

    h = x.reshape(n, d_model)
    for i in range(depth):
        bf = lambda w: w[i].astype(BF16)
        row = lambda v: v[i].reshape(1, -1)
        h = _layer(h, h.astype(BF16), p[i].reshape(n, -1).astype(BF16),
                   bf(w_in), bf(w_attn_out), bf(w_ret_out), row(ret_gn_g), bf(w_o), row(ln1_g), row(ln1_b),
                   w_ffn_gate[i], w_ffn_up[i], bf(w_ffn_down), bf(w_ple_gate), bf(w_ple_up),
                   row(ln2_g), row(ln2_b), cos, sin, batch, seq, alpha)
    return h.reshape(batch, seq, d_model).astype(x.dtype)
```

```python
import functools

import jax
import jax.numpy as jnp
from jax import lax
from jax.experimental import pallas as pl
from jax.experimental.pallas import tpu as pltpu

ATT_HEAD_DIM = 128
ATT_HEADS = 8
ATT_DILATIONS = (1, 4, 16)
ATT_BLOCK = 128
ATT_GROUP_WIDTH = ATT_HEADS * ATT_HEAD_DIM
ATT_QKV_WIDTH = len(ATT_DILATIONS) * ATT_GROUP_WIDTH
RET_HEADS = 8
RET_QK_DIM = 256
RET_V_DIM = 512
RET_QK_WIDTH = RET_HEADS * RET_QK_DIM
RET_V_WIDTH = RET_HEADS * RET_V_DIM
RET_CHUNK = 128
RET_ROPE_BASE = 10000.0
LN_EPS = 1e-5
GN_EPS = 1e-6
NEG_INF = -1e30

V7X_VMEM_LIMIT_BYTES = 56 * 1024 * 1024
LANE = 128

BF16 = jnp.bfloat16
F32 = jnp.float32


def _params(*semantics):
    return pltpu.CompilerParams(dimension_semantics=semantics, vmem_limit_bytes=V7X_VMEM_LIMIT_BYTES)


def _tile(dim, pref):
    if dim <= pref:
        return dim
    t = (pref // LANE) * LANE
    while t > LANE and dim % t:
        t -= LANE
    assert dim % t == 0, (dim, pref)
    return t


def _sigmoid(v):
    return 1.0 / (1.0 + jnp.exp(-v))


def _mm(a, b):
    return lax.dot_general(a, b, (((1,), (0,)), ((), ())), preferred_element_type=F32)


BF16_SUBLANES = 16


def _proj_kernel(x_ref, w_ref, *refs):
    n_cast = (len(refs) - 1) // 2
    o_ref = refs[n_cast]
    o_ref[...] = _mm(x_ref[...], w_ref[...]).astype(o_ref.dtype)
    for src, dst in zip(refs[:n_cast], refs[n_cast + 1:]):
        dst[...] = src[...].astype(dst.dtype)


def _proj(x_bf, w_bf, col0, width, out_dtype, name, cast=()):
    n, k = x_bf.shape
    tm, tn = _tile(n, 1024), _tile(width, 1024)
    assert col0 % tn == 0
    off = col0 // tn
    n_i = n // tm
    steps = (width // tn) * n_i
    cast_specs, cast_shapes = [], []
    for w in cast:
        rows, cols = w.shape
        slab = -(-(-(-rows // steps)) // BF16_SUBLANES) * BF16_SUBLANES
        while rows % slab:
            slab += BF16_SUBLANES
        last = rows // slab - 1
        cast_specs.append(pl.BlockSpec((slab, cols), lambda j, i, last=last: (jnp.minimum(j * n_i + i, last), 0)))
        cast_shapes.append(jax.ShapeDtypeStruct(w.shape, BF16))
    outs = pl.pallas_call(
        _proj_kernel,
        out_shape=[jax.ShapeDtypeStruct((n, width), out_dtype)] + cast_shapes,
        grid=(width // tn, n_i),
        in_specs=[pl.BlockSpec((tm, k), lambda j, i: (i, 0)),
                  pl.BlockSpec((k, tn), lambda j, i: (0, off + j))] + cast_specs,
        out_specs=[pl.BlockSpec((tm, tn), lambda j, i: (i, j))] + cast_specs,
        compiler_params=_params("arbitrary", "arbitrary"),
        name=name,
    )(x_bf, w_bf, *cast)
    return outs if cast else outs[0]


def _proj_rot_kernel(x_ref, w_ref, cos_ref, sin_ref, o_ref, *, n_q_blocks, k_scale):
    j = pl.program_id(0)
    acc = _mm(x_ref[...], w_ref[...])
    scale = jnp.where(j >= n_q_blocks, k_scale, 1.0).astype(F32)
    cos = cos_ref[...]
    sin = sin_ref[...]
    half = RET_QK_DIM // 2
    for h in range(acc.shape[1] // RET_QK_DIM):
        lo = h * RET_QK_DIM
        t1 = acc[:, lo:lo + half]
        t2 = acc[:, lo + half:lo + RET_QK_DIM]
        o_ref[:, lo:lo + half] = (t1 * cos - t2 * sin) * scale
        o_ref[:, lo + half:lo + RET_QK_DIM] = (t1 * sin + t2 * cos) * scale


def _proj_rot(x_bf, w_bf, col0, cos, sin, seq):
    n, k = x_bf.shape
    width = 2 * RET_QK_WIDTH
    tm, tn = _tile(seq, 1024), 1024
    assert col0 % tn == 0 and RET_QK_WIDTH % tn == 0 and n % seq == 0
    off = col0 // tn
    pos_blocks = seq // tm
    kern = functools.partial(_proj_rot_kernel, n_q_blocks=RET_QK_WIDTH // tn, k_scale=RET_QK_DIM ** -0.5)
    half = RET_QK_DIM // 2
    return pl.pallas_call(
        kern,
        out_shape=jax.ShapeDtypeStruct((n, width), F32),
        grid=(width // tn, n // tm),
        in_specs=[pl.BlockSpec((tm, k), lambda j, i: (i, 0)),
                  pl.BlockSpec((k, tn), lambda j, i: (0, off + j)),
                  pl.BlockSpec((tm, half), lambda j, i: (i % pos_blocks, 0)),
                  pl.BlockSpec((tm, half), lambda j, i: (i % pos_blocks, 0))],
        out_specs=pl.BlockSpec((tm, tn), lambda j, i: (i, j)),
        compiler_params=_params("parallel", "parallel"),
        name="proj_rot",
    )(x_bf, w_bf, cos, sin)


ATT_SPAN_BLOCKS = 16
ATT_HEADS_PER_STEP = 2


def _proj_units_kernel(x_ref, w_ref, o_ref, acc_ref, *, dilation):
    acc = _mm(x_ref[...], w_ref[...])
    if dilation == 1:
        o_ref[0, 0] = acc.astype(o_ref.dtype)
        return
    rows = acc.shape[0] // dilation
    slabs = acc.shape[1] // LANE
    for s in range(slabs):
        acc_ref[s] = acc[:, s * LANE:(s + 1) * LANE]
    for r in range(dilation):
        for s in range(slabs):
            piece = acc_ref[s, pl.ds(r, rows, stride=dilation), :]
            o_ref[r, 0, :, s * LANE:(s + 1) * LANE] = piece.astype(o_ref.dtype)


def _proj_units(x_bf, w_bf, group, dilation, batch, seq):
    n, k = x_bf.shape
    tm, tn = _tile(seq, 1024), ATT_GROUP_WIDTH
    assert tm % (dilation * 16) == 0 and seq % tm == 0
    rows = tm // dilation
    tiles = seq // tm
    n_groups = len(ATT_DILATIONS)
    kern = functools.partial(_proj_units_kernel, dilation=dilation)
    out = pl.pallas_call(
        kern,
        out_shape=jax.ShapeDtypeStruct((3, batch * dilation, tiles, rows, tn), BF16),
        grid=(3, n // tm),
        in_specs=[pl.BlockSpec((tm, k), lambda j, i: (i, 0)),
                  pl.BlockSpec((k, tn), lambda j, i: (0, j * n_groups + group))],
        out_specs=pl.BlockSpec((None, dilation, 1, rows, tn), lambda j, i: (j, i // tiles, i % tiles, 0, 0)),
        scratch_shapes=[pltpu.VMEM((tn // LANE, tm, LANE), F32)],
        compiler_params=_params("parallel", "parallel"),
        name=f"proj_att_d{dilation}",
    )(x_bf, w_bf)
    n_blk = seq // (dilation * ATT_BLOCK)
    return out.reshape(3, batch * dilation, n_blk, ATT_BLOCK, tn)


def _attn_span(q_ref, k_ref, kp_ref, v_ref, vp_ref, o_ref, lse_ref, *, dilation, nb):
    qi = lax.broadcasted_iota(jnp.int32, (ATT_BLOCK, 2 * ATT_BLOCK), 0)
    kj = lax.broadcasted_iota(jnp.int32, (ATT_BLOCK, 2 * ATT_BLOCK), 1)
    in_window = kj <= qi + ATT_BLOCK
    valid_inner = jnp.logical_and(kj >= qi, in_window)
    lo = jnp.where(pl.program_id(1) > 0, 0, ATT_BLOCK)
    valid_first = jnp.logical_and(kj >= jnp.maximum(qi, lo), in_window)
    scale = ATT_HEAD_DIM ** -0.5
    dn = (((1,), (1,)), ((), ()))
    ones = jnp.ones((2 * ATT_BLOCK, ATT_HEAD_DIM), BF16)
    units = [(r, i, h) for r in range(dilation) for i in range(nb) for h in range(ATT_HEADS_PER_STEP)]

    def cols(h):
        return slice(h * ATT_HEAD_DIM, (h + 1) * ATT_HEAD_DIM)

    def with_prev(cur_ref, prev_ref, r, i, h):
        prev = prev_ref[r, 0, :, cols(h)] if i == 0 else cur_ref[r, i - 1, :, cols(h)]
        return jnp.concatenate([prev, cur_ref[r, i, :, cols(h)]], axis=0)

    scores = [lax.dot_general(q_ref[r, i, :, cols(h)], with_prev(k_ref, kp_ref, r, i, h), dn,
                              preferred_element_type=F32) * scale for r, i, h in units]
    probs, maxes = [], []
    for (r, i, h), s in zip(units, scores):
        s = jnp.where(valid_first if i == 0 else valid_inner, s, NEG_INF)
        m = jnp.max(s, axis=-1, keepdims=True)
        probs.append(jnp.exp(s - m).astype(BF16))
        maxes.append(m)
    for (r, i, h), p, m in zip(units, probs, maxes):
        v_aug = jnp.concatenate([with_prev(v_ref, vp_ref, r, i, h), ones], axis=1)
        acc = _mm(p, v_aug)
        den = acc[:, ATT_HEAD_DIM:]
        if dilation == 1:
            rows = pl.ds(i * ATT_BLOCK, ATT_BLOCK)
        else:
            rows = pl.ds(i * ATT_BLOCK * dilation + r, ATT_BLOCK, stride=dilation)
        o_ref[h, rows, :] = acc[:, :ATT_HEAD_DIM] / den
        lse_ref[h, rows, :] = m + jnp.log(den)


def _attn_kernel(q_ref, k_ref, kp_ref, v_ref, vp_ref, o_ref, lse_ref, *, dilation, nb):
    _attn_span(q_ref, k_ref, kp_ref, v_ref, vp_ref, o_ref, lse_ref, dilation=dilation, nb=nb)


ATT_COMBINE_ROWS = 256


def _attn_combine_kernel(q_ref, k_ref, kp_ref, v_ref, vp_ref, o1, l1, o2, l2, out_ref, o3, l3, *, dilation, nb):
    _attn_span(q_ref, k_ref, kp_ref, v_ref, vp_ref, o3, l3, dilation=dilation, nb=nb)

    def body(c, carry):
        rows = pl.ds(pl.multiple_of(c * ATT_COMBINE_ROWS, ATT_COMBINE_ROWS), ATT_COMBINE_ROWS)
        for h in range(ATT_HEADS_PER_STEP):
            a, b, d = l1[h, rows, :], l2[h, rows, :], l3[h, rows, :]
            m = jnp.maximum(jnp.maximum(a, b), d)
            ea, eb, ed = jnp.exp(a - m), jnp.exp(b - m), jnp.exp(d - m)
            s = ea + eb + ed
            out = (ea / s) * o1[h, rows, :] + (eb / s) * o2[h, rows, :] + (ed / s) * o3[h, rows, :]
            out_ref[rows, h * ATT_HEAD_DIM:(h + 1) * ATT_HEAD_DIM] = out.astype(out_ref.dtype)
        return carry

    lax.fori_loop(0, out_ref.shape[0] // ATT_COMBINE_ROWS, body, 0)


def _attention(qkv, dilation, batch, seq, others=None):
    n_blk = qkv.shape[2]
    assert ATT_SPAN_BLOCKS % dilation == 0
    nb = ATT_SPAN_BLOCKS // dilation
    assert n_blk % nb == 0
    spans = n_blk // nb
    span_tokens = ATT_SPAN_BLOCKS * ATT_BLOCK
    hw = ATT_HEADS_PER_STEP * ATT_HEAD_DIM
    cur = (None, dilation, nb, ATT_BLOCK, hw)
    one = (None, dilation, 1, ATT_BLOCK, hw)

    def at(part):
        return lambda b, s, hp: (part, b, s, 0, hp)

    def before(part):
        return lambda b, s, hp: (part, b, jnp.maximum(s * nb - 1, 0), 0, hp)

    qkv_specs = [pl.BlockSpec(cur, at(0)), pl.BlockSpec(cur, at(1)), pl.BlockSpec(one, before(1)),
                 pl.BlockSpec(cur, at(2)), pl.BlockSpec(one, before(2))]
    per_head = jax.ShapeDtypeStruct((ATT_HEADS, batch * seq, ATT_HEAD_DIM), F32)
    head_blk = (ATT_HEADS_PER_STEP, span_tokens, ATT_HEAD_DIM)
    head_spec = pl.BlockSpec(head_blk, lambda b, s, hp: (hp, b * spans + s, 0))
    grid = (batch, spans, ATT_HEADS // ATT_HEADS_PER_STEP)
    if others is None:
        return pl.pallas_call(
            functools.partial(_attn_kernel, dilation=dilation, nb=nb),
            out_shape=(per_head, per_head),
            grid=grid,
            in_specs=qkv_specs,
            out_specs=(head_spec, head_spec),
            compiler_params=_params("parallel", "parallel", "parallel"),
            name=f"attn_d{dilation}",
        )(qkv, qkv, qkv, qkv, qkv)
    return pl.pallas_call(
        functools.partial(_attn_combine_kernel, dilation=dilation, nb=nb),
        out_shape=jax.ShapeDtypeStruct((batch * seq, ATT_GROUP_WIDTH), BF16),
        grid=grid,
        in_specs=qkv_specs + [head_spec] * 4,
        out_specs=pl.BlockSpec((span_tokens, hw), lambda b, s, hp: (b * spans + s, hp)),
        scratch_shapes=[pltpu.VMEM(head_blk, F32), pltpu.VMEM(head_blk, F32)],
        compiler_params=_params("parallel", "parallel", "parallel"),
        name=f"attn_d{dilation}_combine",
    )(qkv, qkv, qkv, qkv, qkv, *others)


RET_CHUNKS_PER_STEP = 8


def _ret_kernel(q_ref, k_ref, v_ref, g_ref, gn_ref, intra_ref, cross_ref, sdec_ref, cdec_ref,
                z_ref, state_ref, *, cb):
    @pl.when(pl.program_id(2) == 0)
    def _():
        state_ref[...] = jnp.zeros_like(state_ref)

    c = RET_CHUNK
    intra, cross_decay, state_decay, chunk_decay = intra_ref[...], cross_ref[...], sdec_ref[...], cdec_ref[...]
    gn = gn_ref[...]
    qs, inners, updates = [], [], []
    for t in range(cb):
        rows = slice(t * c, (t + 1) * c)
        q = q_ref[rows, :].astype(BF16)
        k = k_ref[rows, :]
        v = v_ref[rows, :]
        att = lax.dot_general(q, k.astype(BF16), (((1,), (1,)), ((), ())), preferred_element_type=F32) * intra
        inners.append(_mm(att.astype(BF16), v))
        kd = (k * state_decay).astype(BF16)
        updates.append(lax.dot_general(kd, v, (((0,), (0,)), ((), ())), preferred_element_type=F32))
        qs.append(q)
    state = state_ref[...]
    for t in range(cb):
        rows = slice(t * c, (t + 1) * c)
        r = inners[t] + _mm(qs[t], state.astype(BF16)) * cross_decay
        state = chunk_decay * state + updates[t]
        mu = jnp.mean(r, axis=-1, keepdims=True)
        var = jnp.mean(jnp.square(r - mu), axis=-1, keepdims=True)
        rn = (r - mu) * lax.rsqrt(var + GN_EPS) * gn
        g = g_ref[rows, :]
        z_ref[rows, :] = ((g * _sigmoid(g)) * rn).astype(z_ref.dtype)
    state_ref[...] = state


def _retention(qk, v, gates, gn_g, batch, seq):
    n = qk.shape[0]
    c = RET_CHUNK
    nc = seq // c
    cb = min(RET_CHUNKS_PER_STEP, nc)
    assert nc % cb == 0
    steps = nc // cb
    hh = jnp.arange(RET_HEADS, dtype=F32)
    log_gamma = jnp.log(1.0 - 2.0 ** (-5.0 - hh))
    idx = jnp.arange(c, dtype=F32)
    diff = idx[:, None] - idx[None, :]
    intra = jnp.where(diff >= 0, jnp.exp(jnp.maximum(diff, 0.0)[None] * log_gamma[:, None, None]), 0.0)
    cross_decay = jnp.exp((idx + 1.0)[None, :] * log_gamma[:, None])[..., None]
    state_decay = jnp.exp((c - 1.0 - idx)[None, :] * log_gamma[:, None])[..., None]
    chunk_decay = jnp.exp(c * log_gamma)[:, None, None]

    def rows(b, h, s):
        return b * steps + s

    kern = functools.partial(_ret_kernel, cb=cb)
    return pl.pallas_call(
        kern,
        out_shape=jax.ShapeDtypeStruct((n, RET_V_WIDTH), BF16),
        grid=(batch, RET_HEADS, steps),
        in_specs=[
            pl.BlockSpec((cb * c, RET_QK_DIM), lambda b, h, s: (rows(b, h, s), h)),
            pl.BlockSpec((cb * c, RET_QK_DIM), lambda b, h, s: (rows(b, h, s), RET_HEADS + h)),
            pl.BlockSpec((cb * c, RET_V_DIM), lambda b, h, s: (rows(b, h, s), h)),
            pl.BlockSpec((cb * c, RET_V_DIM), lambda b, h, s: (rows(b, h, s), h)),
            pl.BlockSpec((1, RET_V_DIM), lambda b, h, s: (0, h)),
            pl.BlockSpec((None, c, c), lambda b, h, s: (h, 0, 0)),
            pl.BlockSpec((None, c, 1), lambda b, h, s: (h, 0, 0)),
            pl.BlockSpec((None, c, 1), lambda b, h, s: (h, 0, 0)),
            pl.BlockSpec((None, 1, 1), lambda b, h, s: (h, 0, 0)),
        ],
        out_specs=pl.BlockSpec((cb * c, RET_V_DIM), lambda b, h, s: (rows(b, h, s), h)),
        scratch_shapes=[pltpu.VMEM((RET_QK_DIM, RET_V_DIM), F32)],
        compiler_params=_params("parallel", "parallel", "arbitrary"),
        name="retention",
    )(qk, qk, v, gates, gn_g, intra, cross_decay, state_decay, chunk_decay)


def _mix_kernel(oa_ref, z_ref, wa_ref, wr_ref, ga_ref, gr_ref, o_ref):
    ya = _mm(oa_ref[...], wa_ref[...])
    yr = _mm(z_ref[...], wr_ref[...])
    o_ref[...] = (_sigmoid(ga_ref[...]) * ya + _sigmoid(gr_ref[...]) * yr).astype(o_ref.dtype)


def _mix(o_att, z, wa, wr, gates, gate_col0):
    n = o_att.shape[0]
    d = wa.shape[1]
    tm, tn = _tile(n, 512), _tile(d, 1024)
    assert gate_col0 % tn == 0
    ga_off = gate_col0 // tn
    gr_off = ga_off + d // tn
    return pl.pallas_call(
        _mix_kernel,
        out_shape=jax.ShapeDtypeStruct((n, d), BF16),
        grid=(d // tn, n // tm),
        in_specs=[pl.BlockSpec((tm, o_att.shape[1]), lambda j, i: (i, 0)),
                  pl.BlockSpec((tm, z.shape[1]), lambda j, i: (i, 0)),
                  pl.BlockSpec((wa.shape[0], tn), lambda j, i: (0, j)),
                  pl.BlockSpec((wr.shape[0], tn), lambda j, i: (0, j)),
                  pl.BlockSpec((tm, tn), lambda j, i: (i, ga_off + j)),
                  pl.BlockSpec((tm, tn), lambda j, i: (i, gr_off + j))],
        out_specs=pl.BlockSpec((tm, tn), lambda j, i: (i, j)),
        compiler_params=_params("parallel", "parallel"),
        name="mix",
    )(o_att, z, wa, wr, gates, gates)


def _layer_norm_rows(t, g, b):
    mu = jnp.mean(t, axis=-1, keepdims=True)
    var = jnp.mean(jnp.square(t - mu), axis=-1, keepdims=True)
    return (t - mu) * lax.rsqrt(var + LN_EPS) * g + b


LN_ROW_CHUNK = 128


def _layer_norm_inplace(o_ref, g_ref, b_ref, cast_ref=None):
    g, b = g_ref[...], b_ref[...]
    chunk = min(LN_ROW_CHUNK, o_ref.shape[0])

    def body(r, carry):
        rows = pl.ds(pl.multiple_of(r * chunk, chunk), chunk)
        y = _layer_norm_rows(o_ref[rows, :], g, b)
        o_ref[rows, :] = y
        if cast_ref is not None:
            cast_ref[rows, :] = y.astype(cast_ref.dtype)
        return carry

    lax.fori_loop(0, o_ref.shape[0] // chunk, body, 0)


def _proj_ln_kernel(a_ref, w_ref, res_ref, g_ref, b_ref, of_ref, ob_ref, *, alpha):
    y = _layer_norm_rows(alpha * res_ref[...] + _mm(a_ref[...], w_ref[...]), g_ref[...], b_ref[...])
    of_ref[...] = y
    ob_ref[...] = y.astype(ob_ref.dtype)


def _proj_ln(a, w, res, g, b, alpha):
    n, k = a.shape
    d = w.shape[1]
    tm = _tile(n, 128)
    kern = functools.partial(_proj_ln_kernel, alpha=alpha)
    row = pl.BlockSpec((tm, d), lambda i: (i, 0))
    vec = pl.BlockSpec((1, d), lambda i: (0, 0))
    return pl.pallas_call(
        kern,
        out_shape=(jax.ShapeDtypeStruct((n, d), F32), jax.ShapeDtypeStruct((n, d), BF16)),
        grid=(n // tm,),
        in_specs=[pl.BlockSpec((tm, k), lambda i: (i, 0)),
                  pl.BlockSpec((k, d), lambda i: (0, 0), pipeline_mode=pl.Buffered(1)),
                  row, vec, vec],
        out_specs=(row, row),
        compiler_params=_params("parallel"),
        name="proj_ln",
    )(a, w, res, g, b)


def _swiglu_kernel(h_ref, wg_ref, wu_ref, o_ref):
    h = h_ref[...]
    gate = _mm(h, wg_ref[...])
    up = _mm(h, wu_ref[...])
    o_ref[...] = ((gate * _sigmoid(gate)) * up).astype(o_ref.dtype)


def _swiglu(h_bf, wg, wu):
    n, k = h_bf.shape
    f = wg.shape[1]
    tm, tn = _tile(n, 2048), _tile(f, 256)
    w_spec = pl.BlockSpec((k, tn), lambda i, j: (0, j))
    return pl.pallas_call(
        _swiglu_kernel,
        out_shape=jax.ShapeDtypeStruct((n, f), BF16),
        grid=(n // tm, f // tn),
        in_specs=[pl.BlockSpec((tm, k), lambda i, j: (i, 0)), w_spec, w_spec],
        out_specs=pl.BlockSpec((tm, tn), lambda i, j: (i, j)),
        compiler_params=_params("parallel", "parallel"),
        name="swiglu",
    )(h_bf, wg, wu)


def _ple_kernel(hb_ref, wg_ref, p_ref, wu_ref, hf_ref, o_ref, *, alpha):
    gate = _sigmoid(_mm(hb_ref[...], wg_ref[...]))
    up = _mm(p_ref[...], wu_ref[...])
    o_ref[...] = alpha * hf_ref[...] + gate * up


def _ple_base(h_bf, h_f32, p_bf, wpg, wpu, alpha):
    n, k = h_bf.shape
    d = wpg.shape[1]
    tm, tn = _tile(n, 512), _tile(d, 1024)
    kern = functools.partial(_ple_kernel, alpha=alpha)
    return pl.pallas_call(
        kern,
        out_shape=jax.ShapeDtypeStruct((n, d), F32),
        grid=(d // tn, n // tm),
        in_specs=[pl.BlockSpec((tm, k), lambda j, i: (i, 0)),
                  pl.BlockSpec((k, tn), lambda j, i: (0, j)),
                  pl.BlockSpec((tm, p_bf.shape[1]), lambda j, i: (i, 0)),
                  pl.BlockSpec((p_bf.shape[1], tn), lambda j, i: (0, j)),
                  pl.BlockSpec((tm, tn), lambda j, i: (i, j))],
        out_specs=pl.BlockSpec((tm, tn), lambda j, i: (i, j)),
        compiler_params=_params("parallel", "parallel"),
        name="ple_base",
    )(h_bf, wpg, p_bf, wpu, h_f32)


def _down_ln_kernel(a_ref, w_ref, base_ref, g_ref, b_ref, o_ref, *, tn):
    j = pl.program_id(1)
    col = pl.multiple_of(j * tn, tn)
    o_ref[:, pl.ds(col, tn)] = base_ref[...] + _mm(a_ref[...], w_ref[...])

    @pl.when(j == pl.num_programs(1) - 1)
    def _():
        _layer_norm_inplace(o_ref, g_ref, b_ref)


def _down_ln(hidden, wd, base, g, b):
    n, f = hidden.shape
    d = wd.shape[1]
    tm, tn = _tile(n, 512), _tile(d, 256)
    kern = functools.partial(_down_ln_kernel, tn=tn)
    vec = pl.BlockSpec((1, d), lambda i, j: (0, 0))
    return pl.pallas_call(
        kern,
        out_shape=jax.ShapeDtypeStruct((n, d), F32),
        grid=(n // tm, d // tn),
        in_specs=[pl.BlockSpec((tm, f), lambda i, j: (i, 0)),
                  pl.BlockSpec((f, tn), lambda i, j: (0, j)),
                  pl.BlockSpec((tm, tn), lambda i, j: (i, j)),
                  vec, vec],
        out_specs=pl.BlockSpec((tm, d), lambda i, j: (i, 0)),
        compiler_params=_params("parallel", "arbitrary"),
        name="down_ln",
    )(hidden, wd, base, g, b)


def _layer(h_f32, h_bf, p_bf, w_in, w_attn_out, w_ret_out, ret_gn_g, w_o, ln1_g, ln1_b,
           w_ffn_gate, w_ffn_up, w_ffn_down, w_ple_gate, w_ple_up, ln2_g, ln2_b,
           cos, sin, batch, seq, alpha):
    d_model = h_f32.shape[1]
    c_qr = 3 * ATT_QKV_WIDTH
    c_vr = c_qr + 2 * RET_QK_WIDTH
    c_gr = c_vr + RET_V_WIDTH
    assert w_in.shape[1] == c_gr + RET_V_WIDTH + 2 * d_model

    qk_r = _proj_rot(h_bf, w_in, c_qr, cos, sin, seq)
    v_r, w_attn_out, w_ret_out, w_o, w_ple_gate = _proj(
        h_bf, w_in, c_vr, RET_V_WIDTH, BF16, "proj_vr", cast=(w_attn_out, w_ret_out, w_o, w_ple_gate))
    gates, w_ffn_down = _proj(
        h_bf, w_in, c_gr, RET_V_WIDTH + 2 * d_model, F32, "proj_gates", cast=(w_ffn_down,))

    others = []
    for g, dil in enumerate(ATT_DILATIONS[:-1]):
        others.extend(_attention(_proj_units(h_bf, w_in, g, dil, batch, seq), dil, batch, seq))
    g, dil = len(ATT_DILATIONS) - 1, ATT_DILATIONS[-1]
    o_att = _attention(_proj_units(h_bf, w_in, g, dil, batch, seq), dil, batch, seq, others=others)

    z = _retention(qk_r, v_r, gates, ret_gn_g, batch, seq)

    mixed = _mix(o_att, z, w_attn_out, w_ret_out, gates, RET_V_WIDTH)
    h1_f32, h1_bf = _proj_ln(mixed, w_o, h_f32, ln1_g, ln1_b, alpha)

    hidden = _swiglu(h1_bf, w_ffn_gate, w_ffn_up)
    base = _ple_base(h1_bf, h1_f32, p_bf, w_ple_gate, w_ple_up, alpha)
    return _down_ln(hidden, w_ffn_down, base, ln2_g, ln2_b)


def kernel(x, p, w_in, w_attn_out, w_ret_out, ret_gn_g, w_o, ln1_g, ln1_b, w_ffn_gate, w_ffn_up,
           w_ffn_down, w_ple_gate, w_ple_up, ln2_g, ln2_b):
    batch, seq, d_model = x.shape
    depth = w_in.shape[0]
    n = batch * seq
    alpha = (2 * depth) ** 0.25

    half = RET_QK_DIM // 2
    pos = jnp.arange(seq, dtype=F32)
    inv_freq = RET_ROPE_BASE ** (-jnp.arange(half, dtype=F32) / half)
    ang = pos[:, None] * inv_freq[None, :]
    cos, sin = jnp.cos(ang), jnp.sin(ang)

    h = x.reshape(n, d_model)
    for i in range(depth):
        bf = lambda w: w[i].astype(BF16)
        row = lambda v: v[i].reshape(1, -1)
        h = _layer(h, h.astype(BF16), p[i].reshape(n, -1).astype(BF16),
                   bf(w_in), w_attn_out[i], w_ret_out[i], row(ret_gn_g), w_o[i], row(ln1_g), row(ln1_b),
                   w_ffn_gate[i], w_ffn_up[i], w_ffn_down[i], w_ple_gate[i], bf(w_ple_up),
                   row(ln2_g), row(ln2_b), cos, sin, batch, seq, alpha)
    return h.reshape(batch, seq, d_model).astype(x.dtype)
```

```python
import functools

import jax
import jax.numpy as jnp
from jax import lax
from jax.experimental import pallas as pl
from jax.experimental.pallas import tpu as pltpu

ATT_HEAD_DIM = 128
ATT_HEADS = 8
ATT_DILATIONS = (1, 4, 16)
ATT_BLOCK = 128
ATT_GROUP_WIDTH = ATT_HEADS * ATT_HEAD_DIM
ATT_QKV_WIDTH = len(ATT_DILATIONS) * ATT_GROUP_WIDTH
RET_HEADS = 8
RET_QK_DIM = 256
RET_V_DIM = 512
RET_QK_WIDTH = RET_HEADS * RET_QK_DIM
RET_V_WIDTH = RET_HEADS * RET_V_DIM
RET_CHUNK = 128
RET_ROPE_BASE = 10000.0
LN_EPS = 1e-5
GN_EPS = 1e-6
NEG_INF = -1e30

V7X_VMEM_LIMIT_BYTES = 56 * 1024 * 1024
LANE = 128

BF16 = jnp.bfloat16
F32 = jnp.float32


def _params(*semantics):
    return pltpu.CompilerParams(dimension_semantics=semantics, vmem_limit_bytes=V7X_VMEM_LIMIT_BYTES)


def _tile(dim, pref):
    if dim <= pref:
        return dim
    t = (pref // LANE) * LANE
    while t > LANE and dim % t:
        t -= LANE
    assert dim % t == 0, (dim, pref)
    return t


def _sigmoid(v):
    return 1.0 / (1.0 + jnp.exp(-v))


def _mm(a, b):
    return lax.dot_general(a, b, (((1,), (0,)), ((), ())), preferred_element_type=F32)


BF16_SUBLANES = 16


def _proj_kernel(x_ref, w_ref, *refs):
    n_cast = (len(refs) - 1) // 2
    o_ref = refs[n_cast]
    o_ref[...] = _mm(x_ref[...], w_ref[...]).astype(o_ref.dtype)
    for src, dst in zip(refs[:n_cast], refs[n_cast + 1:]):
        dst[...] = src[...].astype(dst.dtype)


def _proj(x_bf, w_bf, col0, width, out_dtype, name, cast=()):
    n, k = x_bf.shape
    tm, tn = _tile(n, 1024), _tile(width, 1024)
    assert col0 % tn == 0
    off = col0 // tn
    n_i = n // tm
    steps = (width // tn) * n_i
    cast_specs, cast_shapes = [], []
    for w in cast:
        rows, cols = w.shape
        slab = -(-(-(-rows // steps)) // BF16_SUBLANES) * BF16_SUBLANES
        while rows % slab:
            slab += BF16_SUBLANES
        last = rows // slab - 1
        cast_specs.append(pl.BlockSpec((slab, cols), lambda j, i, last=last: (jnp.minimum(j * n_i + i, last), 0)))
        cast_shapes.append(jax.ShapeDtypeStruct(w.shape, BF16))
    outs = pl.pallas_call(
        _proj_kernel,
        out_shape=[jax.ShapeDtypeStruct((n, width), out_dtype)] + cast_shapes,
        grid=(width // tn, n_i),
        in_specs=[pl.BlockSpec((tm, k), lambda j, i: (i, 0)),
                  pl.BlockSpec((k, tn), lambda j, i: (0, off + j))] + cast_specs,
        out_specs=[pl.BlockSpec((tm, tn), lambda j, i: (i, j))] + cast_specs,
        compiler_params=_params("arbitrary", "arbitrary"),
        name=name,
    )(x_bf, w_bf, *cast)
    return outs if cast else outs[0]


def _proj_rot_kernel(x_ref, w_ref, cos_ref, sin_ref, o_ref, *, n_q_blocks, k_scale):
    j = pl.program_id(0)
    acc = _mm(x_ref[...], w_ref[...])
    scale = jnp.where(j >= n_q_blocks, k_scale, 1.0).astype(F32)
    cos = cos_ref[...]
    sin = sin_ref[...]
    half = RET_QK_DIM // 2
    for h in range(acc.shape[1] // RET_QK_DIM):
        lo = h * RET_QK_DIM
        t1 = acc[:, lo:lo + half]
        t2 = acc[:, lo + half:lo + RET_QK_DIM]
        o_ref[:, lo:lo + half] = (t1 * cos - t2 * sin) * scale
        o_ref[:, lo + half:lo + RET_QK_DIM] = (t1 * sin + t2 * cos) * scale


def _proj_rot(x_bf, w_bf, col0, cos, sin, seq):
    n, k = x_bf.shape
    width = 2 * RET_QK_WIDTH
    tm, tn = _tile(seq, 1024), 1024
    assert col0 % tn == 0 and RET_QK_WIDTH % tn == 0 and n % seq == 0
    off = col0 // tn
    pos_blocks = seq // tm
    kern = functools.partial(_proj_rot_kernel, n_q_blocks=RET_QK_WIDTH // tn, k_scale=RET_QK_DIM ** -0.5)
    half = RET_QK_DIM // 2
    return pl.pallas_call(
        kern,
        out_shape=jax.ShapeDtypeStruct((n, width), F32),
        grid=(width // tn, n // tm),
        in_specs=[pl.BlockSpec((tm, k), lambda j, i: (i, 0)),
                  pl.BlockSpec((k, tn), lambda j, i: (0, off + j)),
                  pl.BlockSpec((tm, half), lambda j, i: (i % pos_blocks, 0)),
                  pl.BlockSpec((tm, half), lambda j, i: (i % pos_blocks, 0))],
        out_specs=pl.BlockSpec((tm, tn), lambda j, i: (i, j)),
        compiler_params=_params("parallel", "parallel"),
        name="proj_rot",
    )(x_bf, w_bf, cos, sin)


ATT_SPAN_BLOCKS = 16
ATT_HEADS_PER_STEP = 2


def _proj_units_kernel(x_ref, w_ref, o_ref, acc_ref, *, dilation):
    acc = _mm(x_ref[...], w_ref[...])
    if dilation == 1:
        o_ref[0, 0] = acc.astype(o_ref.dtype)
        return
    rows = acc.shape[0] // dilation
    slabs = acc.shape[1] // LANE
    for s in range(slabs):
        acc_ref[s] = acc[:, s * LANE:(s + 1) * LANE]
    for r in range(dilation):
        for s in range(slabs):
            piece = acc_ref[s, pl.ds(r, rows, stride=dilation), :]
            o_ref[r, 0, :, s * LANE:(s + 1) * LANE] = piece.astype(o_ref.dtype)


def _proj_units(x_bf, w_bf, group, dilation, batch, seq):
    n, k = x_bf.shape
    tm, tn = _tile(seq, 1024), ATT_GROUP_WIDTH
    assert tm % (dilation * 16) == 0 and seq % tm == 0
    rows = tm // dilation
    tiles = seq // tm
    n_groups = len(ATT_DILATIONS)
    kern = functools.partial(_proj_units_kernel, dilation=dilation)
    out = pl.pallas_call(
        kern,
        out_shape=jax.ShapeDtypeStruct((3, batch * dilation, tiles, rows, tn), BF16),
        grid=(3, n // tm),
        in_specs=[pl.BlockSpec((tm, k), lambda j, i: (i, 0)),
                  pl.BlockSpec((k, tn), lambda j, i: (0, j * n_groups + group))],
        out_specs=pl.BlockSpec((None, dilation, 1, rows, tn), lambda j, i: (j, i // tiles, i % tiles, 0, 0)),
        scratch_shapes=[pltpu.VMEM((tn // LANE, tm, LANE), F32)],
        compiler_params=_params("parallel", "parallel"),
        name=f"proj_att_d{dilation}",
    )(x_bf, w_bf)
    n_blk = seq // (dilation * ATT_BLOCK)
    return out.reshape(3, batch * dilation, n_blk, ATT_BLOCK, tn)


def _attn_span(q_ref, k_ref, kp_ref, v_ref, vp_ref, o_ref, lse_ref, *, dilation, nb):
    qi = lax.broadcasted_iota(jnp.int32, (ATT_BLOCK, 2 * ATT_BLOCK), 0)
    kj = lax.broadcasted_iota(jnp.int32, (ATT_BLOCK, 2 * ATT_BLOCK), 1)
    in_window = kj <= qi + ATT_BLOCK
    valid_inner = jnp.logical_and(kj >= qi, in_window)
    lo = jnp.where(pl.program_id(1) > 0, 0, ATT_BLOCK)
    valid_first = jnp.logical_and(kj >= jnp.maximum(qi, lo), in_window)
    scale = ATT_HEAD_DIM ** -0.5
    dn = (((1,), (1,)), ((), ()))
    ones = jnp.ones((2 * ATT_BLOCK, ATT_HEAD_DIM), BF16)
    units = [(r, i, h) for r in range(dilation) for i in range(nb) for h in range(ATT_HEADS_PER_STEP)]

    def cols(h):
        return slice(h * ATT_HEAD_DIM, (h + 1) * ATT_HEAD_DIM)

    def with_prev(cur_ref, prev_ref, r, i, h):
        prev = prev_ref[r, 0, :, cols(h)] if i == 0 else cur_ref[r, i - 1, :, cols(h)]
        return jnp.concatenate([prev, cur_ref[r, i, :, cols(h)]], axis=0)

    scores = [lax.dot_general(q_ref[r, i, :, cols(h)], with_prev(k_ref, kp_ref, r, i, h), dn,
                              preferred_element_type=F32) * scale for r, i, h in units]
    probs, maxes = [], []
    for (r, i, h), s in zip(units, scores):
        s = jnp.where(valid_first if i == 0 else valid_inner, s, NEG_INF)
        m = jnp.max(s, axis=-1, keepdims=True)
        probs.append(jnp.exp(s - m).astype(BF16))
        maxes.append(m)
    for (r, i, h), p, m in zip(units, probs, maxes):
        v_aug = jnp.concatenate([with_prev(v_ref, vp_ref, r, i, h), ones], axis=1)
        acc = _mm(p, v_aug)
        den = acc[:, ATT_HEAD_DIM:]
        if dilation == 1:
            rows = pl.ds(i * ATT_BLOCK, ATT_BLOCK)
        else:
            rows = pl.ds(i * ATT_BLOCK * dilation + r, ATT_BLOCK, stride=dilation)
        o_ref[h, rows, :] = acc[:, :ATT_HEAD_DIM] / den
        lse_ref[h, rows, :] = m + jnp.log(den)


def _attn_kernel(q_ref, k_ref, kp_ref, v_ref, vp_ref, o_ref, lse_ref, *, dilation, nb):
    _attn_span(q_ref, k_ref, kp_ref, v_ref, vp_ref, o_ref, lse_ref, dilation=dilation, nb=nb)


ATT_COMBINE_ROWS = 256


def _attn_combine_kernel(q_ref, k_ref, kp_ref, v_ref, vp_ref, o1, l1, o2, l2, out_ref, o3, l3, *, dilation, nb):
    _attn_span(q_ref, k_ref, kp_ref, v_ref, vp_ref, o3, l3, dilation=dilation, nb=nb)

    def body(c, carry):
        rows = pl.ds(pl.multiple_of(c * ATT_COMBINE_ROWS, ATT_COMBINE_ROWS), ATT_COMBINE_ROWS)
        for h in range(ATT_HEADS_PER_STEP):
            a, b, d = l1[h, rows, :], l2[h, rows, :], l3[h, rows, :]
            m = jnp.maximum(jnp.maximum(a, b), d)
            ea, eb, ed = jnp.exp(a - m), jnp.exp(b - m), jnp.exp(d - m)
            s = ea + eb + ed
            out = (ea / s) * o1[h, rows, :] + (eb / s) * o2[h, rows, :] + (ed / s) * o3[h, rows, :]
            out_ref[rows, h * ATT_HEAD_DIM:(h + 1) * ATT_HEAD_DIM] = out.astype(out_ref.dtype)
        return carry

    lax.fori_loop(0, out_ref.shape[0] // ATT_COMBINE_ROWS, body, 0)


def _attention(qkv, dilation, batch, seq, others=None):
    n_blk = qkv.shape[2]
    assert ATT_SPAN_BLOCKS % dilation == 0
    nb = ATT_SPAN_BLOCKS // dilation
    assert n_blk % nb == 0
    spans = n_blk // nb
    span_tokens = ATT_SPAN_BLOCKS * ATT_BLOCK
    hw = ATT_HEADS_PER_STEP * ATT_HEAD_DIM
    cur = (None, dilation, nb, ATT_BLOCK, hw)
    one = (None, dilation, 1, ATT_BLOCK, hw)

    def at(part):
        return lambda b, s, hp: (part, b, s, 0, hp)

    def before(part):
        return lambda b, s, hp: (part, b, jnp.maximum(s * nb - 1, 0), 0, hp)

    qkv_specs = [pl.BlockSpec(cur, at(0)), pl.BlockSpec(cur, at(1)), pl.BlockSpec(one, before(1)),
                 pl.BlockSpec(cur, at(2)), pl.BlockSpec(one, before(2))]
    per_head = jax.ShapeDtypeStruct((ATT_HEADS, batch * seq, ATT_HEAD_DIM), F32)
    head_blk = (ATT_HEADS_PER_STEP, span_tokens, ATT_HEAD_DIM)
    head_spec = pl.BlockSpec(head_blk, lambda b, s, hp: (hp, b * spans + s, 0))
    grid = (batch, spans, ATT_HEADS // ATT_HEADS_PER_STEP)
    if others is None:
        return pl.pallas_call(
            functools.partial(_attn_kernel, dilation=dilation, nb=nb),
            out_shape=(per_head, per_head),
            grid=grid,
            in_specs=qkv_specs,
            out_specs=(head_spec, head_spec),
            compiler_params=_params("parallel", "parallel", "parallel"),
            name=f"attn_d{dilation}",
        )(qkv, qkv, qkv, qkv, qkv)
    return pl.pallas_call(
        functools.partial(_attn_combine_kernel, dilation=dilation, nb=nb),
        out_shape=jax.ShapeDtypeStruct((batch * seq, ATT_GROUP_WIDTH), BF16),
        grid=grid,
        in_specs=qkv_specs + [head_spec] * 4,
        out_specs=pl.BlockSpec((span_tokens, hw), lambda b, s, hp: (b * spans + s, hp)),
        scratch_shapes=[pltpu.VMEM(head_blk, F32), pltpu.VMEM(head_blk, F32)],
        compiler_params=_params("parallel", "parallel", "parallel"),
        name=f"attn_d{dilation}_combine",
    )(qkv, qkv, qkv, qkv, qkv, *others)


RET_CHUNKS_PER_STEP = 8


def _ret_kernel(q_ref, k_ref, v_ref, g_ref, gn_ref, intra_ref, cross_ref, sdec_ref, cdec_ref,
                z_ref, state_ref, *, cb):
    @pl.when(pl.program_id(2) == 0)
    def _():
        state_ref[...] = jnp.zeros_like(state_ref)

    c = RET_CHUNK
    intra, cross_decay, state_decay, chunk_decay = intra_ref[...], cross_ref[...], sdec_ref[...], cdec_ref[...]
    gn = gn_ref[...]
    qs, inners, updates = [], [], []
    for t in range(cb):
        rows = slice(t * c, (t + 1) * c)
        q = q_ref[rows, :].astype(BF16)
        k = k_ref[rows, :]
        v = v_ref[rows, :]
        att = lax.dot_general(q, k.astype(BF16), (((1,), (1,)), ((), ())), preferred_element_type=F32) * intra
        inners.append(_mm(att.astype(BF16), v))
        kd = (k * state_decay).astype(BF16)
        updates.append(lax.dot_general(kd, v, (((0,), (0,)), ((), ())), preferred_element_type=F32))
        qs.append(q)
    state = state_ref[...]
    for t in range(cb):
        rows = slice(t * c, (t + 1) * c)
        r = inners[t] + _mm(qs[t], state.astype(BF16)) * cross_decay
        state = chunk_decay * state + updates[t]
        mu = jnp.mean(r, axis=-1, keepdims=True)
        var = jnp.mean(jnp.square(r - mu), axis=-1, keepdims=True)
        rn = (r - mu) * lax.rsqrt(var + GN_EPS) * gn
        g = g_ref[rows, :]
        z_ref[rows, :] = ((g * _sigmoid(g)) * rn).astype(z_ref.dtype)
    state_ref[...] = state


def _retention(qk, v, gates, gn_g, batch, seq):
    n = qk.shape[0]
    c = RET_CHUNK
    nc = seq // c
    cb = min(RET_CHUNKS_PER_STEP, nc)
    assert nc % cb == 0
    steps = nc // cb
    hh = jnp.arange(RET_HEADS, dtype=F32)
    log_gamma = jnp.log(1.0 - 2.0 ** (-5.0 - hh))
    idx = jnp.arange(c, dtype=F32)
    diff = idx[:, None] - idx[None, :]
    intra = jnp.where(diff >= 0, jnp.exp(jnp.maximum(diff, 0.0)[None] * log_gamma[:, None, None]), 0.0)
    cross_decay = jnp.exp((idx + 1.0)[None, :] * log_gamma[:, None])[..., None]
    state_decay = jnp.exp((c - 1.0 - idx)[None, :] * log_gamma[:, None])[..., None]
    chunk_decay = jnp.exp(c * log_gamma)[:, None, None]

    def rows(b, h, s):
        return b * steps + s

    kern = functools.partial(_ret_kernel, cb=cb)
    return pl.pallas_call(
        kern,
        out_shape=jax.ShapeDtypeStruct((n, RET_V_WIDTH), BF16),
        grid=(batch, RET_HEADS, steps),
        in_specs=[
            pl.BlockSpec((cb * c, RET_QK_DIM), lambda b, h, s: (rows(b, h, s), h)),
            pl.BlockSpec((cb * c, RET_QK_DIM), lambda b, h, s: (rows(b, h, s), RET_HEADS + h)),
            pl.BlockSpec((cb * c, RET_V_DIM), lambda b, h, s: (rows(b, h, s), h)),
            pl.BlockSpec((cb * c, RET_V_DIM), lambda b, h, s: (rows(b, h, s), h)),
            pl.BlockSpec((1, RET_V_DIM), lambda b, h, s: (0, h)),
            pl.BlockSpec((None, c, c), lambda b, h, s: (h, 0, 0)),
            pl.BlockSpec((None, c, 1), lambda b, h, s: (h, 0, 0)),
            pl.BlockSpec((None, c, 1), lambda b, h, s: (h, 0, 0)),
            pl.BlockSpec((None, 1, 1), lambda b, h, s: (h, 0, 0)),
        ],
        out_specs=pl.BlockSpec((cb * c, RET_V_DIM), lambda b, h, s: (rows(b, h, s), h)),
        scratch_shapes=[pltpu.VMEM((RET_QK_DIM, RET_V_DIM), F32)],
        compiler_params=_params("parallel", "parallel", "arbitrary"),
        name="retention",
    )(qk, qk, v, gates, gn_g, intra, cross_decay, state_decay, chunk_decay)


def _mix_kernel(oa_ref, z_ref, wa_ref, wr_ref, ga_ref, gr_ref, o_ref):
    ya = _mm(oa_ref[...], wa_ref[...])
    yr = _mm(z_ref[...], wr_ref[...])
    o_ref[...] = (_sigmoid(ga_ref[...]) * ya + _sigmoid(gr_ref[...]) * yr).astype(o_ref.dtype)


def _mix(o_att, z, wa, wr, gates, gate_col0):
    n = o_att.shape[0]
    d = wa.shape[1]
    tm, tn = _tile(n, 512), _tile(d, 1024)
    assert gate_col0 % tn == 0
    ga_off = gate_col0 // tn
    gr_off = ga_off + d // tn
    return pl.pallas_call(
        _mix_kernel,
        out_shape=jax.ShapeDtypeStruct((n, d), BF16),
        grid=(d // tn, n // tm),
        in_specs=[pl.BlockSpec((tm, o_att.shape[1]), lambda j, i: (i, 0)),
                  pl.BlockSpec((tm, z.shape[1]), lambda j, i: (i, 0)),
                  pl.BlockSpec((wa.shape[0], tn), lambda j, i: (0, j)),
                  pl.BlockSpec((wr.shape[0], tn), lambda j, i: (0, j)),
                  pl.BlockSpec((tm, tn), lambda j, i: (i, ga_off + j)),
                  pl.BlockSpec((tm, tn), lambda j, i: (i, gr_off + j))],
        out_specs=pl.BlockSpec((tm, tn), lambda j, i: (i, j)),
        compiler_params=_params("parallel", "parallel"),
        name="mix",
    )(o_att, z, wa, wr, gates, gates)


def _layer_norm_rows(t, g, b):
    mu = jnp.mean(t, axis=-1, keepdims=True)
    var = jnp.mean(jnp.square(t - mu), axis=-1, keepdims=True)
    return (t - mu) * lax.rsqrt(var + LN_EPS) * g + b


LN_ROW_CHUNK = 128


def _layer_norm_inplace(o_ref, g_ref, b_ref, cast_ref=None):
    g, b = g_ref[...], b_ref[...]
    chunk = min(LN_ROW_CHUNK, o_ref.shape[0])

    def body(r, carry):
        rows = pl.ds(pl.multiple_of(r * chunk, chunk), chunk)
        y = _layer_norm_rows(o_ref[rows, :], g, b)
        o_ref[rows, :] = y
        if cast_ref is not None:
            cast_ref[rows, :] = y.astype(cast_ref.dtype)
        return carry

    lax.fori_loop(0, o_ref.shape[0] // chunk, body, 0)


def _proj_ln_kernel(a_ref, w_ref, res_ref, g_ref, b_ref, of_ref, ob_ref, *, alpha):
    y = _layer_norm_rows(alpha * res_ref[...] + _mm(a_ref[...], w_ref[...]), g_ref[...], b_ref[...])
    of_ref[...] = y
    ob_ref[...] = y.astype(ob_ref.dtype)


def _proj_ln(a, w, res, g, b, alpha):
    n, k = a.shape
    d = w.shape[1]
    tm = _tile(n, 128)
    kern = functools.partial(_proj_ln_kernel, alpha=alpha)
    row = pl.BlockSpec((tm, d), lambda i: (i, 0))
    vec = pl.BlockSpec((1, d), lambda i: (0, 0))
    return pl.pallas_call(
        kern,
        out_shape=(jax.ShapeDtypeStruct((n, d), F32), jax.ShapeDtypeStruct((n, d), BF16)),
        grid=(n // tm,),
        in_specs=[pl.BlockSpec((tm, k), lambda i: (i, 0)),
                  pl.BlockSpec((k, d), lambda i: (0, 0), pipeline_mode=pl.Buffered(1)),
                  row, vec, vec],
        out_specs=(row, row),
        compiler_params=_params("parallel"),
        name="proj_ln",
    )(a, w, res, g, b)


SWIGLU_ROW_SPLITS = 2


def _swiglu_kernel(h_ref, wg_ref, wu_ref, o_ref):
    wg, wu = wg_ref[...], wu_ref[...]
    rows = h_ref.shape[0] // SWIGLU_ROW_SPLITS
    for c in range(SWIGLU_ROW_SPLITS):
        h = h_ref[c * rows:(c + 1) * rows, :]
        gate = _mm(h, wg)
        up = _mm(h, wu)
        o_ref[c * rows:(c + 1) * rows, :] = ((gate * _sigmoid(gate)) * up).astype(o_ref.dtype)


def _swiglu(h_bf, wg, wu):
    n, k = h_bf.shape
    f = wg.shape[1]
    tm, tn = _tile(n, 2048), _tile(f, 256)
    w_spec = pl.BlockSpec((k, tn), lambda i, j: (0, j))
    return pl.pallas_call(
        _swiglu_kernel,
        out_shape=jax.ShapeDtypeStruct((n, f), BF16),
        grid=(n // tm, f // tn),
        in_specs=[pl.BlockSpec((tm, k), lambda i, j: (i, 0)), w_spec, w_spec],
        out_specs=pl.BlockSpec((tm, tn), lambda i, j: (i, j)),
        compiler_params=_params("parallel", "parallel"),
        name="swiglu",
    )(h_bf, wg, wu)


def _ple_kernel(hb_ref, wg_ref, p_ref, wu_ref, hf_ref, o_ref, *, alpha):
    gate = _sigmoid(_mm(hb_ref[...], wg_ref[...]))
    up = _mm(p_ref[...], wu_ref[...])
    o_ref[...] = alpha * hf_ref[...] + gate * up


def _ple_base(h_bf, h_f32, p_bf, wpg, wpu, alpha):
    n, k = h_bf.shape
    d = wpg.shape[1]
    tm, tn = _tile(n, 512), _tile(d, 1024)
    kern = functools.partial(_ple_kernel, alpha=alpha)
    return pl.pallas_call(
        kern,
        out_shape=jax.ShapeDtypeStruct((n, d), F32),
        grid=(d // tn, n // tm),
        in_specs=[pl.BlockSpec((tm, k), lambda j, i: (i, 0)),
                  pl.BlockSpec((k, tn), lambda j, i: (0, j)),
                  pl.BlockSpec((tm, p_bf.shape[1]), lambda j, i: (i, 0)),
                  pl.BlockSpec((p_bf.shape[1], tn), lambda j, i: (0, j)),
                  pl.BlockSpec((tm, tn), lambda j, i: (i, j))],
        out_specs=pl.BlockSpec((tm, tn), lambda j, i: (i, j)),
        compiler_params=_params("parallel", "parallel"),
        name="ple_base",
    )(h_bf, wpg, p_bf, wpu, h_f32)


def _down_ln_kernel(a_ref, w_ref, base_ref, g_ref, b_ref, o_ref, *, tn):
    j = pl.program_id(1)
    col = pl.multiple_of(j * tn, tn)
    o_ref[:, pl.ds(col, tn)] = base_ref[...] + _mm(a_ref[...], w_ref[...])

    @pl.when(j == pl.num_programs(1) - 1)
    def _():
        _layer_norm_inplace(o_ref, g_ref, b_ref)


def _down_ln(hidden, wd, base, g, b):
    n, f = hidden.shape
    d = wd.shape[1]
    tm, tn = _tile(n, 512), _tile(d, 256)
    kern = functools.partial(_down_ln_kernel, tn=tn)
    vec = pl.BlockSpec((1, d), lambda i, j: (0, 0))
    return pl.pallas_call(
        kern,
        out_shape=jax.ShapeDtypeStruct((n, d), F32),
        grid=(n // tm, d // tn),
        in_specs=[pl.BlockSpec((tm, f), lambda i, j: (i, 0)),
                  pl.BlockSpec((f, tn), lambda i, j: (0, j)),
                  pl.BlockSpec((tm, tn), lambda i, j: (i, j)),
                  vec, vec],
        out_specs=pl.BlockSpec((tm, d), lambda i, j: (i, 0)),
        compiler_params=_params("parallel", "arbitrary"),
        name="down_ln",
    )(hidden, wd, base, g, b)


def _layer(h_f32, h_bf, p_bf, w_in, w_attn_out, w_ret_out, ret_gn_g, w_o, ln1_g, ln1_b,
           w_ffn_gate, w_ffn_up, w_ffn_down, w_ple_gate, w_ple_up, ln2_g, ln2_b,
           cos, sin, batch, seq, alpha):
    d_model = h_f32.shape[1]
    c_qr = 3 * ATT_QKV_WIDTH
    c_vr = c_qr + 2 * RET_QK_WIDTH
    c_gr = c_vr + RET_V_WIDTH
    assert w_in.shape[1] == c_gr + RET_V_WIDTH + 2 * d_model

    qk_r = _proj_rot(h_bf, w_in, c_qr, cos, sin, seq)
    v_r, w_attn_out, w_ret_out, w_o, w_ple_gate = _proj(
        h_bf, w_in, c_vr, RET_V_WIDTH, BF16, "proj_vr", cast=(w_attn_out, w_ret_out, w_o, w_ple_gate))
    gates, w_ffn_down = _proj(
        h_bf, w_in, c_gr, RET_V_WIDTH + 2 * d_model, F32, "proj_gates", cast=(w_ffn_down,))

    others = []
    for g, dil in enumerate(ATT_DILATIONS[:-1]):
        others.extend(_attention(_proj_units(h_bf, w_in, g, dil, batch, seq), dil, batch, seq))
    g, dil = len(ATT_DILATIONS) - 1, ATT_DILATIONS[-1]
    o_att = _attention(_proj_units(h_bf, w_in, g, dil, batch, seq), dil, batch, seq, others=others)

    z = _retention(qk_r, v_r, gates, ret_gn_g, batch, seq)

    mixed = _mix(o_att, z, w_attn_out, w_ret_out, gates, RET_V_WIDTH)
    h1_f32, h1_bf = _proj_ln(mixed, w_o, h_f32, ln1_g, ln1_b, alpha)

    hidden = _swiglu(h1_bf, w_ffn_gate, w_ffn_up)
    base = _ple_base(h1_bf, h1_f32, p_bf, w_ple_gate, w_ple_up, alpha)
    return _down_ln(hidden, w_ffn_down, base, ln2_g, ln2_b)


def kernel(x, p, w_in, w_attn_out, w_ret_out, ret_gn_g, w_o, ln1_g, ln1_b, w_ffn_gate, w_ffn_up,
           w_ffn_down, w_ple_gate, w_ple_up, ln2_g, ln2_b):
    batch, seq, d_model = x.shape
    depth = w_in.shape[0]
    n = batch * seq
    alpha = (2 * depth) ** 0.25

    half = RET_QK_DIM // 2
    pos = jnp.arange(seq, dtype=F32)
    inv_freq = RET_ROPE_BASE ** (-jnp.arange(half, dtype=F32) / half)
    ang = pos[:, None] * inv_freq[None, :]
    cos, sin = jnp.cos(ang), jnp.sin(ang)

    h = x.reshape(n, d_model)
    for i in range(depth):
        bf = lambda w: w[i].astype(BF16)
        row = lambda v: v[i].reshape(1, -1)
        h = _layer(h, h.astype(BF16), p[i].reshape(n, -1).astype(BF16),
                   bf(w_in), w_attn_out[i], w_ret_out[i], row(ret_gn_g), w_o[i], row(ln1_g), row(ln1_b),
                   w_ffn_gate[i], w_ffn_up[i], w_ffn_down[i], w_ple_gate[i], bf(w_ple_up),
                   row(ln2_g), row(ln2_b), cos, sin, batch, seq, alpha)
    return h.reshape(batch, seq, d_model).astype(x.dtype)
```

```python
import functools

import jax
import jax.numpy as jnp
from jax import lax
from jax.experimental import pallas as pl
from jax.experimental.pallas import tpu as pltpu

ATT_HEAD_DIM = 128
ATT_HEADS = 8
ATT_DILATIONS = (1, 4, 16)
ATT_BLOCK = 128
ATT_GROUP_WIDTH = ATT_HEADS * ATT_HEAD_DIM
ATT_QKV_WIDTH = len(ATT_DILATIONS) * ATT_GROUP_WIDTH
RET_HEADS = 8
RET_QK_DIM = 256
RET_V_DIM = 512
RET_QK_WIDTH = RET_HEADS * RET_QK_DIM
RET_V_WIDTH = RET_HEADS * RET_V_DIM
RET_CHUNK = 128
RET_ROPE_BASE = 10000.0
LN_EPS = 1e-5
GN_EPS = 1e-6
NEG_INF = -1e30

V7X_VMEM_LIMIT_BYTES = 56 * 1024 * 1024
LANE = 128

BF16 = jnp.bfloat16
F32 = jnp.float32


def _params(*semantics):
    return pltpu.CompilerParams(dimension_semantics=semantics, vmem_limit_bytes=V7X_VMEM_LIMIT_BYTES)


def _tile(dim, pref):
    if dim <= pref:
        return dim
    t = (pref // LANE) * LANE
    while t > LANE and dim % t:
        t -= LANE
    assert dim % t == 0, (dim, pref)
    return t


def _sigmoid(v):
    return 1.0 / (1.0 + jnp.exp(-v))


def _mm(a, b):
    return lax.dot_general(a, b, (((1,), (0,)), ((), ())), preferred_element_type=F32)


ROW_SPLITS = 2


def _row_halves(n_rows):
    rows = n_rows // ROW_SPLITS
    return [slice(c * rows, (c + 1) * rows) for c in range(ROW_SPLITS)]


BF16_SUBLANES = 16


def _proj_kernel(x_ref, w_ref, *refs):
    n_cast = (len(refs) - 1) // 2
    o_ref = refs[n_cast]
    o_ref[...] = _mm(x_ref[...], w_ref[...]).astype(o_ref.dtype)
    for src, dst in zip(refs[:n_cast], refs[n_cast + 1:]):
        dst[...] = src[...].astype(dst.dtype)


def _proj(x_bf, w_bf, col0, width, out_dtype, name, cast=()):
    n, k = x_bf.shape
    tm, tn = _tile(n, 1024), _tile(width, 1024)
    assert col0 % tn == 0
    off = col0 // tn
    n_i = n // tm
    steps = (width // tn) * n_i
    cast_specs, cast_shapes = [], []
    for w in cast:
        rows, cols = w.shape
        slab = -(-(-(-rows // steps)) // BF16_SUBLANES) * BF16_SUBLANES
        while rows % slab:
            slab += BF16_SUBLANES
        last = rows // slab - 1
        cast_specs.append(pl.BlockSpec((slab, cols), lambda j, i, last=last: (jnp.minimum(j * n_i + i, last), 0)))
        cast_shapes.append(jax.ShapeDtypeStruct(w.shape, BF16))
    outs = pl.pallas_call(
        _proj_kernel,
        out_shape=[jax.ShapeDtypeStruct((n, width), out_dtype)] + cast_shapes,
        grid=(width // tn, n_i),
        in_specs=[pl.BlockSpec((tm, k), lambda j, i: (i, 0)),
                  pl.BlockSpec((k, tn), lambda j, i: (0, off + j))] + cast_specs,
        out_specs=[pl.BlockSpec((tm, tn), lambda j, i: (i, j))] + cast_specs,
        compiler_params=_params("arbitrary", "arbitrary"),
        name=name,
    )(x_bf, w_bf, *cast)
    return outs if cast else outs[0]


def _proj_rot_kernel(x_ref, w_ref, cos_ref, sin_ref, o_ref, *, n_q_blocks, k_scale):
    j = pl.program_id(0)
    acc = _mm(x_ref[...], w_ref[...])
    scale = jnp.where(j >= n_q_blocks, k_scale, 1.0).astype(F32)
    cos = cos_ref[...]
    sin = sin_ref[...]
    half = RET_QK_DIM // 2
    for h in range(acc.shape[1] // RET_QK_DIM):
        lo = h * RET_QK_DIM
        t1 = acc[:, lo:lo + half]
        t2 = acc[:, lo + half:lo + RET_QK_DIM]
        o_ref[:, lo:lo + half] = (t1 * cos - t2 * sin) * scale
        o_ref[:, lo + half:lo + RET_QK_DIM] = (t1 * sin + t2 * cos) * scale


def _proj_rot(x_bf, w_bf, col0, cos, sin, seq):
    n, k = x_bf.shape
    width = 2 * RET_QK_WIDTH
    tm, tn = _tile(seq, 1024), 1024
    assert col0 % tn == 0 and RET_QK_WIDTH % tn == 0 and n % seq == 0
    off = col0 // tn
    pos_blocks = seq // tm
    kern = functools.partial(_proj_rot_kernel, n_q_blocks=RET_QK_WIDTH // tn, k_scale=RET_QK_DIM ** -0.5)
    half = RET_QK_DIM // 2
    return pl.pallas_call(
        kern,
        out_shape=jax.ShapeDtypeStruct((n, width), F32),
        grid=(width // tn, n // tm),
        in_specs=[pl.BlockSpec((tm, k), lambda j, i: (i, 0)),
                  pl.BlockSpec((k, tn), lambda j, i: (0, off + j)),
                  pl.BlockSpec((tm, half), lambda j, i: (i % pos_blocks, 0)),
                  pl.BlockSpec((tm, half), lambda j, i: (i % pos_blocks, 0))],
        out_specs=pl.BlockSpec((tm, tn), lambda j, i: (i, j)),
        compiler_params=_params("parallel", "parallel"),
        name="proj_rot",
    )(x_bf, w_bf, cos, sin)


ATT_SPAN_BLOCKS = 16
ATT_HEADS_PER_STEP = 4
ATT_HEADS_PER_COMBINE_STEP = 2


def _proj_units_kernel(x_ref, w_ref, o_ref, acc_ref, *, dilation):
    if dilation == 1:
        o_ref[0, 0] = _mm(x_ref[...], w_ref[...]).astype(o_ref.dtype)
        return
    w = w_ref[...]
    slabs = w.shape[1] // LANE
    for tok in _row_halves(x_ref.shape[0]):
        acc = _mm(x_ref[tok, :], w)
        for s in range(slabs):
            acc_ref[s, tok, :] = acc[:, s * LANE:(s + 1) * LANE]
        rows = acc.shape[0] // dilation
        dst = slice(tok.start // dilation, tok.start // dilation + rows)
        for r in range(dilation):
            for s in range(slabs):
                piece = acc_ref[s, pl.ds(tok.start + r, rows, stride=dilation), :]
                o_ref[r, 0, dst, s * LANE:(s + 1) * LANE] = piece.astype(o_ref.dtype)


def _proj_units(x_bf, w_bf, group, dilation, batch, seq):
    n, k = x_bf.shape
    tm, tn = _tile(seq, 1024), ATT_GROUP_WIDTH
    assert (tm // ROW_SPLITS) % (dilation * BF16_SUBLANES) == 0 and seq % tm == 0
    rows = tm // dilation
    tiles = seq // tm
    n_groups = len(ATT_DILATIONS)
    kern = functools.partial(_proj_units_kernel, dilation=dilation)
    out = pl.pallas_call(
        kern,
        out_shape=jax.ShapeDtypeStruct((3, batch * dilation, tiles, rows, tn), BF16),
        grid=(3, n // tm),
        in_specs=[pl.BlockSpec((tm, k), lambda j, i: (i, 0)),
                  pl.BlockSpec((k, tn), lambda j, i: (0, j * n_groups + group))],
        out_specs=pl.BlockSpec((None, dilation, 1, rows, tn), lambda j, i: (j, i // tiles, i % tiles, 0, 0)),
        scratch_shapes=[pltpu.VMEM((tn // LANE, tm, LANE), F32)],
        compiler_params=_params("parallel", "parallel"),
        name=f"proj_att_d{dilation}",
    )(x_bf, w_bf)
    n_blk = seq // (dilation * ATT_BLOCK)
    return out.reshape(3, batch * dilation, n_blk, ATT_BLOCK, tn)


def _attn_span(q_ref, k_ref, kp_ref, v_ref, vp_ref, o_ref, lse_ref, *, dilation, nb):
    qi = lax.broadcasted_iota(jnp.int32, (ATT_BLOCK, 2 * ATT_BLOCK), 0)
    kj = lax.broadcasted_iota(jnp.int32, (ATT_BLOCK, 2 * ATT_BLOCK), 1)
    in_window = kj <= qi + ATT_BLOCK
    valid_inner = jnp.logical_and(kj >= qi, in_window)
    lo = jnp.where(pl.program_id(1) > 0, 0, ATT_BLOCK)
    valid_first = jnp.logical_and(kj >= jnp.maximum(qi, lo), in_window)
    scale = ATT_HEAD_DIM ** -0.5
    dn = (((1,), (1,)), ((), ()))
    ones = jnp.ones((2 * ATT_BLOCK, ATT_HEAD_DIM), BF16)
    units = [(r, i, h) for r in range(dilation) for i in range(nb) for h in range(o_ref.shape[0])]

    def cols(h):
        return slice(h * ATT_HEAD_DIM, (h + 1) * ATT_HEAD_DIM)

    def with_prev(cur_ref, prev_ref, r, i, h):
        prev = prev_ref[r, 0, :, cols(h)] if i == 0 else cur_ref[r, i - 1, :, cols(h)]
        return jnp.concatenate([prev, cur_ref[r, i, :, cols(h)]], axis=0)

    scores = [lax.dot_general(q_ref[r, i, :, cols(h)], with_prev(k_ref, kp_ref, r, i, h), dn,
                              preferred_element_type=F32) * scale for r, i, h in units]
    probs, maxes = [], []
    for (r, i, h), s in zip(units, scores):
        s = jnp.where(valid_first if i == 0 else valid_inner, s, NEG_INF)
        m = jnp.max(s, axis=-1, keepdims=True)
        probs.append(jnp.exp(s - m).astype(BF16))
        maxes.append(m)
    for (r, i, h), p, m in zip(units, probs, maxes):
        v_aug = jnp.concatenate([with_prev(v_ref, vp_ref, r, i, h), ones], axis=1)
        acc = _mm(p, v_aug)
        den = acc[:, ATT_HEAD_DIM:]
        if dilation == 1:
            rows = pl.ds(i * ATT_BLOCK, ATT_BLOCK)
        else:
            rows = pl.ds(i * ATT_BLOCK * dilation + r, ATT_BLOCK, stride=dilation)
        o_ref[h, rows, :] = acc[:, :ATT_HEAD_DIM] / den
        lse_ref[h, rows, :] = m + jnp.log(den)


def _attn_kernel(q_ref, k_ref, kp_ref, v_ref, vp_ref, o_ref, lse_ref, *, dilation, nb):
    _attn_span(q_ref, k_ref, kp_ref, v_ref, vp_ref, o_ref, lse_ref, dilation=dilation, nb=nb)


ATT_COMBINE_ROWS = 256


def _attn_combine_kernel(q_ref, k_ref, kp_ref, v_ref, vp_ref, o1, l1, o2, l2, out_ref, o3, l3, *, dilation, nb):
    _attn_span(q_ref, k_ref, kp_ref, v_ref, vp_ref, o3, l3, dilation=dilation, nb=nb)

    def body(c, carry):
        rows = pl.ds(pl.multiple_of(c * ATT_COMBINE_ROWS, ATT_COMBINE_ROWS), ATT_COMBINE_ROWS)
        for h in range(o3.shape[0]):
            a, b, d = l1[h, rows, :], l2[h, rows, :], l3[h, rows, :]
            m = jnp.maximum(jnp.maximum(a, b), d)
            ea, eb, ed = jnp.exp(a - m), jnp.exp(b - m), jnp.exp(d - m)
            s = ea + eb + ed
            out = (ea / s) * o1[h, rows, :] + (eb / s) * o2[h, rows, :] + (ed / s) * o3[h, rows, :]
            out_ref[rows, h * ATT_HEAD_DIM:(h + 1) * ATT_HEAD_DIM] = out.astype(out_ref.dtype)
        return carry

    lax.fori_loop(0, out_ref.shape[0] // ATT_COMBINE_ROWS, body, 0)


def _attention(qkv, dilation, batch, seq, others=None):
    n_blk = qkv.shape[2]
    assert ATT_SPAN_BLOCKS % dilation == 0
    nb = ATT_SPAN_BLOCKS // dilation
    assert n_blk % nb == 0
    spans = n_blk // nb
    span_tokens = ATT_SPAN_BLOCKS * ATT_BLOCK
    heads = ATT_HEADS_PER_STEP if others is None else ATT_HEADS_PER_COMBINE_STEP
    hw = heads * ATT_HEAD_DIM
    cur = (None, dilation, nb, ATT_BLOCK, hw)
    one = (None, dilation, 1, ATT_BLOCK, hw)

    def at(part):
        return lambda b, s, hp: (part, b, s, 0, hp)

    def before(part):
        return lambda b, s, hp: (part, b, jnp.maximum(s * nb - 1, 0), 0, hp)

    qkv_specs = [pl.BlockSpec(cur, at(0)), pl.BlockSpec(cur, at(1)), pl.BlockSpec(one, before(1)),
                 pl.BlockSpec(cur, at(2)), pl.BlockSpec(one, before(2))]
    per_head = jax.ShapeDtypeStruct((ATT_HEADS, batch * seq, ATT_HEAD_DIM), F32)
    head_blk = (heads, span_tokens, ATT_HEAD_DIM)
    head_spec = pl.BlockSpec(head_blk, lambda b, s, hp: (hp, b * spans + s, 0))
    grid = (batch, spans, ATT_HEADS // heads)
    if others is None:
        return pl.pallas_call(
            functools.partial(_attn_kernel, dilation=dilation, nb=nb),
            out_shape=(per_head, per_head),
            grid=grid,
            in_specs=qkv_specs,
            out_specs=(head_spec, head_spec),
            compiler_params=_params("parallel", "parallel", "parallel"),
            name=f"attn_d{dilation}",
        )(qkv, qkv, qkv, qkv, qkv)
    return pl.pallas_call(
        functools.partial(_attn_combine_kernel, dilation=dilation, nb=nb),
        out_shape=jax.ShapeDtypeStruct((batch * seq, ATT_GROUP_WIDTH), BF16),
        grid=grid,
        in_specs=qkv_specs + [head_spec] * 4,
        out_specs=pl.BlockSpec((span_tokens, hw), lambda b, s, hp: (b * spans + s, hp)),
        scratch_shapes=[pltpu.VMEM(head_blk, F32), pltpu.VMEM(head_blk, F32)],
        compiler_params=_params("parallel", "parallel", "parallel"),
        name=f"attn_d{dilation}_combine",
    )(qkv, qkv, qkv, qkv, qkv, *others)


RET_CHUNKS_PER_STEP = 16


def _ret_kernel(q_ref, k_ref, v_ref, g_ref, gn_ref, intra_ref, cross_ref, sdec_ref, cdec_ref,
                z_ref, state_ref, *, cb):
    @pl.when(pl.program_id(2) == 0)
    def _():
        state_ref[...] = jnp.zeros_like(state_ref)

    c = RET_CHUNK
    intra, cross_decay, state_decay, chunk_decay = intra_ref[...], cross_ref[...], sdec_ref[...], cdec_ref[...]
    gn = gn_ref[...]
    qs, inners, updates = [], [], []
    for t in range(cb):
        rows = slice(t * c, (t + 1) * c)
        q = q_ref[rows, :].astype(BF16)
        k = k_ref[rows, :]
        v = v_ref[rows, :]
        att = lax.dot_general(q, k.astype(BF16), (((1,), (1,)), ((), ())), preferred_element_type=F32) * intra
        inners.append(_mm(att.astype(BF16), v))
        kd = (k * state_decay).astype(BF16)
        updates.append(lax.dot_general(kd, v, (((0,), (0,)), ((), ())), preferred_element_type=F32))
        qs.append(q)
    state = state_ref[...]
    for t in range(cb):
        rows = slice(t * c, (t + 1) * c)
        r = inners[t] + _mm(qs[t], state.astype(BF16)) * cross_decay
        state = chunk_decay * state + updates[t]
        mu = jnp.mean(r, axis=-1, keepdims=True)
        var = jnp.mean(jnp.square(r - mu), axis=-1, keepdims=True)
        rn = (r - mu) * lax.rsqrt(var + GN_EPS) * gn
        g = g_ref[rows, :]
        z_ref[rows, :] = ((g * _sigmoid(g)) * rn).astype(z_ref.dtype)
    state_ref[...] = state


def _retention(qk, v, gates, gn_g, batch, seq):
    n = qk.shape[0]
    c = RET_CHUNK
    nc = seq // c
    cb = min(RET_CHUNKS_PER_STEP, nc)
    assert nc % cb == 0
    steps = nc // cb
    hh = jnp.arange(RET_HEADS, dtype=F32)
    log_gamma = jnp.log(1.0 - 2.0 ** (-5.0 - hh))
    idx = jnp.arange(c, dtype=F32)
    diff = idx[:, None] - idx[None, :]
    intra = jnp.where(diff >= 0, jnp.exp(jnp.maximum(diff, 0.0)[None] * log_gamma[:, None, None]), 0.0)
    cross_decay = jnp.exp((idx + 1.0)[None, :] * log_gamma[:, None])[..., None]
    state_decay = jnp.exp((c - 1.0 - idx)[None, :] * log_gamma[:, None])[..., None]
    chunk_decay = jnp.exp(c * log_gamma)[:, None, None]

    def rows(b, h, s):
        return b * steps + s

    kern = functools.partial(_ret_kernel, cb=cb)
    return pl.pallas_call(
        kern,
        out_shape=jax.ShapeDtypeStruct((n, RET_V_WIDTH), BF16),
        grid=(batch, RET_HEADS, steps),
        in_specs=[
            pl.BlockSpec((cb * c, RET_QK_DIM), lambda b, h, s: (rows(b, h, s), h)),
            pl.BlockSpec((cb * c, RET_QK_DIM), lambda b, h, s: (rows(b, h, s), RET_HEADS + h)),
            pl.BlockSpec((cb * c, RET_V_DIM), lambda b, h, s: (rows(b, h, s), h)),
            pl.BlockSpec((cb * c, RET_V_DIM), lambda b, h, s: (rows(b, h, s), h)),
            pl.BlockSpec((1, RET_V_DIM), lambda b, h, s: (0, h)),
            pl.BlockSpec((None, c, c), lambda b, h, s: (h, 0, 0)),
            pl.BlockSpec((None, c, 1), lambda b, h, s: (h, 0, 0)),
            pl.BlockSpec((None, c, 1), lambda b, h, s: (h, 0, 0)),
            pl.BlockSpec((None, 1, 1), lambda b, h, s: (h, 0, 0)),
        ],
        out_specs=pl.BlockSpec((cb * c, RET_V_DIM), lambda b, h, s: (rows(b, h, s), h)),
        scratch_shapes=[pltpu.VMEM((RET_QK_DIM, RET_V_DIM), F32)],
        compiler_params=_params("parallel", "parallel", "arbitrary"),
        name="retention",
    )(qk, qk, v, gates, gn_g, intra, cross_decay, state_decay, chunk_decay)


def _mix_kernel(oa_ref, z_ref, wa_ref, wr_ref, ga_ref, gr_ref, o_ref):
    wa, wr = wa_ref[...], wr_ref[...]
    for rows in _row_halves(o_ref.shape[0]):
        ya = _mm(oa_ref[rows, :], wa)
        yr = _mm(z_ref[rows, :], wr)
        o_ref[rows, :] = (_sigmoid(ga_ref[rows, :]) * ya + _sigmoid(gr_ref[rows, :]) * yr).astype(o_ref.dtype)


def _mix(o_att, z, wa, wr, gates, gate_col0):
    n = o_att.shape[0]
    d = wa.shape[1]
    tm, tn = _tile(n, 512), _tile(d, 1024)
    assert gate_col0 % tn == 0
    ga_off = gate_col0 // tn
    gr_off = ga_off + d // tn
    return pl.pallas_call(
        _mix_kernel,
        out_shape=jax.ShapeDtypeStruct((n, d), BF16),
        grid=(d // tn, n // tm),
        in_specs=[pl.BlockSpec((tm, o_att.shape[1]), lambda j, i: (i, 0)),
                  pl.BlockSpec((tm, z.shape[1]), lambda j, i: (i, 0)),
                  pl.BlockSpec((wa.shape[0], tn), lambda j, i: (0, j)),
                  pl.BlockSpec((wr.shape[0], tn), lambda j, i: (0, j)),
                  pl.BlockSpec((tm, tn), lambda j, i: (i, ga_off + j)),
                  pl.BlockSpec((tm, tn), lambda j, i: (i, gr_off + j))],
        out_specs=pl.BlockSpec((tm, tn), lambda j, i: (i, j)),
        compiler_params=_params("parallel", "parallel"),
        name="mix",
    )(o_att, z, wa, wr, gates, gates)


def _layer_norm_rows(t, g, b):
    mu = jnp.mean(t, axis=-1, keepdims=True)
    var = jnp.mean(jnp.square(t - mu), axis=-1, keepdims=True)
    return (t - mu) * lax.rsqrt(var + LN_EPS) * g + b


LN_ROW_CHUNK = 128


def _layer_norm_inplace(o_ref, g_ref, b_ref, cast_ref=None):
    g, b = g_ref[...], b_ref[...]
    chunk = min(LN_ROW_CHUNK, o_ref.shape[0])

    def body(r, carry):
        rows = pl.ds(pl.multiple_of(r * chunk, chunk), chunk)
        y = _layer_norm_rows(o_ref[rows, :], g, b)
        o_ref[rows, :] = y
        if cast_ref is not None:
            cast_ref[rows, :] = y.astype(cast_ref.dtype)
        return carry

    lax.fori_loop(0, o_ref.shape[0] // chunk, body, 0)


def _proj_ln_kernel(a_ref, w_ref, res_ref, g_ref, b_ref, of_ref, ob_ref, *, alpha):
    y = _layer_norm_rows(alpha * res_ref[...] + _mm(a_ref[...], w_ref[...]), g_ref[...], b_ref[...])
    of_ref[...] = y
    ob_ref[...] = y.astype(ob_ref.dtype)


def _proj_ln(a, w, res, g, b, alpha):
    n, k = a.shape
    d = w.shape[1]
    tm = _tile(n, 128)
    kern = functools.partial(_proj_ln_kernel, alpha=alpha)
    row = pl.BlockSpec((tm, d), lambda i: (i, 0))
    vec = pl.BlockSpec((1, d), lambda i: (0, 0))
    return pl.pallas_call(
        kern,
        out_shape=(jax.ShapeDtypeStruct((n, d), F32), jax.ShapeDtypeStruct((n, d), BF16)),
        grid=(n // tm,),
        in_specs=[pl.BlockSpec((tm, k), lambda i: (i, 0)),
                  pl.BlockSpec((k, d), lambda i: (0, 0), pipeline_mode=pl.Buffered(1)),
                  row, vec, vec],
        out_specs=(row, row),
        compiler_params=_params("parallel"),
        name="proj_ln",
    )(a, w, res, g, b)


def _swiglu_kernel(h_ref, wg_ref, wu_ref, o_ref):
    wg, wu = wg_ref[...], wu_ref[...]
    for rows in _row_halves(o_ref.shape[0]):
        h = h_ref[rows, :]
        gate = _mm(h, wg)
        up = _mm(h, wu)
        o_ref[rows, :] = ((gate * _sigmoid(gate)) * up).astype(o_ref.dtype)


def _swiglu(h_bf, wg, wu):
    n, k = h_bf.shape
    f = wg.shape[1]
    tm, tn = _tile(n, 2048), _tile(f, 256)
    w_spec = pl.BlockSpec((k, tn), lambda i, j: (0, j))
    return pl.pallas_call(
        _swiglu_kernel,
        out_shape=jax.ShapeDtypeStruct((n, f), BF16),
        grid=(n // tm, f // tn),
        in_specs=[pl.BlockSpec((tm, k), lambda i, j: (i, 0)), w_spec, w_spec],
        out_specs=pl.BlockSpec((tm, tn), lambda i, j: (i, j)),
        compiler_params=_params("parallel", "parallel"),
        name="swiglu",
    )(h_bf, wg, wu)


def _ple_kernel(hb_ref, wg_ref, p_ref, wu_ref, hf_ref, o_ref, *, alpha):
    wg, wu = wg_ref[...], wu_ref[...]
    for rows in _row_halves(o_ref.shape[0]):
        gate = _sigmoid(_mm(hb_ref[rows, :], wg))
        up = _mm(p_ref[rows, :], wu)
        o_ref[rows, :] = alpha * hf_ref[rows, :] + gate * up


def _ple_base(h_bf, h_f32, p_bf, wpg, wpu, alpha):
    n, k = h_bf.shape
    d = wpg.shape[1]
    tm, tn = _tile(n, 512), _tile(d, 1024)
    kern = functools.partial(_ple_kernel, alpha=alpha)
    return pl.pallas_call(
        kern,
        out_shape=jax.ShapeDtypeStruct((n, d), F32),
        grid=(d // tn, n // tm),
        in_specs=[pl.BlockSpec((tm, k), lambda j, i: (i, 0)),
                  pl.BlockSpec((k, tn), lambda j, i: (0, j)),
                  pl.BlockSpec((tm, p_bf.shape[1]), lambda j, i: (i, 0)),
                  pl.BlockSpec((p_bf.shape[1], tn), lambda j, i: (0, j)),
                  pl.BlockSpec((tm, tn), lambda j, i: (i, j))],
        out_specs=pl.BlockSpec((tm, tn), lambda j, i: (i, j)),
        compiler_params=_params("parallel", "parallel"),
        name="ple_base",
    )(h_bf, wpg, p_bf, wpu, h_f32)


def _down_ln_kernel(a_ref, w_ref, base_ref, g_ref, b_ref, o_ref, *, tn):
    j = pl.program_id(1)
    col = pl.multiple_of(j * tn, tn)
    o_ref[:, pl.ds(col, tn)] = base_ref[...] + _mm(a_ref[...], w_ref[...])

    @pl.when(j == pl.num_programs(1) - 1)
    def _():
        _layer_norm_inplace(o_ref, g_ref, b_ref)


def _down_ln(hidden, wd, base, g, b):
    n, f = hidden.shape
    d = wd.shape[1]
    tm, tn = _tile(n, 512), _tile(d, 256)
    kern = functools.partial(_down_ln_kernel, tn=tn)
    vec = pl.BlockSpec((1, d), lambda i, j: (0, 0))
    return pl.pallas_call(
        kern,
        out_shape=jax.ShapeDtypeStruct((n, d), F32),
        grid=(n // tm, d // tn),
        in_specs=[pl.BlockSpec((tm, f), lambda i, j: (i, 0)),
                  pl.BlockSpec((f, tn), lambda i, j: (0, j)),
                  pl.BlockSpec((tm, tn), lambda i, j: (i, j)),
                  vec, vec],
        out_specs=pl.BlockSpec((tm, d), lambda i, j: (i, 0)),
        compiler_params=_params("parallel", "arbitrary"),
        name="down_ln",
    )(hidden, wd, base, g, b)


def _layer(h_f32, h_bf, p_bf, w_in, w_attn_out, w_ret_out, ret_gn_g, w_o, ln1_g, ln1_b,
           w_ffn_gate, w_ffn_up, w_ffn_down, w_ple_gate, w_ple_up, ln2_g, ln2_b,
           cos, sin, batch, seq, alpha):
    d_model = h_f32.shape[1]
    c_qr = 3 * ATT_QKV_WIDTH
    c_vr = c_qr + 2 * RET_QK_WIDTH
    c_gr = c_vr + RET_V_WIDTH
    assert w_in.shape[1] == c_gr + RET_V_WIDTH + 2 * d_model

    qk_r = _proj_rot(h_bf, w_in, c_qr, cos, sin, seq)
    v_r, w_attn_out, w_ret_out, w_o, w_ple_gate = _proj(
        h_bf, w_in, c_vr, RET_V_WIDTH, BF16, "proj_vr", cast=(w_attn_out, w_ret_out, w_o, w_ple_gate))
    gates, w_ffn_down = _proj(
        h_bf, w_in, c_gr, RET_V_WIDTH + 2 * d_model, F32, "proj_gates", cast=(w_ffn_down,))

    others = []
    for g, dil in enumerate(ATT_DILATIONS[:-1]):
        others.extend(_attention(_proj_units(h_bf, w_in, g, dil, batch, seq), dil, batch, seq))
    g, dil = len(ATT_DILATIONS) - 1, ATT_DILATIONS[-1]
    o_att = _attention(_proj_units(h_bf, w_in, g, dil, batch, seq), dil, batch, seq, others=others)

    z = _retention(qk_r, v_r, gates, ret_gn_g, batch, seq)

    mixed = _mix(o_att, z, w_attn_out, w_ret_out, gates, RET_V_WIDTH)
    h1_f32, h1_bf = _proj_ln(mixed, w_o, h_f32, ln1_g, ln1_b, alpha)

    hidden = _swiglu(h1_bf, w_ffn_gate, w_ffn_up)
    base = _ple_base(h1_bf, h1_f32, p_bf, w_ple_gate, w_ple_up, alpha)
    return _down_ln(hidden, w_ffn_down, base, ln2_g, ln2_b)


def kernel(x, p, w_in, w_attn_out, w_ret_out, ret_gn_g, w_o, ln1_g, ln1_b, w_ffn_gate, w_ffn_up,
           w_ffn_down, w_ple_gate, w_ple_up, ln2_g, ln2_b):
    batch, seq, d_model = x.shape
    depth = w_in.shape[0]
    n = batch * seq
    alpha = (2 * depth) ** 0.25

    half = RET_QK_DIM // 2
    pos = jnp.arange(seq, dtype=F32)
    inv_freq = RET_ROPE_BASE ** (-jnp.arange(half, dtype=F32) / half)
    ang = pos[:, None] * inv_freq[None, :]
    cos, sin = jnp.cos(ang), jnp.sin(ang)

    h = x.reshape(n, d_model)
    for i in range(depth):
        bf = lambda w: w[i].astype(BF16)
        row = lambda v: v[i].reshape(1, -1)
        h = _layer(h, h.astype(BF16), p[i].reshape(n, -1).astype(BF16),
                   bf(w_in), w_attn_out[i], w_ret_out[i], row(ret_gn_g), w_o[i], row(ln1_g), row(ln1_b),
                   w_ffn_gate[i], w_ffn_up[i], w_ffn_down[i], w_ple_gate[i], bf(w_ple_up),
                   row(ln2_g), row(ln2_b), cos, sin, batch, seq, alpha)
    return h.reshape(batch, seq, d_model).astype(x.dtype)
```

```python
import functools

import jax
import jax.numpy as jnp
from jax import lax
from jax.experimental import pallas as pl
from jax.experimental.pallas import tpu as pltpu

ATT_HEAD_DIM = 128
ATT_HEADS = 8
ATT_DILATIONS = (1, 4, 16)
ATT_BLOCK = 128
ATT_GROUP_WIDTH = ATT_HEADS * ATT_HEAD_DIM
ATT_QKV_WIDTH = len(ATT_DILATIONS) * ATT_GROUP_WIDTH
RET_HEADS = 8
RET_QK_DIM = 256
RET_V_DIM = 512
RET_QK_WIDTH = RET_HEADS * RET_QK_DIM
RET_V_WIDTH = RET_HEADS * RET_V_DIM
RET_CHUNK = 128
RET_ROPE_BASE = 10000.0
LN_EPS = 1e-5
GN_EPS = 1e-6
NEG_INF = -1e30

V7X_VMEM_LIMIT_BYTES = 56 * 1024 * 1024
LANE = 128

BF16 = jnp.bfloat16
F32 = jnp.float32


def _params(*semantics):
    return pltpu.CompilerParams(dimension_semantics=semantics, vmem_limit_bytes=V7X_VMEM_LIMIT_BYTES)


def _tile(dim, pref):
    if dim <= pref:
        return dim
    t = (pref // LANE) * LANE
    while t > LANE and dim % t:
        t -= LANE
    assert dim % t == 0, (dim, pref)
    return t


def _sigmoid(v):
    return 1.0 / (1.0 + jnp.exp(-v))


def _mm(a, b):
    return lax.dot_general(a, b, (((1,), (0,)), ((), ())), preferred_element_type=F32)


ROW_SPLITS = 2


def _row_halves(n_rows):
    rows = n_rows // ROW_SPLITS
    return [slice(c * rows, (c + 1) * rows) for c in range(ROW_SPLITS)]


BF16_SUBLANES = 16


def _proj_kernel(x_ref, w_ref, *refs):
    n_cast = (len(refs) - 1) // 2
    o_ref = refs[n_cast]
    o_ref[...] = _mm(x_ref[...], w_ref[...]).astype(o_ref.dtype)
    for src, dst in zip(refs[:n_cast], refs[n_cast + 1:]):
        dst[...] = src[...].astype(dst.dtype)


def _proj(x_bf, w_bf, col0, width, out_dtype, name, cast=()):
    n, k = x_bf.shape
    tm, tn = _tile(n, 1024), _tile(width, 1024)
    assert col0 % tn == 0
    off = col0 // tn
    n_i = n // tm
    steps = (width // tn) * n_i
    cast_specs, cast_shapes = [], []
    for w in cast:
        rows, cols = w.shape
        slab = -(-(-(-rows // steps)) // BF16_SUBLANES) * BF16_SUBLANES
        while rows % slab:
            slab += BF16_SUBLANES
        last = rows // slab - 1
        cast_specs.append(pl.BlockSpec((slab, cols), lambda j, i, last=last: (jnp.minimum(j * n_i + i, last), 0)))
        cast_shapes.append(jax.ShapeDtypeStruct(w.shape, BF16))
    outs = pl.pallas_call(
        _proj_kernel,
        out_shape=[jax.ShapeDtypeStruct((n, width), out_dtype)] + cast_shapes,
        grid=(width // tn, n_i),
        in_specs=[pl.BlockSpec((tm, k), lambda j, i: (i, 0)),
                  pl.BlockSpec((k, tn), lambda j, i: (0, off + j))] + cast_specs,
        out_specs=[pl.BlockSpec((tm, tn), lambda j, i: (i, j))] + cast_specs,
        compiler_params=_params("arbitrary", "arbitrary"),
        name=name,
    )(x_bf, w_bf, *cast)
    return outs if cast else outs[0]


def _proj_rot_kernel(x_ref, w_ref, cos_ref, sin_ref, o_ref, *, n_q_blocks, k_scale):
    j = pl.program_id(0)
    acc = _mm(x_ref[...], w_ref[...])
    scale = jnp.where(j >= n_q_blocks, k_scale, 1.0).astype(F32)
    cos = cos_ref[...]
    sin = sin_ref[...]
    half = RET_QK_DIM // 2
    for h in range(acc.shape[1] // RET_QK_DIM):
        lo = h * RET_QK_DIM
        t1 = acc[:, lo:lo + half]
        t2 = acc[:, lo + half:lo + RET_QK_DIM]
        o_ref[:, lo:lo + half] = (t1 * cos - t2 * sin) * scale
        o_ref[:, lo + half:lo + RET_QK_DIM] = (t1 * sin + t2 * cos) * scale


def _proj_rot(x_bf, w_bf, col0, cos, sin, seq):
    n, k = x_bf.shape
    width = 2 * RET_QK_WIDTH
    tm, tn = _tile(seq, 1024), 1024
    assert col0 % tn == 0 and RET_QK_WIDTH % tn == 0 and n % seq == 0
    off = col0 // tn
    pos_blocks = seq // tm
    kern = functools.partial(_proj_rot_kernel, n_q_blocks=RET_QK_WIDTH // tn, k_scale=RET_QK_DIM ** -0.5)
    half = RET_QK_DIM // 2
    return pl.pallas_call(
        kern,
        out_shape=jax.ShapeDtypeStruct((n, width), F32),
        grid=(width // tn, n // tm),
        in_specs=[pl.BlockSpec((tm, k), lambda j, i: (i, 0)),
                  pl.BlockSpec((k, tn), lambda j, i: (0, off + j)),
                  pl.BlockSpec((tm, half), lambda j, i: (i % pos_blocks, 0)),
                  pl.BlockSpec((tm, half), lambda j, i: (i % pos_blocks, 0))],
        out_specs=pl.BlockSpec((tm, tn), lambda j, i: (i, j)),
        compiler_params=_params("parallel", "parallel"),
        name="proj_rot",
    )(x_bf, w_bf, cos, sin)


ATT_SPAN_BLOCKS = 16
ATT_HEADS_PER_STEP = 4
ATT_HEADS_PER_COMBINE_STEP = 2


def _proj_units_kernel(x_ref, w_ref, o_ref, acc_ref, *, dilation):
    if dilation == 1:
        o_ref[0, 0] = _mm(x_ref[...], w_ref[...]).astype(o_ref.dtype)
        return
    w = w_ref[...]
    slabs = w.shape[1] // LANE
    for tok in _row_halves(x_ref.shape[0]):
        acc = _mm(x_ref[tok, :], w)
        for s in range(slabs):
            acc_ref[s, tok, :] = acc[:, s * LANE:(s + 1) * LANE]
        rows = acc.shape[0] // dilation
        dst = slice(tok.start // dilation, tok.start // dilation + rows)
        for r in range(dilation):
            for s in range(slabs):
                piece = acc_ref[s, pl.ds(tok.start + r, rows, stride=dilation), :]
                o_ref[r, 0, dst, s * LANE:(s + 1) * LANE] = piece.astype(o_ref.dtype)


def _proj_units(x_bf, w_bf, group, dilation, batch, seq):
    n, k = x_bf.shape
    tm, tn = _tile(seq, 1024), ATT_GROUP_WIDTH
    assert (tm // ROW_SPLITS) % (dilation * BF16_SUBLANES) == 0 and seq % tm == 0
    rows = tm // dilation
    tiles = seq // tm
    n_groups = len(ATT_DILATIONS)
    kern = functools.partial(_proj_units_kernel, dilation=dilation)
    out = pl.pallas_call(
        kern,
        out_shape=jax.ShapeDtypeStruct((3, batch * dilation, tiles, rows, tn), BF16),
        grid=(3, n // tm),
        in_specs=[pl.BlockSpec((tm, k), lambda j, i: (i, 0)),
                  pl.BlockSpec((k, tn), lambda j, i: (0, j * n_groups + group))],
        out_specs=pl.BlockSpec((None, dilation, 1, rows, tn), lambda j, i: (j, i // tiles, i % tiles, 0, 0)),
        scratch_shapes=[pltpu.VMEM((tn // LANE, tm, LANE), F32)],
        compiler_params=_params("parallel", "parallel"),
        name=f"proj_att_d{dilation}",
    )(x_bf, w_bf)
    n_blk = seq // (dilation * ATT_BLOCK)
    return out.reshape(3, batch * dilation, n_blk, ATT_BLOCK, tn)


def _proj_units_f32_kernel(x_ref, w_ref, o_ref, xb_ref):
    xb = x_ref[...].astype(xb_ref.dtype)
    xb_ref[...] = xb
    o_ref[0, 0] = _mm(xb, w_ref[...]).astype(o_ref.dtype)


def _proj_units_from_f32(x_f32, w_bf, group, batch, seq):
    n, k = x_f32.shape
    tm, tn = _tile(seq, 512), ATT_GROUP_WIDTH
    tiles = seq // tm
    n_groups = len(ATT_DILATIONS)
    out, x_bf = pl.pallas_call(
        _proj_units_f32_kernel,
        out_shape=(jax.ShapeDtypeStruct((3, batch, tiles, tm, tn), BF16), jax.ShapeDtypeStruct((n, k), BF16)),
        grid=(n // tm, 3),
        in_specs=[pl.BlockSpec((tm, k), lambda i, j: (i, 0)),
                  pl.BlockSpec((k, tn), lambda i, j: (0, j * n_groups + group))],
        out_specs=(pl.BlockSpec((None, 1, 1, tm, tn), lambda i, j: (j, i // tiles, i % tiles, 0, 0)),
                   pl.BlockSpec((tm, k), lambda i, j: (i, 0))),
        compiler_params=_params("arbitrary", "arbitrary"),
        name="proj_att_d1",
    )(x_f32, w_bf)
    return out.reshape(3, batch, seq // ATT_BLOCK, ATT_BLOCK, tn), x_bf


def _attn_span(q_ref, k_ref, kp_ref, v_ref, vp_ref, o_ref, lse_ref, *, dilation, nb):
    qi = lax.broadcasted_iota(jnp.int32, (ATT_BLOCK, 2 * ATT_BLOCK), 0)
    kj = lax.broadcasted_iota(jnp.int32, (ATT_BLOCK, 2 * ATT_BLOCK), 1)
    in_window = kj <= qi + ATT_BLOCK
    valid_inner = jnp.logical_and(kj >= qi, in_window)
    lo = jnp.where(pl.program_id(1) > 0, 0, ATT_BLOCK)
    valid_first = jnp.logical_and(kj >= jnp.maximum(qi, lo), in_window)
    scale = ATT_HEAD_DIM ** -0.5
    dn = (((1,), (1,)), ((), ()))
    ones = jnp.ones((2 * ATT_BLOCK, ATT_HEAD_DIM), BF16)
    units = [(r, i, h) for r in range(dilation) for i in range(nb) for h in range(o_ref.shape[0])]

    def cols(h):
        return slice(h * ATT_HEAD_DIM, (h + 1) * ATT_HEAD_DIM)

    def with_prev(cur_ref, prev_ref, r, i, h):
        prev = prev_ref[r, 0, :, cols(h)] if i == 0 else cur_ref[r, i - 1, :, cols(h)]
        return jnp.concatenate([prev, cur_ref[r, i, :, cols(h)]], axis=0)

    scores = [lax.dot_general(q_ref[r, i, :, cols(h)], with_prev(k_ref, kp_ref, r, i, h), dn,
                              preferred_element_type=F32) * scale for r, i, h in units]
    probs, maxes = [], []
    for (r, i, h), s in zip(units, scores):
        s = jnp.where(valid_first if i == 0 else valid_inner, s, NEG_INF)
        m = jnp.max(s, axis=-1, keepdims=True)
        probs.append(jnp.exp(s - m).astype(BF16))
        maxes.append(m)
    for (r, i, h), p, m in zip(units, probs, maxes):
        v_aug = jnp.concatenate([with_prev(v_ref, vp_ref, r, i, h), ones], axis=1)
        acc = _mm(p, v_aug)
        den = acc[:, ATT_HEAD_DIM:]
        if dilation == 1:
            rows = pl.ds(i * ATT_BLOCK, ATT_BLOCK)
        else:
            rows = pl.ds(i * ATT_BLOCK * dilation + r, ATT_BLOCK, stride=dilation)
        o_ref[h, rows, :] = acc[:, :ATT_HEAD_DIM] / den
        lse_ref[h, rows, :] = m + jnp.log(den)


def _attn_kernel(q_ref, k_ref, kp_ref, v_ref, vp_ref, o_ref, lse_ref, *, dilation, nb):
    _attn_span(q_ref, k_ref, kp_ref, v_ref, vp_ref, o_ref, lse_ref, dilation=dilation, nb=nb)


ATT_COMBINE_ROWS = 256


def _attn_combine_kernel(q_ref, k_ref, kp_ref, v_ref, vp_ref, o1, l1, o2, l2, out_ref, o3, l3, *, dilation, nb):
    _attn_span(q_ref, k_ref, kp_ref, v_ref, vp_ref, o3, l3, dilation=dilation, nb=nb)

    def body(c, carry):
        rows = pl.ds(pl.multiple_of(c * ATT_COMBINE_ROWS, ATT_COMBINE_ROWS), ATT_COMBINE_ROWS)
        for h in range(o3.shape[0]):
            a, b, d = l1[h, rows, :], l2[h, rows, :], l3[h, rows, :]
            m = jnp.maximum(jnp.maximum(a, b), d)
            ea, eb, ed = jnp.exp(a - m), jnp.exp(b - m), jnp.exp(d - m)
            s = ea + eb + ed
            out = (ea / s) * o1[h, rows, :] + (eb / s) * o2[h, rows, :] + (ed / s) * o3[h, rows, :]
            out_ref[rows, h * ATT_HEAD_DIM:(h + 1) * ATT_HEAD_DIM] = out.astype(out_ref.dtype)
        return carry

    lax.fori_loop(0, out_ref.shape[0] // ATT_COMBINE_ROWS, body, 0)


def _attention(qkv, dilation, batch, seq, others=None):
    n_blk = qkv.shape[2]
    assert ATT_SPAN_BLOCKS % dilation == 0
    nb = ATT_SPAN_BLOCKS // dilation
    assert n_blk % nb == 0
    spans = n_blk // nb
    span_tokens = ATT_SPAN_BLOCKS * ATT_BLOCK
    heads = ATT_HEADS_PER_STEP if others is None else ATT_HEADS_PER_COMBINE_STEP
    hw = heads * ATT_HEAD_DIM
    cur = (None, dilation, nb, ATT_BLOCK, hw)
    one = (None, dilation, 1, ATT_BLOCK, hw)

    def at(part):
        return lambda b, s, hp: (part, b, s, 0, hp)

    def before(part):
        return lambda b, s, hp: (part, b, jnp.maximum(s * nb - 1, 0), 0, hp)

    qkv_specs = [pl.BlockSpec(cur, at(0)), pl.BlockSpec(cur, at(1)), pl.BlockSpec(one, before(1)),
                 pl.BlockSpec(cur, at(2)), pl.BlockSpec(one, before(2))]
    per_head = jax.ShapeDtypeStruct((ATT_HEADS, batch * seq, ATT_HEAD_DIM), F32)
    head_blk = (heads, span_tokens, ATT_HEAD_DIM)
    head_spec = pl.BlockSpec(head_blk, lambda b, s, hp: (hp, b * spans + s, 0))
    grid = (batch, spans, ATT_HEADS // heads)
    if others is None:
        return pl.pallas_call(
            functools.partial(_attn_kernel, dilation=dilation, nb=nb),
            out_shape=(per_head, per_head),
            grid=grid,
            in_specs=qkv_specs,
            out_specs=(head_spec, head_spec),
            compiler_params=_params("parallel", "parallel", "parallel"),
            name=f"attn_d{dilation}",
        )(qkv, qkv, qkv, qkv, qkv)
    return pl.pallas_call(
        functools.partial(_attn_combine_kernel, dilation=dilation, nb=nb),
        out_shape=jax.ShapeDtypeStruct((batch * seq, ATT_GROUP_WIDTH), BF16),
        grid=grid,
        in_specs=qkv_specs + [head_spec] * 4,
        out_specs=pl.BlockSpec((span_tokens, hw), lambda b, s, hp: (b * spans + s, hp)),
        scratch_shapes=[pltpu.VMEM(head_blk, F32), pltpu.VMEM(head_blk, F32)],
        compiler_params=_params("parallel", "parallel", "parallel"),
        name=f"attn_d{dilation}_combine",
    )(qkv, qkv, qkv, qkv, qkv, *others)


RET_CHUNKS_PER_STEP = 16


def _ret_kernel(q_ref, k_ref, v_ref, g_ref, gn_ref, intra_ref, cross_ref, sdec_ref, cdec_ref,
                z_ref, state_ref, *, cb):
    @pl.when(pl.program_id(2) == 0)
    def _():
        state_ref[...] = jnp.zeros_like(state_ref)

    c = RET_CHUNK
    intra, cross_decay, state_decay, chunk_decay = intra_ref[...], cross_ref[...], sdec_ref[...], cdec_ref[...]
    gn = gn_ref[...]
    qs, inners, updates = [], [], []
    for t in range(cb):
        rows = slice(t * c, (t + 1) * c)
        q = q_ref[rows, :].astype(BF16)
        k = k_ref[rows, :]
        v = v_ref[rows, :]
        att = lax.dot_general(q, k.astype(BF16), (((1,), (1,)), ((), ())), preferred_element_type=F32) * intra
        inners.append(_mm(att.astype(BF16), v))
        kd = (k * state_decay).astype(BF16)
        updates.append(lax.dot_general(kd, v, (((0,), (0,)), ((), ())), preferred_element_type=F32))
        qs.append(q)
    state = state_ref[...]
    for t in range(cb):
        rows = slice(t * c, (t + 1) * c)
        r = inners[t] + _mm(qs[t], state.astype(BF16)) * cross_decay
        state = chunk_decay * state + updates[t]
        mu = jnp.mean(r, axis=-1, keepdims=True)
        var = jnp.mean(jnp.square(r - mu), axis=-1, keepdims=True)
        rn = (r - mu) * lax.rsqrt(var + GN_EPS) * gn
        g = g_ref[rows, :]
        z_ref[rows, :] = ((g * _sigmoid(g)) * rn).astype(z_ref.dtype)
    state_ref[...] = state


def _retention(qk, v, gates, gn_g, batch, seq):
    n = qk.shape[0]
    c = RET_CHUNK
    nc = seq // c
    cb = min(RET_CHUNKS_PER_STEP, nc)
    assert nc % cb == 0
    steps = nc // cb
    hh = jnp.arange(RET_HEADS, dtype=F32)
    log_gamma = jnp.log(1.0 - 2.0 ** (-5.0 - hh))
    idx = jnp.arange(c, dtype=F32)
    diff = idx[:, None] - idx[None, :]
    intra = jnp.where(diff >= 0, jnp.exp(jnp.maximum(diff, 0.0)[None] * log_gamma[:, None, None]), 0.0)
    cross_decay = jnp.exp((idx + 1.0)[None, :] * log_gamma[:, None])[..., None]
    state_decay = jnp.exp((c - 1.0 - idx)[None, :] * log_gamma[:, None])[..., None]
    chunk_decay = jnp.exp(c * log_gamma)[:, None, None]

    def rows(b, h, s):
        return b * steps + s

    kern = functools.partial(_ret_kernel, cb=cb)
    return pl.pallas_call(
        kern,
        out_shape=jax.ShapeDtypeStruct((n, RET_V_WIDTH), BF16),
        grid=(batch, RET_HEADS, steps),
        in_specs=[
            pl.BlockSpec((cb * c, RET_QK_DIM), lambda b, h, s: (rows(b, h, s), h)),
            pl.BlockSpec((cb * c, RET_QK_DIM), lambda b, h, s: (rows(b, h, s), RET_HEADS + h)),
            pl.BlockSpec((cb * c, RET_V_DIM), lambda b, h, s: (rows(b, h, s), h)),
            pl.BlockSpec((cb * c, RET_V_DIM), lambda b, h, s: (rows(b, h, s), h)),
            pl.BlockSpec((1, RET_V_DIM), lambda b, h, s: (0, h)),
            pl.BlockSpec((None, c, c), lambda b, h, s: (h, 0, 0)),
            pl.BlockSpec((None, c, 1), lambda b, h, s: (h, 0, 0)),
            pl.BlockSpec((None, c, 1), lambda b, h, s: (h, 0, 0)),
            pl.BlockSpec((None, 1, 1), lambda b, h, s: (h, 0, 0)),
        ],
        out_specs=pl.BlockSpec((cb * c, RET_V_DIM), lambda b, h, s: (rows(b, h, s), h)),
        scratch_shapes=[pltpu.VMEM((RET_QK_DIM, RET_V_DIM), F32)],
        compiler_params=_params("parallel", "parallel", "arbitrary"),
        name="retention",
    )(qk, qk, v, gates, gn_g, intra, cross_decay, state_decay, chunk_decay)


def _mix_kernel(oa_ref, z_ref, wa_ref, wr_ref, ga_ref, gr_ref, o_ref):
    ya = _mm(oa_ref[...], wa_ref[...])
    yr = _mm(z_ref[...], wr_ref[...])
    o_ref[...] = (_sigmoid(ga_ref[...]) * ya + _sigmoid(gr_ref[...]) * yr).astype(o_ref.dtype)


def _mix(o_att, z, wa, wr, gates, gate_col0):
    n = o_att.shape[0]
    d = wa.shape[1]
    tm, tn = _tile(n, 512), _tile(d, 1024)
    assert gate_col0 % tn == 0
    ga_off = gate_col0 // tn
    gr_off = ga_off + d // tn
    return pl.pallas_call(
        _mix_kernel,
        out_shape=jax.ShapeDtypeStruct((n, d), BF16),
        grid=(d // tn, n // tm),
        in_specs=[pl.BlockSpec((tm, o_att.shape[1]), lambda j, i: (i, 0)),
                  pl.BlockSpec((tm, z.shape[1]), lambda j, i: (i, 0)),
                  pl.BlockSpec((wa.shape[0], tn), lambda j, i: (0, j)),
                  pl.BlockSpec((wr.shape[0], tn), lambda j, i: (0, j)),
                  pl.BlockSpec((tm, tn), lambda j, i: (i, ga_off + j)),
                  pl.BlockSpec((tm, tn), lambda j, i: (i, gr_off + j))],
        out_specs=pl.BlockSpec((tm, tn), lambda j, i: (i, j)),
        compiler_params=_params("parallel", "parallel"),
        name="mix",
    )(o_att, z, wa, wr, gates, gates)


def _layer_norm_rows(t, g, b):
    mu = jnp.mean(t, axis=-1, keepdims=True)
    var = jnp.mean(jnp.square(t - mu), axis=-1, keepdims=True)
    return (t - mu) * lax.rsqrt(var + LN_EPS) * g + b


LN_ROW_CHUNK = 128


def _layer_norm_inplace(o_ref, g_ref, b_ref, cast_ref=None):
    g, b = g_ref[...], b_ref[...]
    chunk = min(LN_ROW_CHUNK, o_ref.shape[0])

    def body(r, carry):
        rows = pl.ds(pl.multiple_of(r * chunk, chunk), chunk)
        y = _layer_norm_rows(o_ref[rows, :], g, b)
        o_ref[rows, :] = y
        if cast_ref is not None:
            cast_ref[rows, :] = y.astype(cast_ref.dtype)
        return carry

    lax.fori_loop(0, o_ref.shape[0] // chunk, body, 0)


def _proj_ln_kernel(a_ref, w_ref, res_ref, g_ref, b_ref, of_ref, ob_ref, *, alpha):
    y = _layer_norm_rows(alpha * res_ref[...] + _mm(a_ref[...], w_ref[...]), g_ref[...], b_ref[...])
    of_ref[...] = y
    ob_ref[...] = y.astype(ob_ref.dtype)


def _proj_ln(a, w, res, g, b, alpha):
    n, k = a.shape
    d = w.shape[1]
    tm = _tile(n, 128)
    kern = functools.partial(_proj_ln_kernel, alpha=alpha)
    row = pl.BlockSpec((tm, d), lambda i: (i, 0))
    vec = pl.BlockSpec((1, d), lambda i: (0, 0))
    return pl.pallas_call(
        kern,
        out_shape=(jax.ShapeDtypeStruct((n, d), F32), jax.ShapeDtypeStruct((n, d), BF16)),
        grid=(n // tm,),
        in_specs=[pl.BlockSpec((tm, k), lambda i: (i, 0)),
                  pl.BlockSpec((k, d), lambda i: (0, 0), pipeline_mode=pl.Buffered(1)),
                  row, vec, vec],
        out_specs=(row, row),
        compiler_params=_params("parallel"),
        name="proj_ln",
    )(a, w, res, g, b)


def _swiglu_kernel(h_ref, wg_ref, wu_ref, o_ref):
    wg, wu = wg_ref[...], wu_ref[...]
    for rows in _row_halves(o_ref.shape[0]):
        h = h_ref[rows, :]
        gate = _mm(h, wg)
        up = _mm(h, wu)
        o_ref[rows, :] = ((gate * _sigmoid(gate)) * up).astype(o_ref.dtype)


def _swiglu(h_bf, wg, wu):
    n, k = h_bf.shape
    f = wg.shape[1]
    tm, tn = _tile(n, 2048), _tile(f, 256)
    w_spec = pl.BlockSpec((k, tn), lambda i, j: (0, j))
    return pl.pallas_call(
        _swiglu_kernel,
        out_shape=jax.ShapeDtypeStruct((n, f), BF16),
        grid=(n // tm, f // tn),
        in_specs=[pl.BlockSpec((tm, k), lambda i, j: (i, 0)), w_spec, w_spec],
        out_specs=pl.BlockSpec((tm, tn), lambda i, j: (i, j)),
        compiler_params=_params("parallel", "parallel"),
        name="swiglu",
    )(h_bf, wg, wu)


def _ple_kernel(hb_ref, wg_ref, p_ref, wu_ref, hf_ref, o_ref, *, alpha):
    gate = _sigmoid(_mm(hb_ref[...], wg_ref[...]))
    up = _mm(p_ref[...], wu_ref[...])
    o_ref[...] = alpha * hf_ref[...] + gate * up


def _ple_base(h_bf, h_f32, p_bf, wpg, wpu, alpha):
    n, k = h_bf.shape
    d = wpg.shape[1]
    tm, tn = _tile(n, 512), _tile(d, 1024)
    kern = functools.partial(_ple_kernel, alpha=alpha)
    return pl.pallas_call(
        kern,
        out_shape=jax.ShapeDtypeStruct((n, d), F32),
        grid=(d // tn, n // tm),
        in_specs=[pl.BlockSpec((tm, k), lambda j, i: (i, 0)),
                  pl.BlockSpec((k, tn), lambda j, i: (0, j)),
                  pl.BlockSpec((tm, p_bf.shape[1]), lambda j, i: (i, 0)),
                  pl.BlockSpec((p_bf.shape[1], tn), lambda j, i: (0, j)),
                  pl.BlockSpec((tm, tn), lambda j, i: (i, j))],
        out_specs=pl.BlockSpec((tm, tn), lambda j, i: (i, j)),
        compiler_params=_params("parallel", "parallel"),
        name="ple_base",
    )(h_bf, wpg, p_bf, wpu, h_f32)


def _down_ln_kernel(a_ref, w_ref, base_ref, g_ref, b_ref, o_ref, *, tn):
    j = pl.program_id(1)
    col = pl.multiple_of(j * tn, tn)
    o_ref[:, pl.ds(col, tn)] = base_ref[...] + _mm(a_ref[...], w_ref[...])

    @pl.when(j == pl.num_programs(1) - 1)
    def _():
        _layer_norm_inplace(o_ref, g_ref, b_ref)


def _down_ln(hidden, wd, base, g, b):
    n, f = hidden.shape
    d = wd.shape[1]
    tm, tn = _tile(n, 512), _tile(d, 256)
    kern = functools.partial(_down_ln_kernel, tn=tn)
    vec = pl.BlockSpec((1, d), lambda i, j: (0, 0))
    return pl.pallas_call(
        kern,
        out_shape=jax.ShapeDtypeStruct((n, d), F32),
        grid=(n // tm, d // tn),
        in_specs=[pl.BlockSpec((tm, f), lambda i, j: (i, 0)),
                  pl.BlockSpec((f, tn), lambda i, j: (0, j)),
                  pl.BlockSpec((tm, tn), lambda i, j: (i, j)),
                  vec, vec],
        out_specs=pl.BlockSpec((tm, d), lambda i, j: (i, 0)),
        compiler_params=_params("parallel", "arbitrary"),
        name="down_ln",
    )(hidden, wd, base, g, b)


def _layer(h_f32, p_bf, w_in, w_attn_out, w_ret_out, ret_gn_g, w_o, ln1_g, ln1_b,
           w_ffn_gate, w_ffn_up, w_ffn_down, w_ple_gate, w_ple_up, ln2_g, ln2_b,
           cos, sin, batch, seq, alpha):
    d_model = h_f32.shape[1]
    c_qr = 3 * ATT_QKV_WIDTH
    c_vr = c_qr + 2 * RET_QK_WIDTH
    c_gr = c_vr + RET_V_WIDTH
    assert w_in.shape[1] == c_gr + RET_V_WIDTH + 2 * d_model
    assert ATT_DILATIONS[0] == 1

    qkv_first, h_bf = _proj_units_from_f32(h_f32, w_in, 0, batch, seq)
    qk_r = _proj_rot(h_bf, w_in, c_qr, cos, sin, seq)
    v_r, w_attn_out, w_ret_out, w_o, w_ple_gate = _proj(
        h_bf, w_in, c_vr, RET_V_WIDTH, BF16, "proj_vr", cast=(w_attn_out, w_ret_out, w_o, w_ple_gate))
    gates, w_ffn_down = _proj(
        h_bf, w_in, c_gr, RET_V_WIDTH + 2 * d_model, F32, "proj_gates", cast=(w_ffn_down,))

    others = list(_attention(qkv_first, 1, batch, seq))
    for g, dil in list(enumerate(ATT_DILATIONS))[1:-1]:
        others.extend(_attention(_proj_units(h_bf, w_in, g, dil, batch, seq), dil, batch, seq))
    g, dil = len(ATT_DILATIONS) - 1, ATT_DILATIONS[-1]
    o_att = _attention(_proj_units(h_bf, w_in, g, dil, batch, seq), dil, batch, seq, others=others)

    z = _retention(qk_r, v_r, gates, ret_gn_g, batch, seq)

    mixed = _mix(o_att, z, w_attn_out, w_ret_out, gates, RET_V_WIDTH)
    h1_f32, h1_bf = _proj_ln(mixed, w_o, h_f32, ln1_g, ln1_b, alpha)

    hidden = _swiglu(h1_bf, w_ffn_gate, w_ffn_up)
    base = _ple_base(h1_bf, h1_f32, p_bf, w_ple_gate, w_ple_up, alpha)
    return _down_ln(hidden, w_ffn_down, base, ln2_g, ln2_b)


def kernel(x, p, w_in, w_attn_out, w_ret_out, ret_gn_g, w_o, ln1_g, ln1_b, w_ffn_gate, w_ffn_up,
           w_ffn_down, w_ple_gate, w_ple_up, ln2_g, ln2_b):
    batch, seq, d_model = x.shape
    depth = w_in.shape[0]
    n = batch * seq
    alpha = (2 * depth) ** 0.25

    half = RET_QK_DIM // 2
    pos = jnp.arange(seq, dtype=F32)
    inv_freq = RET_ROPE_BASE ** (-jnp.arange(half, dtype=F32) / half)
    ang = pos[:, None] * inv_freq[None, :]
    cos, sin = jnp.cos(ang), jnp.sin(ang)

    h = x.reshape(n, d_model)
    for i in range(depth):
        bf = lambda w: w[i].astype(BF16)
        row = lambda v: v[i].reshape(1, -1)
        h = _layer(h, p[i].reshape(n, -1).astype(BF16),
                   bf(w_in), w_attn_out[i], w_ret_out[i], row(ret_gn_g), w_o[i], row(ln1_g), row(ln1_b),
                   w_ffn_gate[i], w_ffn_up[i], w_ffn_down[i], w_ple_gate[i], bf(w_ple_up),
                   row(ln2_g), row(ln2_b), cos, sin, batch, seq, alpha)
    return h.reshape(batch, seq, d_model).astype(x.dtype)
```

```python
import functools

import jax
import jax.numpy as jnp
from jax import lax
from jax.experimental import pallas as pl
from jax.experimental.pallas import tpu as pltpu

ATT_HEAD_DIM = 128
ATT_HEADS = 8
ATT_DILATIONS = (1, 4, 16)
ATT_BLOCK = 128
ATT_GROUP_WIDTH = ATT_HEADS * ATT_HEAD_DIM
ATT_QKV_WIDTH = len(ATT_DILATIONS) * ATT_GROUP_WIDTH
RET_HEADS = 8
RET_QK_DIM = 256
RET_V_DIM = 512
RET_QK_WIDTH = RET_HEADS * RET_QK_DIM
RET_V_WIDTH = RET_HEADS * RET_V_DIM
RET_CHUNK = 128
RET_ROPE_BASE = 10000.0
LN_EPS = 1e-5
GN_EPS = 1e-6
NEG_INF = -1e30

V7X_VMEM_LIMIT_BYTES = 56 * 1024 * 1024
LANE = 128

BF16 = jnp.bfloat16
F32 = jnp.float32


def _params(*semantics):
    return pltpu.CompilerParams(dimension_semantics=semantics, vmem_limit_bytes=V7X_VMEM_LIMIT_BYTES)


def _tile(dim, pref):
    if dim <= pref:
        return dim
    t = (pref // LANE) * LANE
    while t > LANE and dim % t:
        t -= LANE
    assert dim % t == 0, (dim, pref)
    return t


def _sigmoid(v):
    return 1.0 / (1.0 + jnp.exp(-v))


def _mm(a, b):
    return lax.dot_general(a, b, (((1,), (0,)), ((), ())), preferred_element_type=F32)


ROW_SPLITS = 2


def _row_halves(n_rows):
    rows = n_rows // ROW_SPLITS
    return [slice(c * rows, (c + 1) * rows) for c in range(ROW_SPLITS)]


BF16_SUBLANES = 16


def _proj_kernel(x_ref, w_ref, *refs):
    n_cast = (len(refs) - 1) // 2
    o_ref = refs[n_cast]
    o_ref[...] = _mm(x_ref[...], w_ref[...]).astype(o_ref.dtype)
    for src, dst in zip(refs[:n_cast], refs[n_cast + 1:]):
        dst[...] = src[...].astype(dst.dtype)


def _proj(x_bf, w_bf, col0, width, out_dtype, name, cast=()):
    n, k = x_bf.shape
    tm, tn = _tile(n, 1024), _tile(width, 1024)
    assert col0 % tn == 0
    off = col0 // tn
    n_i = n // tm
    steps = (width // tn) * n_i
    cast_specs, cast_shapes = [], []
    for w in cast:
        rows, cols = w.shape
        slab = -(-(-(-rows // steps)) // BF16_SUBLANES) * BF16_SUBLANES
        while rows % slab:
            slab += BF16_SUBLANES
        last = rows // slab - 1
        cast_specs.append(pl.BlockSpec((slab, cols), lambda j, i, last=last: (jnp.minimum(j * n_i + i, last), 0)))
        cast_shapes.append(jax.ShapeDtypeStruct(w.shape, BF16))
    outs = pl.pallas_call(
        _proj_kernel,
        out_shape=[jax.ShapeDtypeStruct((n, width), out_dtype)] + cast_shapes,
        grid=(width // tn, n_i),
        in_specs=[pl.BlockSpec((tm, k), lambda j, i: (i, 0)),
                  pl.BlockSpec((k, tn), lambda j, i: (0, off + j))] + cast_specs,
        out_specs=[pl.BlockSpec((tm, tn), lambda j, i: (i, j))] + cast_specs,
        compiler_params=_params("arbitrary", "arbitrary"),
        name=name,
    )(x_bf, w_bf, *cast)
    return outs if cast else outs[0]


def _proj_rot_kernel(x_ref, w_ref, cos_ref, sin_ref, o_ref, *, n_q_blocks, k_scale):
    j = pl.program_id(0)
    acc = _mm(x_ref[...], w_ref[...])
    scale = jnp.where(j >= n_q_blocks, k_scale, 1.0).astype(F32)
    cos = cos_ref[...]
    sin = sin_ref[...]
    half = RET_QK_DIM // 2
    for h in range(acc.shape[1] // RET_QK_DIM):
        lo = h * RET_QK_DIM
        t1 = acc[:, lo:lo + half]
        t2 = acc[:, lo + half:lo + RET_QK_DIM]
        o_ref[:, lo:lo + half] = (t1 * cos - t2 * sin) * scale
        o_ref[:, lo + half:lo + RET_QK_DIM] = (t1 * sin + t2 * cos) * scale


def _proj_rot(x_bf, w_bf, col0, cos, sin, seq):
    n, k = x_bf.shape
    width = 2 * RET_QK_WIDTH
    tm, tn = _tile(seq, 1024), 1024
    assert col0 % tn == 0 and RET_QK_WIDTH % tn == 0 and n % seq == 0
    off = col0 // tn
    pos_blocks = seq // tm
    kern = functools.partial(_proj_rot_kernel, n_q_blocks=RET_QK_WIDTH // tn, k_scale=RET_QK_DIM ** -0.5)
    half = RET_QK_DIM // 2
    return pl.pallas_call(
        kern,
        out_shape=jax.ShapeDtypeStruct((n, width), F32),
        grid=(width // tn, n // tm),
        in_specs=[pl.BlockSpec((tm, k), lambda j, i: (i, 0)),
                  pl.BlockSpec((k, tn), lambda j, i: (0, off + j)),
                  pl.BlockSpec((tm, half), lambda j, i: (i % pos_blocks, 0)),
                  pl.BlockSpec((tm, half), lambda j, i: (i % pos_blocks, 0))],
        out_specs=pl.BlockSpec((tm, tn), lambda j, i: (i, j)),
        compiler_params=_params("parallel", "parallel"),
        name="proj_rot",
    )(x_bf, w_bf, cos, sin)


ATT_SPAN_BLOCKS = 16
ATT_HEADS_PER_STEP = 4
ATT_HEADS_PER_COMBINE_STEP = 2


def _proj_units_kernel(x_ref, w_ref, o_ref, acc_ref, *, dilation):
    if dilation == 1:
        o_ref[0, 0] = _mm(x_ref[...], w_ref[...]).astype(o_ref.dtype)
        return
    w = w_ref[...]
    slabs = w.shape[1] // LANE
    for tok in _row_halves(x_ref.shape[0]):
        acc = _mm(x_ref[tok, :], w)
        for s in range(slabs):
            acc_ref[s, tok, :] = acc[:, s * LANE:(s + 1) * LANE]
        rows = acc.shape[0] // dilation
        dst = slice(tok.start // dilation, tok.start // dilation + rows)
        for r in range(dilation):
            for s in range(slabs):
                piece = acc_ref[s, pl.ds(tok.start + r, rows, stride=dilation), :]
                o_ref[r, 0, dst, s * LANE:(s + 1) * LANE] = piece.astype(o_ref.dtype)


def _proj_units(x_bf, w_bf, group, dilation, batch, seq):
    n, k = x_bf.shape
    tm, tn = _tile(seq, 1024), ATT_GROUP_WIDTH
    assert (tm // ROW_SPLITS) % (dilation * BF16_SUBLANES) == 0 and seq % tm == 0
    rows = tm // dilation
    tiles = seq // tm
    n_groups = len(ATT_DILATIONS)
    kern = functools.partial(_proj_units_kernel, dilation=dilation)
    out = pl.pallas_call(
        kern,
        out_shape=jax.ShapeDtypeStruct((3, batch * dilation, tiles, rows, tn), BF16),
        grid=(3, n // tm),
        in_specs=[pl.BlockSpec((tm, k), lambda j, i: (i, 0)),
                  pl.BlockSpec((k, tn), lambda j, i: (0, j * n_groups + group))],
        out_specs=pl.BlockSpec((None, dilation, 1, rows, tn), lambda j, i: (j, i // tiles, i % tiles, 0, 0)),
        scratch_shapes=[pltpu.VMEM((tn // LANE, tm, LANE), F32)],
        compiler_params=_params("parallel", "parallel"),
        name=f"proj_att_d{dilation}",
    )(x_bf, w_bf)
    n_blk = seq // (dilation * ATT_BLOCK)
    return out.reshape(3, batch * dilation, n_blk, ATT_BLOCK, tn)


def _proj_units_f32_kernel(x_ref, wq_ref, wk_ref, wv_ref, o_ref, xb_ref):
    xb = x_ref[...].astype(xb_ref.dtype)
    xb_ref[...] = xb
    for part, w_ref in enumerate((wq_ref, wk_ref, wv_ref)):
        o_ref[part, 0, 0] = _mm(xb, w_ref[...]).astype(o_ref.dtype)


def _proj_units_from_f32(x_f32, w_bf, group, batch, seq):
    n, k = x_f32.shape
    tm, tn = _tile(seq, 256), ATT_GROUP_WIDTH
    tiles = seq // tm
    n_groups = len(ATT_DILATIONS)

    def w_part(part):
        return pl.BlockSpec((k, tn), lambda i: (0, part * n_groups + group), pipeline_mode=pl.Buffered(1))

    out, x_bf = pl.pallas_call(
        _proj_units_f32_kernel,
        out_shape=(jax.ShapeDtypeStruct((3, batch, tiles, tm, tn), BF16), jax.ShapeDtypeStruct((n, k), BF16)),
        grid=(n // tm,),
        in_specs=[pl.BlockSpec((tm, k), lambda i: (i, 0)), w_part(0), w_part(1), w_part(2)],
        out_specs=(pl.BlockSpec((3, 1, 1, tm, tn), lambda i: (0, i // tiles, i % tiles, 0, 0)),
                   pl.BlockSpec((tm, k), lambda i: (i, 0))),
        compiler_params=_params("parallel"),
        name="proj_att_d1",
    )(x_f32, w_bf, w_bf, w_bf)
    return out.reshape(3, batch, seq // ATT_BLOCK, ATT_BLOCK, tn), x_bf


def _attn_span(q_ref, k_ref, kp_ref, v_ref, vp_ref, o_ref, lse_ref, *, dilation, nb):
    qi = lax.broadcasted_iota(jnp.int32, (ATT_BLOCK, 2 * ATT_BLOCK), 0)
    kj = lax.broadcasted_iota(jnp.int32, (ATT_BLOCK, 2 * ATT_BLOCK), 1)
    in_window = kj <= qi + ATT_BLOCK
    valid_inner = jnp.logical_and(kj >= qi, in_window)
    lo = jnp.where(pl.program_id(1) > 0, 0, ATT_BLOCK)
    valid_first = jnp.logical_and(kj >= jnp.maximum(qi, lo), in_window)
    scale = ATT_HEAD_DIM ** -0.5
    dn = (((1,), (1,)), ((), ()))
    ones = jnp.ones((2 * ATT_BLOCK, ATT_HEAD_DIM), BF16)
    units = [(r, i, h) for r in range(dilation) for i in range(nb) for h in range(o_ref.shape[0])]

    def cols(h):
        return slice(h * ATT_HEAD_DIM, (h + 1) * ATT_HEAD_DIM)

    def with_prev(cur_ref, prev_ref, r, i, h):
        prev = prev_ref[r, 0, :, cols(h)] if i == 0 else cur_ref[r, i - 1, :, cols(h)]
        return jnp.concatenate([prev, cur_ref[r, i, :, cols(h)]], axis=0)

    scores = [lax.dot_general(q_ref[r, i, :, cols(h)], with_prev(k_ref, kp_ref, r, i, h), dn,
                              preferred_element_type=F32) * scale for r, i, h in units]
    probs, maxes = [], []
    for (r, i, h), s in zip(units, scores):
        s = jnp.where(valid_first if i == 0 else valid_inner, s, NEG_INF)
        m = jnp.max(s, axis=-1, keepdims=True)
        probs.append(jnp.exp(s - m).astype(BF16))
        maxes.append(m)
    for (r, i, h), p, m in zip(units, probs, maxes):
        v_aug = jnp.concatenate([with_prev(v_ref, vp_ref, r, i, h), ones], axis=1)
        acc = _mm(p, v_aug)
        den = acc[:, ATT_HEAD_DIM:]
        if dilation == 1:
            rows = pl.ds(i * ATT_BLOCK, ATT_BLOCK)
        else:
            rows = pl.ds(i * ATT_BLOCK * dilation + r, ATT_BLOCK, stride=dilation)
        o_ref[h, rows, :] = acc[:, :ATT_HEAD_DIM] / den
        lse_ref[h, rows, :] = m + jnp.log(den)


def _attn_kernel(q_ref, k_ref, kp_ref, v_ref, vp_ref, o_ref, lse_ref, *, dilation, nb):
    _attn_span(q_ref, k_ref, kp_ref, v_ref, vp_ref, o_ref, lse_ref, dilation=dilation, nb=nb)


ATT_COMBINE_ROWS = 256


def _attn_combine_kernel(q_ref, k_ref, kp_ref, v_ref, vp_ref, o1, l1, o2, l2, out_ref, o3, l3, *, dilation, nb):
    _attn_span(q_ref, k_ref, kp_ref, v_ref, vp_ref, o3, l3, dilation=dilation, nb=nb)

    def body(c, carry):
        rows = pl.ds(pl.multiple_of(c * ATT_COMBINE_ROWS, ATT_COMBINE_ROWS), ATT_COMBINE_ROWS)
        for h in range(o3.shape[0]):
            a, b, d = l1[h, rows, :], l2[h, rows, :], l3[h, rows, :]
            m = jnp.maximum(jnp.maximum(a, b), d)
            ea, eb, ed = jnp.exp(a - m), jnp.exp(b - m), jnp.exp(d - m)
            s = ea + eb + ed
            out = (ea / s) * o1[h, rows, :] + (eb / s) * o2[h, rows, :] + (ed / s) * o3[h, rows, :]
            out_ref[rows, h * ATT_HEAD_DIM:(h + 1) * ATT_HEAD_DIM] = out.astype(out_ref.dtype)
        return carry

    lax.fori_loop(0, out_ref.shape[0] // ATT_COMBINE_ROWS, body, 0)


def _attention(qkv, dilation, batch, seq, others=None):
    n_blk = qkv.shape[2]
    assert ATT_SPAN_BLOCKS % dilation == 0
    nb = ATT_SPAN_BLOCKS // dilation
    assert n_blk % nb == 0
    spans = n_blk // nb
    span_tokens = ATT_SPAN_BLOCKS * ATT_BLOCK
    heads = ATT_HEADS_PER_STEP if others is None else ATT_HEADS_PER_COMBINE_STEP
    hw = heads * ATT_HEAD_DIM
    cur = (None, dilation, nb, ATT_BLOCK, hw)
    one = (None, dilation, 1, ATT_BLOCK, hw)

    def at(part):
        return lambda b, s, hp: (part, b, s, 0, hp)

    def before(part):
        return lambda b, s, hp: (part, b, jnp.maximum(s * nb - 1, 0), 0, hp)

    qkv_specs = [pl.BlockSpec(cur, at(0)), pl.BlockSpec(cur, at(1)), pl.BlockSpec(one, before(1)),
                 pl.BlockSpec(cur, at(2)), pl.BlockSpec(one, before(2))]
    per_head = jax.ShapeDtypeStruct((ATT_HEADS, batch * seq, ATT_HEAD_DIM), F32)
    head_blk = (heads, span_tokens, ATT_HEAD_DIM)
    head_spec = pl.BlockSpec(head_blk, lambda b, s, hp: (hp, b * spans + s, 0))
    grid = (batch, spans, ATT_HEADS // heads)
    if others is None:
        return pl.pallas_call(
            functools.partial(_attn_kernel, dilation=dilation, nb=nb),
            out_shape=(per_head, per_head),
            grid=grid,
            in_specs=qkv_specs,
            out_specs=(head_spec, head_spec),
            compiler_params=_params("parallel", "parallel", "parallel"),
            name=f"attn_d{dilation}",
        )(qkv, qkv, qkv, qkv, qkv)
    return pl.pallas_call(
        functools.partial(_attn_combine_kernel, dilation=dilation, nb=nb),
        out_shape=jax.ShapeDtypeStruct((batch * seq, ATT_GROUP_WIDTH), BF16),
        grid=grid,
        in_specs=qkv_specs + [head_spec] * 4,
        out_specs=pl.BlockSpec((span_tokens, hw), lambda b, s, hp: (b * spans + s, hp)),
        scratch_shapes=[pltpu.VMEM(head_blk, F32), pltpu.VMEM(head_blk, F32)],
        compiler_params=_params("parallel", "parallel", "parallel"),
        name=f"attn_d{dilation}_combine",
    )(qkv, qkv, qkv, qkv, qkv, *others)


RET_CHUNKS_PER_STEP = 16


def _ret_kernel(q_ref, k_ref, v_ref, g_ref, gn_ref, intra_ref, cross_ref, sdec_ref, cdec_ref,
                z_ref, state_ref, *, cb):
    @pl.when(pl.program_id(2) == 0)
    def _():
        state_ref[...] = jnp.zeros_like(state_ref)

    c = RET_CHUNK
    intra, cross_decay, state_decay, chunk_decay = intra_ref[...], cross_ref[...], sdec_ref[...], cdec_ref[...]
    gn = gn_ref[...]
    qs, inners, updates = [], [], []
    for t in range(cb):
        rows = slice(t * c, (t + 1) * c)
        q = q_ref[rows, :].astype(BF16)
        k = k_ref[rows, :]
        v = v_ref[rows, :]
        att = lax.dot_general(q, k.astype(BF16), (((1,), (1,)), ((), ())), preferred_element_type=F32) * intra
        inners.append(_mm(att.astype(BF16), v))
        kd = (k * state_decay).astype(BF16)
        updates.append(lax.dot_general(kd, v, (((0,), (0,)), ((), ())), preferred_element_type=F32))
        qs.append(q)
    state = state_ref[...]
    for t in range(cb):
        rows = slice(t * c, (t + 1) * c)
        r = inners[t] + _mm(qs[t], state.astype(BF16)) * cross_decay
        state = chunk_decay * state + updates[t]
        mu = jnp.mean(r, axis=-1, keepdims=True)
        var = jnp.mean(jnp.square(r - mu), axis=-1, keepdims=True)
        rn = (r - mu) * lax.rsqrt(var + GN_EPS) * gn
        g = g_ref[rows, :]
        z_ref[rows, :] = ((g * _sigmoid(g)) * rn).astype(z_ref.dtype)
    state_ref[...] = state


def _retention(qk, v, gates, gn_g, batch, seq):
    n = qk.shape[0]
    c = RET_CHUNK
    nc = seq // c
    cb = min(RET_CHUNKS_PER_STEP, nc)
    assert nc % cb == 0
    steps = nc // cb
    hh = jnp.arange(RET_HEADS, dtype=F32)
    log_gamma = jnp.log(1.0 - 2.0 ** (-5.0 - hh))
    idx = jnp.arange(c, dtype=F32)
    diff = idx[:, None] - idx[None, :]
    intra = jnp.where(diff >= 0, jnp.exp(jnp.maximum(diff, 0.0)[None] * log_gamma[:, None, None]), 0.0)
    cross_decay = jnp.exp((idx + 1.0)[None, :] * log_gamma[:, None])[..., None]
    state_decay = jnp.exp((c - 1.0 - idx)[None, :] * log_gamma[:, None])[..., None]
    chunk_decay = jnp.exp(c * log_gamma)[:, None, None]

    def rows(b, h, s):
        return b * steps + s

    kern = functools.partial(_ret_kernel, cb=cb)
    return pl.pallas_call(
        kern,
        out_shape=jax.ShapeDtypeStruct((n, RET_V_WIDTH), BF16),
        grid=(batch, RET_HEADS, steps),
        in_specs=[
            pl.BlockSpec((cb * c, RET_QK_DIM), lambda b, h, s: (rows(b, h, s), h)),
            pl.BlockSpec((cb * c, RET_QK_DIM), lambda b, h, s: (rows(b, h, s), RET_HEADS + h)),
            pl.BlockSpec((cb * c, RET_V_DIM), lambda b, h, s: (rows(b, h, s), h)),
            pl.BlockSpec((cb * c, RET_V_DIM), lambda b, h, s: (rows(b, h, s), h)),
            pl.BlockSpec((1, RET_V_DIM), lambda b, h, s: (0, h)),
            pl.BlockSpec((None, c, c), lambda b, h, s: (h, 0, 0)),
            pl.BlockSpec((None, c, 1), lambda b, h, s: (h, 0, 0)),
            pl.BlockSpec((None, c, 1), lambda b, h, s: (h, 0, 0)),
            pl.BlockSpec((None, 1, 1), lambda b, h, s: (h, 0, 0)),
        ],
        out_specs=pl.BlockSpec((cb * c, RET_V_DIM), lambda b, h, s: (rows(b, h, s), h)),
        scratch_shapes=[pltpu.VMEM((RET_QK_DIM, RET_V_DIM), F32)],
        compiler_params=_params("parallel", "parallel", "arbitrary"),
        name="retention",
    )(qk, qk, v, gates, gn_g, intra, cross_decay, state_decay, chunk_decay)


def _mix_kernel(oa_ref, z_ref, wa_ref, wr_ref, ga_ref, gr_ref, o_ref):
    ya = _mm(oa_ref[...], wa_ref[...])
    yr = _mm(z_ref[...], wr_ref[...])
    o_ref[...] = (_sigmoid(ga_ref[...]) * ya + _sigmoid(gr_ref[...]) * yr).astype(o_ref.dtype)


def _mix(o_att, z, wa, wr, gates, gate_col0):
    n = o_att.shape[0]
    d = wa.shape[1]
    tm, tn = _tile(n, 512), _tile(d, 1024)
    assert gate_col0 % tn == 0
    ga_off = gate_col0 // tn
    gr_off = ga_off + d // tn
    return pl.pallas_call(
        _mix_kernel,
        out_shape=jax.ShapeDtypeStruct((n, d), BF16),
        grid=(d // tn, n // tm),
        in_specs=[pl.BlockSpec((tm, o_att.shape[1]), lambda j, i: (i, 0)),
                  pl.BlockSpec((tm, z.shape[1]), lambda j, i: (i, 0)),
                  pl.BlockSpec((wa.shape[0], tn), lambda j, i: (0, j)),
                  pl.BlockSpec((wr.shape[0], tn), lambda j, i: (0, j)),
                  pl.BlockSpec((tm, tn), lambda j, i: (i, ga_off + j)),
                  pl.BlockSpec((tm, tn), lambda j, i: (i, gr_off + j))],
        out_specs=pl.BlockSpec((tm, tn), lambda j, i: (i, j)),
        compiler_params=_params("parallel", "parallel"),
        name="mix",
    )(o_att, z, wa, wr, gates, gates)


def _layer_norm_rows(t, g, b):
    mu = jnp.mean(t, axis=-1, keepdims=True)
    var = jnp.mean(jnp.square(t - mu), axis=-1, keepdims=True)
    return (t - mu) * lax.rsqrt(var + LN_EPS) * g + b


LN_ROW_CHUNK = 128


def _layer_norm_inplace(o_ref, g_ref, b_ref, cast_ref=None):
    g, b = g_ref[...], b_ref[...]
    chunk = min(LN_ROW_CHUNK, o_ref.shape[0])

    def body(r, carry):
        rows = pl.ds(pl.multiple_of(r * chunk, chunk), chunk)
        y = _layer_norm_rows(o_ref[rows, :], g, b)
        o_ref[rows, :] = y
        if cast_ref is not None:
            cast_ref[rows, :] = y.astype(cast_ref.dtype)
        return carry

    lax.fori_loop(0, o_ref.shape[0] // chunk, body, 0)


def _proj_ln_kernel(a_ref, w_ref, res_ref, g_ref, b_ref, of_ref, ob_ref, *, alpha):
    y = _layer_norm_rows(alpha * res_ref[...] + _mm(a_ref[...], w_ref[...]), g_ref[...], b_ref[...])
    of_ref[...] = y
    ob_ref[...] = y.astype(ob_ref.dtype)


def _proj_ln(a, w, res, g, b, alpha):
    n, k = a.shape
    d = w.shape[1]
    tm = _tile(n, 128)
    kern = functools.partial(_proj_ln_kernel, alpha=alpha)
    row = pl.BlockSpec((tm, d), lambda i: (i, 0))
    vec = pl.BlockSpec((1, d), lambda i: (0, 0))
    return pl.pallas_call(
        kern,
        out_shape=(jax.ShapeDtypeStruct((n, d), F32), jax.ShapeDtypeStruct((n, d), BF16)),
        grid=(n // tm,),
        in_specs=[pl.BlockSpec((tm, k), lambda i: (i, 0)),
                  pl.BlockSpec((k, d), lambda i: (0, 0), pipeline_mode=pl.Buffered(1)),
                  row, vec, vec],
        out_specs=(row, row),
        compiler_params=_params("parallel"),
        name="proj_ln",
    )(a, w, res, g, b)


def _swiglu_kernel(h_ref, wg_ref, wu_ref, o_ref):
    wg, wu = wg_ref[...], wu_ref[...]
    for rows in _row_halves(o_ref.shape[0]):
        h = h_ref[rows, :]
        gate = _mm(h, wg)
        up = _mm(h, wu)
        o_ref[rows, :] = ((gate * _sigmoid(gate)) * up).astype(o_ref.dtype)


def _swiglu(h_bf, wg, wu):
    n, k = h_bf.shape
    f = wg.shape[1]
    tm, tn = _tile(n, 2048), _tile(f, 256)
    w_spec = pl.BlockSpec((k, tn), lambda i, j: (0, j))
    return pl.pallas_call(
        _swiglu_kernel,
        out_shape=jax.ShapeDtypeStruct((n, f), BF16),
        grid=(n // tm, f // tn),
        in_specs=[pl.BlockSpec((tm, k), lambda i, j: (i, 0)), w_spec, w_spec],
        out_specs=pl.BlockSpec((tm, tn), lambda i, j: (i, j)),
        compiler_params=_params("parallel", "parallel"),
        name="swiglu",
    )(h_bf, wg, wu)


def _ple_kernel(hb_ref, wg_ref, p_ref, wu_ref, hf_ref, o_ref, *, alpha):
    gate = _sigmoid(_mm(hb_ref[...], wg_ref[...]))
    up = _mm(p_ref[...], wu_ref[...])
    o_ref[...] = alpha * hf_ref[...] + gate * up


def _ple_base(h_bf, h_f32, p_bf, wpg, wpu, alpha):
    n, k = h_bf.shape
    d = wpg.shape[1]
    tm, tn = _tile(n, 512), _tile(d, 1024)
    kern = functools.partial(_ple_kernel, alpha=alpha)
    return pl.pallas_call(
        kern,
        out_shape=jax.ShapeDtypeStruct((n, d), F32),
        grid=(d // tn, n // tm),
        in_specs=[pl.BlockSpec((tm, k), lambda j, i: (i, 0)),
                  pl.BlockSpec((k, tn), lambda j, i: (0, j)),
                  pl.BlockSpec((tm, p_bf.shape[1]), lambda j, i: (i, 0)),
                  pl.BlockSpec((p_bf.shape[1], tn), lambda j, i: (0, j)),
                  pl.BlockSpec((tm, tn), lambda j, i: (i, j))],
        out_specs=pl.BlockSpec((tm, tn), lambda j, i: (i, j)),
        compiler_params=_params("parallel", "parallel"),
        name="ple_base",
    )(h_bf, wpg, p_bf, wpu, h_f32)


def _down_ln_kernel(a_ref, w_ref, base_ref, g_ref, b_ref, o_ref, *, tn):
    j = pl.program_id(1)
    col = pl.multiple_of(j * tn, tn)
    w = w_ref[...]
    for rows in _row_halves(o_ref.shape[0]):
        o_ref[rows, pl.ds(col, tn)] = base_ref[rows, :] + _mm(a_ref[rows, :], w)

    @pl.when(j == pl.num_programs(1) - 1)
    def _():
        _layer_norm_inplace(o_ref, g_ref, b_ref)


def _down_ln(hidden, wd, base, g, b):
    n, f = hidden.shape
    d = wd.shape[1]
    tm, tn = _tile(n, 512), _tile(d, 256)
    kern = functools.partial(_down_ln_kernel, tn=tn)
    vec = pl.BlockSpec((1, d), lambda i, j: (0, 0))
    return pl.pallas_call(
        kern,
        out_shape=jax.ShapeDtypeStruct((n, d), F32),
        grid=(n // tm, d // tn),
        in_specs=[pl.BlockSpec((tm, f), lambda i, j: (i, 0)),
                  pl.BlockSpec((f, tn), lambda i, j: (0, j)),
                  pl.BlockSpec((tm, tn), lambda i, j: (i, j)),
                  vec, vec],
        out_specs=pl.BlockSpec((tm, d), lambda i, j: (i, 0)),
        compiler_params=_params("parallel", "arbitrary"),
        name="down_ln",
    )(hidden, wd, base, g, b)


def _layer(h_f32, p_bf, w_in, w_attn_out, w_ret_out, ret_gn_g, w_o, ln1_g, ln1_b,
           w_ffn_gate, w_ffn_up, w_ffn_down, w_ple_gate, w_ple_up, ln2_g, ln2_b,
           cos, sin, batch, seq, alpha):
    d_model = h_f32.shape[1]
    c_qr = 3 * ATT_QKV_WIDTH
    c_vr = c_qr + 2 * RET_QK_WIDTH
    c_gr = c_vr + RET_V_WIDTH
    assert w_in.shape[1] == c_gr + RET_V_WIDTH + 2 * d_model
    assert ATT_DILATIONS[0] == 1

    qkv_first, h_bf = _proj_units_from_f32(h_f32, w_in, 0, batch, seq)
    qk_r = _proj_rot(h_bf, w_in, c_qr, cos, sin, seq)
    v_r, w_attn_out, w_ret_out, w_o, w_ple_gate = _proj(
        h_bf, w_in, c_vr, RET_V_WIDTH, BF16, "proj_vr", cast=(w_attn_out, w_ret_out, w_o, w_ple_gate))
    gates, w_ffn_down = _proj(
        h_bf, w_in, c_gr, RET_V_WIDTH + 2 * d_model, F32, "proj_gates", cast=(w_ffn_down,))

    others = list(_attention(qkv_first, 1, batch, seq))
    for g, dil in list(enumerate(ATT_DILATIONS))[1:-1]:
        others.extend(_attention(_proj_units(h_bf, w_in, g, dil, batch, seq), dil, batch, seq))
    g, dil = len(ATT_DILATIONS) - 1, ATT_DILATIONS[-1]
    o_att = _attention(_proj_units(h_bf, w_in, g, dil, batch, seq), dil, batch, seq, others=others)

    z = _retention(qk_r, v_r, gates, ret_gn_g, batch, seq)

    mixed = _mix(o_att, z, w_attn_out, w_ret_out, gates, RET_V_WIDTH)
    h1_f32, h1_bf = _proj_ln(mixed, w_o, h_f32, ln1_g, ln1_b, alpha)

    hidden = _swiglu(h1_bf, w_ffn_gate, w_ffn_up)
    base = _ple_base(h1_bf, h1_f32, p_bf, w_ple_gate, w_ple_up, alpha)
    return _down_ln(hidden, w_ffn_down, base, ln2_g, ln2_b)


def kernel(x, p, w_in, w_attn_out, w_ret_out, ret_gn_g, w_o, ln1_g, ln1_b, w_ffn_gate, w_ffn_up,
           w_ffn_down, w_ple_gate, w_ple_up, ln2_g, ln2_b):
    batch, seq, d_model = x.shape
    depth = w_in.shape[0]
    n = batch * seq
    alpha = (2 * depth) ** 0.25

    half = RET_QK_DIM // 2
    pos = jnp.arange(seq, dtype=F32)
    inv_freq = RET_ROPE_BASE ** (-jnp.arange(half, dtype=F32) / half)
    ang = pos[:, None] * inv_freq[None, :]
    cos, sin = jnp.cos(ang), jnp.sin(ang)

    h = x.reshape(n, d_model)
    for i in range(depth):
        bf = lambda w: w[i].astype(BF16)
        row = lambda v: v[i].reshape(1, -1)
        h = _layer(h, p[i].reshape(n, -1).astype(BF16),
                   bf(w_in), w_attn_out[i], w_ret_out[i], row(ret_gn_g), w_o[i], row(ln1_g), row(ln1_b),
                   w_ffn_gate[i], w_ffn_up[i], w_ffn_down[i], w_ple_gate[i], bf(w_ple_up),
                   row(ln2_g), row(ln2_b), cos, sin, batch, seq, alpha)
    return h.reshape(batch, seq, d_model).astype(x.dtype)
```

```python
import functools

import jax
import jax.numpy as jnp
from jax import lax
from jax.experimental import pallas as pl
from jax.experimental.pallas import tpu as pltpu

ATT_HEAD_DIM = 128
ATT_HEADS = 8
ATT_DILATIONS = (1, 4, 16)
ATT_BLOCK = 128
ATT_GROUP_WIDTH = ATT_HEADS * ATT_HEAD_DIM
ATT_QKV_WIDTH = len(ATT_DILATIONS) * ATT_GROUP_WIDTH
RET_HEADS = 8
RET_QK_DIM = 256
RET_V_DIM = 512
RET_QK_WIDTH = RET_HEADS * RET_QK_DIM
RET_V_WIDTH = RET_HEADS * RET_V_DIM
RET_CHUNK = 128
RET_ROPE_BASE = 10000.0
LN_EPS = 1e-5
GN_EPS = 1e-6
NEG_INF = -1e30

V7X_VMEM_LIMIT_BYTES = 56 * 1024 * 1024
LANE = 128

BF16 = jnp.bfloat16
F32 = jnp.float32


def _params(*semantics):
    return pltpu.CompilerParams(dimension_semantics=semantics, vmem_limit_bytes=V7X_VMEM_LIMIT_BYTES)


def _tile(dim, pref):
    if dim <= pref:
        return dim
    t = (pref // LANE) * LANE
    while t > LANE and dim % t:
        t -= LANE
    assert dim % t == 0, (dim, pref)
    return t


def _sigmoid(v):
    return 1.0 / (1.0 + jnp.exp(-v))


def _mm(a, b):
    return lax.dot_general(a, b, (((1,), (0,)), ((), ())), preferred_element_type=F32)


ROW_SPLITS = 2


def _row_halves(n_rows):
    rows = n_rows // ROW_SPLITS
    return [slice(c * rows, (c + 1) * rows) for c in range(ROW_SPLITS)]


BF16_SUBLANES = 16


COL_BLOCK = 1024


def _cast_job(w, col_blocks, steps, linear_step):
    rows, cols = w.shape
    ncb, width = (1, cols) if col_blocks is None else (len(col_blocks), COL_BLOCK)
    slab = -(-(-(-rows * ncb // steps)) // BF16_SUBLANES) * BF16_SUBLANES
    while rows % slab:
        slab += BF16_SUBLANES
    pieces = (rows // slab) * ncb
    assert pieces <= steps

    def piece(*idx):
        return jnp.minimum(linear_step(*idx), pieces - 1)

    def source_col(local):
        if col_blocks is None:
            return 0
        return sum(jnp.where(local == i, cb, 0) for i, cb in enumerate(col_blocks))

    in_spec = pl.BlockSpec((slab, width), lambda *idx: (piece(*idx) // ncb, source_col(piece(*idx) % ncb)))
    out_spec = pl.BlockSpec((slab, width), lambda *idx: (piece(*idx) // ncb, piece(*idx) % ncb))
    return in_spec, out_spec, jax.ShapeDtypeStruct((rows, ncb * width), BF16)


def _cast_jobs(cast, steps, linear_step):
    jobs = [_cast_job(w, cbs, steps, linear_step) for w, cbs in cast]
    return [j[0] for j in jobs], [j[1] for j in jobs], [j[2] for j in jobs]


def _run_casts(srcs, dsts):
    for src, dst in zip(srcs, dsts):
        dst[...] = src[...].astype(dst.dtype)


def _proj_kernel(x_ref, w_ref, *refs):
    n_cast = (len(refs) - 1) // 2
    o_ref = refs[n_cast]
    o_ref[...] = _mm(x_ref[...], w_ref[...]).astype(o_ref.dtype)
    _run_casts(refs[:n_cast], refs[n_cast + 1:])


def _proj(x_bf, w_bf, col0, width, out_dtype, name, cast=(), rows_per_step=1024):
    n, k = x_bf.shape
    tm, tn = _tile(n, rows_per_step), _tile(width, 1024)
    assert col0 % tn == 0
    off = col0 // tn
    n_i = n // tm
    steps = (width // tn) * n_i
    cast_in, cast_out, cast_shapes = _cast_jobs(cast, steps, lambda j, i: j * n_i + i)
    outs = pl.pallas_call(
        _proj_kernel,
        out_shape=[jax.ShapeDtypeStruct((n, width), out_dtype)] + cast_shapes,
        grid=(width // tn, n_i),
        in_specs=[pl.BlockSpec((tm, k), lambda j, i: (i, 0)),
                  pl.BlockSpec((k, tn), lambda j, i: (0, off + j))] + cast_in,
        out_specs=[pl.BlockSpec((tm, tn), lambda j, i: (i, j))] + cast_out,
        compiler_params=_params("arbitrary", "arbitrary"),
        name=name,
    )(x_bf, w_bf, *[w for w, _ in cast])
    return outs if cast else outs[0]


def _proj_rot_kernel(x_ref, w_ref, cos_ref, sin_ref, *refs, n_q_blocks, k_scale):
    n_cast = (len(refs) - 1) // 2
    o_ref = refs[n_cast]
    _run_casts(refs[:n_cast], refs[n_cast + 1:])
    j = pl.program_id(0)
    acc = _mm(x_ref[...], w_ref[...])
    scale = jnp.where(j >= n_q_blocks, k_scale, 1.0).astype(F32)
    cos = cos_ref[...]
    sin = sin_ref[...]
    half = RET_QK_DIM // 2
    for h in range(acc.shape[1] // RET_QK_DIM):
        lo = h * RET_QK_DIM
        t1 = acc[:, lo:lo + half]
        t2 = acc[:, lo + half:lo + RET_QK_DIM]
        o_ref[:, lo:lo + half] = (t1 * cos - t2 * sin) * scale
        o_ref[:, lo + half:lo + RET_QK_DIM] = (t1 * sin + t2 * cos) * scale


def _proj_rot(x_bf, w_bf, col0, cos, sin, seq, cast=()):
    n, k = x_bf.shape
    width = 2 * RET_QK_WIDTH
    tm, tn = _tile(seq, 1024), 1024
    assert col0 % tn == 0 and RET_QK_WIDTH % tn == 0 and n % seq == 0
    off = col0 // tn
    pos_blocks = seq // tm
    n_i = n // tm
    kern = functools.partial(_proj_rot_kernel, n_q_blocks=RET_QK_WIDTH // tn, k_scale=RET_QK_DIM ** -0.5)
    half = RET_QK_DIM // 2
    cast_in, cast_out, cast_shapes = _cast_jobs(cast, (width // tn) * n_i, lambda j, i: j * n_i + i)
    outs = pl.pallas_call(
        kern,
        out_shape=[jax.ShapeDtypeStruct((n, width), F32)] + cast_shapes,
        grid=(width // tn, n_i),
        in_specs=[pl.BlockSpec((tm, k), lambda j, i: (i, 0)),
                  pl.BlockSpec((k, tn), lambda j, i: (0, off + j)),
                  pl.BlockSpec((tm, half), lambda j, i: (i % pos_blocks, 0)),
                  pl.BlockSpec((tm, half), lambda j, i: (i % pos_blocks, 0))] + cast_in,
        out_specs=[pl.BlockSpec((tm, tn), lambda j, i: (i, j))] + cast_out,
        compiler_params=_params("arbitrary", "arbitrary"),
        name="proj_rot",
    )(x_bf, w_bf, cos, sin, *[w for w, _ in cast])
    return outs if cast else outs[0]


ATT_SPAN_BLOCKS = 16
ATT_HEADS_PER_STEP = 4
ATT_HEADS_PER_COMBINE_STEP = 2


def _proj_units_kernel(x_ref, w_ref, o_ref, acc_ref, *, dilation):
    if dilation == 1:
        o_ref[0, 0] = _mm(x_ref[...], w_ref[...]).astype(o_ref.dtype)
        return
    w = w_ref[...]
    slabs = w.shape[1] // LANE
    for tok in _row_halves(x_ref.shape[0]):
        acc = _mm(x_ref[tok, :], w)
        for s in range(slabs):
            acc_ref[s, tok, :] = acc[:, s * LANE:(s + 1) * LANE]
        rows = acc.shape[0] // dilation
        dst = slice(tok.start // dilation, tok.start // dilation + rows)
        for r in range(dilation):
            for s in range(slabs):
                piece = acc_ref[s, pl.ds(tok.start + r, rows, stride=dilation), :]
                o_ref[r, 0, dst, s * LANE:(s + 1) * LANE] = piece.astype(o_ref.dtype)


def _proj_units(x_bf, w_bf, first_block, dilation, batch, seq):
    n, k = x_bf.shape
    tm, tn = _tile(seq, 1024), ATT_GROUP_WIDTH
    assert (tm // ROW_SPLITS) % (dilation * BF16_SUBLANES) == 0 and seq % tm == 0
    rows = tm // dilation
    tiles = seq // tm
    kern = functools.partial(_proj_units_kernel, dilation=dilation)
    out = pl.pallas_call(
        kern,
        out_shape=jax.ShapeDtypeStruct((3, batch * dilation, tiles, rows, tn), BF16),
        grid=(3, n // tm),
        in_specs=[pl.BlockSpec((tm, k), lambda j, i: (i, 0)),
                  pl.BlockSpec((k, tn), lambda j, i: (0, first_block + j))],
        out_specs=pl.BlockSpec((None, dilation, 1, rows, tn), lambda j, i: (j, i // tiles, i % tiles, 0, 0)),
        scratch_shapes=[pltpu.VMEM((tn // LANE, tm, LANE), F32)],
        compiler_params=_params("parallel", "parallel"),
        name=f"proj_att_d{dilation}",
    )(x_bf, w_bf)
    n_blk = seq // (dilation * ATT_BLOCK)
    return out.reshape(3, batch * dilation, n_blk, ATT_BLOCK, tn)


def _proj_units_f32_kernel(x_ref, w_ref, *refs):
    n_cast = (len(refs) - 2) // 2
    o_ref, xb_ref = refs[n_cast], refs[n_cast + 1]
    xb = x_ref[...].astype(xb_ref.dtype)
    xb_ref[...] = xb
    tn = o_ref.shape[-1]
    for part in range(3):
        o_ref[part, 0, 0] = _mm(xb, w_ref[:, part * tn:(part + 1) * tn]).astype(o_ref.dtype)
    _run_casts(refs[:n_cast], refs[n_cast + 2:])


def _proj_units_from_f32(x_f32, w_qkv, batch, seq, cast=()):
    n, k = x_f32.shape
    tm, tn = _tile(seq, 256), ATT_GROUP_WIDTH
    assert w_qkv.shape == (k, 3 * tn)
    tiles = seq // tm
    cast_in, cast_out, cast_shapes = _cast_jobs(cast, n // tm, lambda i: i)
    outs = pl.pallas_call(
        _proj_units_f32_kernel,
        out_shape=[jax.ShapeDtypeStruct((3, batch, tiles, tm, tn), BF16),
                   jax.ShapeDtypeStruct((n, k), BF16)] + cast_shapes,
        grid=(n // tm,),
        in_specs=[pl.BlockSpec((tm, k), lambda i: (i, 0)),
                  pl.BlockSpec((k, 3 * tn), lambda i: (0, 0), pipeline_mode=pl.Buffered(1))] + cast_in,
        out_specs=[pl.BlockSpec((3, 1, 1, tm, tn), lambda i: (0, i // tiles, i % tiles, 0, 0)),
                   pl.BlockSpec((tm, k), lambda i: (i, 0))] + cast_out,
        compiler_params=_params("arbitrary"),
        name="proj_att_d1",
    )(x_f32, w_qkv, *[w for w, _ in cast])
    return [outs[0].reshape(3, batch, seq // ATT_BLOCK, ATT_BLOCK, tn)] + list(outs[1:])


def _attn_span(q_ref, k_ref, kp_ref, v_ref, vp_ref, o_ref, lse_ref, *, dilation, nb):
    qi = lax.broadcasted_iota(jnp.int32, (ATT_BLOCK, 2 * ATT_BLOCK), 0)
    kj = lax.broadcasted_iota(jnp.int32, (ATT_BLOCK, 2 * ATT_BLOCK), 1)
    in_window = kj <= qi + ATT_BLOCK
    valid_inner = jnp.logical_and(kj >= qi, in_window)
    lo = jnp.where(pl.program_id(1) > 0, 0, ATT_BLOCK)
    valid_first = jnp.logical_and(kj >= jnp.maximum(qi, lo), in_window)
    scale = ATT_HEAD_DIM ** -0.5
    dn = (((1,), (1,)), ((), ()))
    ones = jnp.ones((2 * ATT_BLOCK, ATT_HEAD_DIM), BF16)
    units = [(r, i, h) for r in range(dilation) for i in range(nb) for h in range(o_ref.shape[0])]

    def cols(h):
        return slice(h * ATT_HEAD_DIM, (h + 1) * ATT_HEAD_DIM)

    def with_prev(cur_ref, prev_ref, r, i, h):
        prev = prev_ref[r, 0, :, cols(h)] if i == 0 else cur_ref[r, i - 1, :, cols(h)]
        return jnp.concatenate([prev, cur_ref[r, i, :, cols(h)]], axis=0)

    scores = [lax.dot_general(q_ref[r, i, :, cols(h)], with_prev(k_ref, kp_ref, r, i, h), dn,
                              preferred_element_type=F32) * scale for r, i, h in units]
    probs, maxes = [], []
    for (r, i, h), s in zip(units, scores):
        s = jnp.where(valid_first if i == 0 else valid_inner, s, NEG_INF)
        m = jnp.max(s, axis=-1, keepdims=True)
        probs.append(jnp.exp(s - m).astype(BF16))
        maxes.append(m)
    for (r, i, h), p, m in zip(units, probs, maxes):
        v_aug = jnp.concatenate([with_prev(v_ref, vp_ref, r, i, h), ones], axis=1)
        acc = _mm(p, v_aug)
        den = acc[:, ATT_HEAD_DIM:]
        if dilation == 1:
            rows = pl.ds(i * ATT_BLOCK, ATT_BLOCK)
        else:
            rows = pl.ds(i * ATT_BLOCK * dilation + r, ATT_BLOCK, stride=dilation)
        o_ref[h, rows, :] = acc[:, :ATT_HEAD_DIM] / den
        lse_ref[h, rows, :] = m + jnp.log(den)


def _attn_kernel(q_ref, k_ref, kp_ref, v_ref, vp_ref, o_ref, lse_ref, *, dilation, nb):
    _attn_span(q_ref, k_ref, kp_ref, v_ref, vp_ref, o_ref, lse_ref, dilation=dilation, nb=nb)


ATT_COMBINE_ROWS = 256


def _attn_combine_kernel(q_ref, k_ref, kp_ref, v_ref, vp_ref, o1, l1, o2, l2, out_ref, o3, l3, *, dilation, nb):
    _attn_span(q_ref, k_ref, kp_ref, v_ref, vp_ref, o3, l3, dilation=dilation, nb=nb)

    def body(c, carry):
        rows = pl.ds(pl.multiple_of(c * ATT_COMBINE_ROWS, ATT_COMBINE_ROWS), ATT_COMBINE_ROWS)
        for h in range(o3.shape[0]):
            a, b, d = l1[h, rows, :], l2[h, rows, :], l3[h, rows, :]
            m = jnp.maximum(jnp.maximum(a, b), d)
            ea, eb, ed = jnp.exp(a - m), jnp.exp(b - m), jnp.exp(d - m)
            s = ea + eb + ed
            out = (ea / s) * o1[h, rows, :] + (eb / s) * o2[h, rows, :] + (ed / s) * o3[h, rows, :]
            out_ref[rows, h * ATT_HEAD_DIM:(h + 1) * ATT_HEAD_DIM] = out.astype(out_ref.dtype)
        return carry

    lax.fori_loop(0, out_ref.shape[0] // ATT_COMBINE_ROWS, body, 0)


def _attention(qkv, dilation, batch, seq, others=None):
    n_blk = qkv.shape[2]
    assert ATT_SPAN_BLOCKS % dilation == 0
    nb = ATT_SPAN_BLOCKS // dilation
    assert n_blk % nb == 0
    spans = n_blk // nb
    span_tokens = ATT_SPAN_BLOCKS * ATT_BLOCK
    heads = ATT_HEADS_PER_STEP if others is None else ATT_HEADS_PER_COMBINE_STEP
    hw = heads * ATT_HEAD_DIM
    cur = (None, dilation, nb, ATT_BLOCK, hw)
    one = (None, dilation, 1, ATT_BLOCK, hw)

    def at(part):
        return lambda b, s, hp: (part, b, s, 0, hp)

    def before(part):
        return lambda b, s, hp: (part, b, jnp.maximum(s * nb - 1, 0), 0, hp)

    qkv_specs = [pl.BlockSpec(cur, at(0)), pl.BlockSpec(cur, at(1)), pl.BlockSpec(one, before(1)),
                 pl.BlockSpec(cur, at(2)), pl.BlockSpec(one, before(2))]
    per_head = jax.ShapeDtypeStruct((ATT_HEADS, batch * seq, ATT_HEAD_DIM), F32)
    head_blk = (heads, span_tokens, ATT_HEAD_DIM)
    head_spec = pl.BlockSpec(head_blk, lambda b, s, hp: (hp, b * spans + s, 0))
    grid = (batch, spans, ATT_HEADS // heads)
    if others is None:
        return pl.pallas_call(
            functools.partial(_attn_kernel, dilation=dilation, nb=nb),
            out_shape=(per_head, per_head),
            grid=grid,
            in_specs=qkv_specs,
            out_specs=(head_spec, head_spec),
            compiler_params=_params("parallel", "parallel", "parallel"),
            name=f"attn_d{dilation}",
        )(qkv, qkv, qkv, qkv, qkv)
    return pl.pallas_call(
        functools.partial(_attn_combine_kernel, dilation=dilation, nb=nb),
        out_shape=jax.ShapeDtypeStruct((batch * seq, ATT_GROUP_WIDTH), BF16),
        grid=grid,
        in_specs=qkv_specs + [head_spec] * 4,
        out_specs=pl.BlockSpec((span_tokens, hw), lambda b, s, hp: (b * spans + s, hp)),
        scratch_shapes=[pltpu.VMEM(head_blk, F32), pltpu.VMEM(head_blk, F32)],
        compiler_params=_params("parallel", "parallel", "parallel"),
        name=f"attn_d{dilation}_combine",
    )(qkv, qkv, qkv, qkv, qkv, *others)


RET_CHUNKS_PER_STEP = 16


def _ret_kernel(q_ref, k_ref, v_ref, g_ref, gn_ref, intra_ref, cross_ref, sdec_ref, cdec_ref,
                z_ref, state_ref, *, cb):
    @pl.when(pl.program_id(2) == 0)
    def _():
        state_ref[...] = jnp.zeros_like(state_ref)

    c = RET_CHUNK
    intra, cross_decay, state_decay, chunk_decay = intra_ref[...], cross_ref[...], sdec_ref[...], cdec_ref[...]
    gn = gn_ref[...]
    qs, inners, updates = [], [], []
    for t in range(cb):
        rows = slice(t * c, (t + 1) * c)
        q = q_ref[rows, :].astype(BF16)
        k = k_ref[rows, :]
        v = v_ref[rows, :]
        att = lax.dot_general(q, k.astype(BF16), (((1,), (1,)), ((), ())), preferred_element_type=F32) * intra
        inners.append(_mm(att.astype(BF16), v))
        kd = (k * state_decay).astype(BF16)
        updates.append(lax.dot_general(kd, v, (((0,), (0,)), ((), ())), preferred_element_type=F32))
        qs.append(q)
    state = state_ref[...]
    for t in range(cb):
        rows = slice(t * c, (t + 1) * c)
        r = inners[t] + _mm(qs[t], state.astype(BF16)) * cross_decay
        state = chunk_decay * state + updates[t]
        mu = jnp.mean(r, axis=-1, keepdims=True)
        var = jnp.mean(jnp.square(r - mu), axis=-1, keepdims=True)
        rn = (r - mu) * lax.rsqrt(var + GN_EPS) * gn
        g = g_ref[rows, :]
        z_ref[rows, :] = ((g * _sigmoid(g)) * rn).astype(z_ref.dtype)
    state_ref[...] = state


def _retention(qk, v, gates, gn_g, batch, seq):
    n = qk.shape[0]
    c = RET_CHUNK
    nc = seq // c
    cb = min(RET_CHUNKS_PER_STEP, nc)
    assert nc % cb == 0
    steps = nc // cb
    hh = jnp.arange(RET_HEADS, dtype=F32)
    log_gamma = jnp.log(1.0 - 2.0 ** (-5.0 - hh))
    idx = jnp.arange(c, dtype=F32)
    diff = idx[:, None] - idx[None, :]
    intra = jnp.where(diff >= 0, jnp.exp(jnp.maximum(diff, 0.0)[None] * log_gamma[:, None, None]), 0.0)
    cross_decay = jnp.exp((idx + 1.0)[None, :] * log_gamma[:, None])[..., None]
    state_decay = jnp.exp((c - 1.0 - idx)[None, :] * log_gamma[:, None])[..., None]
    chunk_decay = jnp.exp(c * log_gamma)[:, None, None]

    def rows(b, h, s):
        return b * steps + s

    kern = functools.partial(_ret_kernel, cb=cb)
    return pl.pallas_call(
        kern,
        out_shape=jax.ShapeDtypeStruct((n, RET_V_WIDTH), BF16),
        grid=(batch, RET_HEADS, steps),
        in_specs=[
            pl.BlockSpec((cb * c, RET_QK_DIM), lambda b, h, s: (rows(b, h, s), h)),
            pl.BlockSpec((cb * c, RET_QK_DIM), lambda b, h, s: (rows(b, h, s), RET_HEADS + h)),
            pl.BlockSpec((cb * c, RET_V_DIM), lambda b, h, s: (rows(b, h, s), h)),
            pl.BlockSpec((cb * c, RET_V_DIM), lambda b, h, s: (rows(b, h, s), h)),
            pl.BlockSpec((1, RET_V_DIM), lambda b, h, s: (0, h)),
            pl.BlockSpec((None, c, c), lambda b, h, s: (h, 0, 0)),
            pl.BlockSpec((None, c, 1), lambda b, h, s: (h, 0, 0)),
            pl.BlockSpec((None, c, 1), lambda b, h, s: (h, 0, 0)),
            pl.BlockSpec((None, 1, 1), lambda b, h, s: (h, 0, 0)),
        ],
        out_specs=pl.BlockSpec((cb * c, RET_V_DIM), lambda b, h, s: (rows(b, h, s), h)),
        scratch_shapes=[pltpu.VMEM((RET_QK_DIM, RET_V_DIM), F32)],
        compiler_params=_params("parallel", "parallel", "arbitrary"),
        name="retention",
    )(qk, qk, v, gates, gn_g, intra, cross_decay, state_decay, chunk_decay)


def _mix_kernel(oa_ref, z_ref, wa_ref, wr_ref, ga_ref, gr_ref, o_ref):
    ya = _mm(oa_ref[...], wa_ref[...])
    yr = _mm(z_ref[...], wr_ref[...])
    o_ref[...] = (_sigmoid(ga_ref[...]) * ya + _sigmoid(gr_ref[...]) * yr).astype(o_ref.dtype)


def _mix(o_att, z, wa, wr, gates, gate_col0):
    n = o_att.shape[0]
    d = wa.shape[1]
    tm, tn = _tile(n, 512), _tile(d, 1024)
    assert gate_col0 % tn == 0
    ga_off = gate_col0 // tn
    gr_off = ga_off + d // tn
    return pl.pallas_call(
        _mix_kernel,
        out_shape=jax.ShapeDtypeStruct((n, d), BF16),
        grid=(d // tn, n // tm),
        in_specs=[pl.BlockSpec((tm, o_att.shape[1]), lambda j, i: (i, 0)),
                  pl.BlockSpec((tm, z.shape[1]), lambda j, i: (i, 0)),
                  pl.BlockSpec((wa.shape[0], tn), lambda j, i: (0, j)),
                  pl.BlockSpec((wr.shape[0], tn), lambda j, i: (0, j)),
                  pl.BlockSpec((tm, tn), lambda j, i: (i, ga_off + j)),
                  pl.BlockSpec((tm, tn), lambda j, i: (i, gr_off + j))],
        out_specs=pl.BlockSpec((tm, tn), lambda j, i: (i, j)),
        compiler_params=_params("parallel", "parallel"),
        name="mix",
    )(o_att, z, wa, wr, gates, gates)


def _layer_norm_rows(t, g, b):
    mu = jnp.mean(t, axis=-1, keepdims=True)
    var = jnp.mean(jnp.square(t - mu), axis=-1, keepdims=True)
    return (t - mu) * lax.rsqrt(var + LN_EPS) * g + b


LN_ROW_CHUNK = 128


def _layer_norm_inplace(o_ref, g_ref, b_ref, cast_ref=None):
    g, b = g_ref[...], b_ref[...]
    chunk = min(LN_ROW_CHUNK, o_ref.shape[0])

    def body(r, carry):
        rows = pl.ds(pl.multiple_of(r * chunk, chunk), chunk)
        y = _layer_norm_rows(o_ref[rows, :], g, b)
        o_ref[rows, :] = y
        if cast_ref is not None:
            cast_ref[rows, :] = y.astype(cast_ref.dtype)
        return carry

    lax.fori_loop(0, o_ref.shape[0] // chunk, body, 0)


def _proj_ln_kernel(a_ref, w_ref, res_ref, g_ref, b_ref, of_ref, ob_ref, *, alpha):
    y = _layer_norm_rows(alpha * res_ref[...] + _mm(a_ref[...], w_ref[...]), g_ref[...], b_ref[...])
    of_ref[...] = y
    ob_ref[...] = y.astype(ob_ref.dtype)


def _proj_ln(a, w, res, g, b, alpha):
    n, k = a.shape
    d = w.shape[1]
    tm = _tile(n, 128)
    kern = functools.partial(_proj_ln_kernel, alpha=alpha)
    row = pl.BlockSpec((tm, d), lambda i: (i, 0))
    vec = pl.BlockSpec((1, d), lambda i: (0, 0))
    return pl.pallas_call(
        kern,
        out_shape=(jax.ShapeDtypeStruct((n, d), F32), jax.ShapeDtypeStruct((n, d), BF16)),
        grid=(n // tm,),
        in_specs=[pl.BlockSpec((tm, k), lambda i: (i, 0)),
                  pl.BlockSpec((k, d), lambda i: (0, 0), pipeline_mode=pl.Buffered(1)),
                  row, vec, vec],
        out_specs=(row, row),
        compiler_params=_params("parallel"),
        name="proj_ln",
    )(a, w, res, g, b)


def _swiglu_kernel(h_ref, wg_ref, wu_ref, o_ref):
    wg, wu = wg_ref[...], wu_ref[...]
    for rows in _row_halves(o_ref.shape[0]):
        h = h_ref[rows, :]
        gate = _mm(h, wg)
        up = _mm(h, wu)
        o_ref[rows, :] = ((gate * _sigmoid(gate)) * up).astype(o_ref.dtype)


def _swiglu(h_bf, wg, wu):
    n, k = h_bf.shape
    f = wg.shape[1]
    tm, tn = _tile(n, 2048), _tile(f, 256)
    w_spec = pl.BlockSpec((k, tn), lambda i, j: (0, j))
    return pl.pallas_call(
        _swiglu_kernel,
        out_shape=jax.ShapeDtypeStruct((n, f), BF16),
        grid=(n // tm, f // tn),
        in_specs=[pl.BlockSpec((tm, k), lambda i, j: (i, 0)), w_spec, w_spec],
        out_specs=pl.BlockSpec((tm, tn), lambda i, j: (i, j)),
        compiler_params=_params("parallel", "parallel"),
        name="swiglu",
    )(h_bf, wg, wu)


def _ple_kernel(hb_ref, wg_ref, p_ref, wu_ref, hf_ref, o_ref, *, alpha):
    gate = _sigmoid(_mm(hb_ref[...], wg_ref[...]))
    up = _mm(p_ref[...], wu_ref[...])
    o_ref[...] = alpha * hf_ref[...] + gate * up


def _ple_base(h_bf, h_f32, p_bf, wpg, wpu, alpha):
    n, k = h_bf.shape
    d = wpg.shape[1]
    tm, tn = _tile(n, 512), _tile(d, 1024)
    kern = functools.partial(_ple_kernel, alpha=alpha)
    return pl.pallas_call(
        kern,
        out_shape=jax.ShapeDtypeStruct((n, d), F32),
        grid=(d // tn, n // tm),
        in_specs=[pl.BlockSpec((tm, k), lambda j, i: (i, 0)),
                  pl.BlockSpec((k, tn), lambda j, i: (0, j)),
                  pl.BlockSpec((tm, p_bf.shape[1]), lambda j, i: (i, 0)),
                  pl.BlockSpec((p_bf.shape[1], tn), lambda j, i: (0, j)),
                  pl.BlockSpec((tm, tn), lambda j, i: (i, j))],
        out_specs=pl.BlockSpec((tm, tn), lambda j, i: (i, j)),
        compiler_params=_params("parallel", "parallel"),
        name="ple_base",
    )(h_bf, wpg, p_bf, wpu, h_f32)


def _down_ln_kernel(a_ref, w_ref, base_ref, g_ref, b_ref, o_ref, *, tn):
    j = pl.program_id(1)
    col = pl.multiple_of(j * tn, tn)
    o_ref[:, pl.ds(col, tn)] = base_ref[...] + _mm(a_ref[...], w_ref[...])

    @pl.when(j == pl.num_programs(1) - 1)
    def _():
        _layer_norm_inplace(o_ref, g_ref, b_ref)


def _down_ln(hidden, wd, base, g, b):
    n, f = hidden.shape
    d = wd.shape[1]
    tm, tn = _tile(n, 512), _tile(d, 256)
    kern = functools.partial(_down_ln_kernel, tn=tn)
    vec = pl.BlockSpec((1, d), lambda i, j: (0, 0))
    return pl.pallas_call(
        kern,
        out_shape=jax.ShapeDtypeStruct((n, d), F32),
        grid=(n // tm, d // tn),
        in_specs=[pl.BlockSpec((tm, f), lambda i, j: (i, 0)),
                  pl.BlockSpec((f, tn), lambda i, j: (0, j)),
                  pl.BlockSpec((tm, tn), lambda i, j: (i, j)),
                  vec, vec],
        out_specs=pl.BlockSpec((tm, d), lambda i, j: (i, 0)),
        compiler_params=_params("parallel", "arbitrary"),
        name="down_ln",
    )(hidden, wd, base, g, b)


def _layer(h_f32, p_bf, w_in, w_attn_out, w_ret_out, ret_gn_g, w_o, ln1_g, ln1_b,
           w_ffn_gate, w_ffn_up, w_ffn_down, w_ple_gate, w_ple_up, ln2_g, ln2_b,
           cos, sin, batch, seq, alpha):
    d_model = h_f32.shape[1]
    c_qr = 3 * ATT_QKV_WIDTH
    c_vr = c_qr + 2 * RET_QK_WIDTH
    c_gr = c_vr + RET_V_WIDTH
    c_end = c_gr + RET_V_WIDTH + 2 * d_model
    assert w_in.shape[1] == c_end and c_end % COL_BLOCK == 0
    assert ATT_DILATIONS[0] == 1 and ATT_GROUP_WIDTH == COL_BLOCK
    n_groups = len(ATT_DILATIONS)

    def att_blocks(g):
        return [part * n_groups + g for part in range(3)]

    def blocks(c0, c1):
        return list(range(c0 // COL_BLOCK, c1 // COL_BLOCK))

    w_first = jnp.concatenate([w_in[:, b * COL_BLOCK:(b + 1) * COL_BLOCK] for b in att_blocks(0)],
                              axis=1).astype(BF16)
    qkv_first, h_bf, w_rot, w_o, w_ple_gate = _proj_units_from_f32(
        h_f32, w_first, batch, seq, cast=((w_in, blocks(c_qr, c_vr)), (w_o, None), (w_ple_gate, None)))
    qk_r, w_vr, w_attn_out = _proj_rot(
        h_bf, w_rot, 0, cos, sin, seq, cast=((w_in, blocks(c_vr, c_gr)), (w_attn_out, None)))
    v_r, w_gates, w_ret_out = _proj(h_bf, w_vr, 0, RET_V_WIDTH, BF16, "proj_vr", rows_per_step=512,
                                    cast=((w_in, blocks(c_gr, c_end)), (w_ret_out, None)))
    later_groups = list(range(1, n_groups))
    gates, w_att_later, w_ffn_down = _proj(
        h_bf, w_gates, 0, RET_V_WIDTH + 2 * d_model, F32, "proj_gates",
        cast=((w_in, sum((att_blocks(g) for g in later_groups), [])), (w_ffn_down, None)))

    others = list(_attention(qkv_first, 1, batch, seq))
    for pos, g in enumerate(later_groups):
        qkv = _proj_units(h_bf, w_att_later, 3 * pos, ATT_DILATIONS[g], batch, seq)
        if g < n_groups - 1:
            others.extend(_attention(qkv, ATT_DILATIONS[g], batch, seq))
        else:
            o_att = _attention(qkv, ATT_DILATIONS[g], batch, seq, others=others)

    z = _retention(qk_r, v_r, gates, ret_gn_g, batch, seq)

    mixed = _mix(o_att, z, w_attn_out, w_ret_out, gates, RET_V_WIDTH)
    h1_f32, h1_bf = _proj_ln(mixed, w_o, h_f32, ln1_g, ln1_b, alpha)

    hidden = _swiglu(h1_bf, w_ffn_gate, w_ffn_up)
    base = _ple_base(h1_bf, h1_f32, p_bf, w_ple_gate, w_ple_up, alpha)
    return _down_ln(hidden, w_ffn_down, base, ln2_g, ln2_b)


def kernel(x, p, w_in, w_attn_out, w_ret_out, ret_gn_g, w_o, ln1_g, ln1_b, w_ffn_gate, w_ffn_up,
           w_ffn_down, w_ple_gate, w_ple_up, ln2_g, ln2_b):
    batch, seq, d_model = x.shape
    depth = w_in.shape[0]
    n = batch * seq
    alpha = (2 * depth) ** 0.25

    half = RET_QK_DIM // 2
    pos = jnp.arange(seq, dtype=F32)
    inv_freq = RET_ROPE_BASE ** (-jnp.arange(half, dtype=F32) / half)
    ang = pos[:, None] * inv_freq[None, :]
    cos, sin = jnp.cos(ang), jnp.sin(ang)

    h = x.reshape(n, d_model)
    for i in range(depth):
        bf = lambda w: w[i].astype(BF16)
        row = lambda v: v[i].reshape(1, -1)
        h = _layer(h, p[i].reshape(n, -1).astype(BF16),
                   w_in[i], w_attn_out[i], w_ret_out[i], row(ret_gn_g), w_o[i], row(ln1_g), row(ln1_b),
                   w_ffn_gate[i], w_ffn_up[i], w_ffn_down[i], w_ple_gate[i], bf(w_ple_up),
                   row(ln2_g), row(ln2_b), cos, sin, batch, seq, alpha)
    return h.reshape(batch, seq, d_model).astype(x.dtype)
```

```python
import functools

import jax
import jax.numpy as jnp
from jax import lax
from jax.experimental import pallas as pl
from jax.experimental.pallas import tpu as pltpu

ATT_HEAD_DIM = 128
ATT_HEADS = 8
ATT_DILATIONS = (1, 4, 16)
ATT_BLOCK = 128
ATT_GROUP_WIDTH = ATT_HEADS * ATT_HEAD_DIM
ATT_QKV_WIDTH = len(ATT_DILATIONS) * ATT_GROUP_WIDTH
RET_HEADS = 8
RET_QK_DIM = 256
RET_V_DIM = 512
RET_QK_WIDTH = RET_HEADS * RET_QK_DIM
RET_V_WIDTH = RET_HEADS * RET_V_DIM
RET_CHUNK = 128
RET_ROPE_BASE = 10000.0
LN_EPS = 1e-5
GN_EPS = 1e-6
NEG_INF = -1e30

V7X_VMEM_LIMIT_BYTES = 56 * 1024 * 1024
LANE = 128

BF16 = jnp.bfloat16
F32 = jnp.float32


def _params(*semantics):
    return pltpu.CompilerParams(dimension_semantics=semantics, vmem_limit_bytes=V7X_VMEM_LIMIT_BYTES)


def _tile(dim, pref):
    if dim <= pref:
        return dim
    t = (pref // LANE) * LANE
    while t > LANE and dim % t:
        t -= LANE
    assert dim % t == 0, (dim, pref)
    return t


def _sigmoid(v):
    return 1.0 / (1.0 + jnp.exp(-v))


def _mm(a, b):
    return lax.dot_general(a, b, (((1,), (0,)), ((), ())), preferred_element_type=F32)


ROW_SPLITS = 2


def _row_halves(n_rows, splits=ROW_SPLITS):
    rows = n_rows // splits
    return [slice(c * rows, (c + 1) * rows) for c in range(splits)]


BF16_SUBLANES = 16


COL_BLOCK = 1024


def _cast_job(w, col_blocks, steps, linear_step):
    rows, cols = w.shape
    ncb, width = (1, cols) if col_blocks is None else (len(col_blocks), COL_BLOCK)
    slab = -(-(-(-rows * ncb // steps)) // BF16_SUBLANES) * BF16_SUBLANES
    while rows % slab:
        slab += BF16_SUBLANES
    pieces = (rows // slab) * ncb
    assert pieces <= steps

    def piece(*idx):
        return jnp.minimum(linear_step(*idx), pieces - 1)

    def source_col(local):
        if col_blocks is None:
            return 0
        return sum(jnp.where(local == i, cb, 0) for i, cb in enumerate(col_blocks))

    in_spec = pl.BlockSpec((slab, width), lambda *idx: (piece(*idx) // ncb, source_col(piece(*idx) % ncb)))
    out_spec = pl.BlockSpec((slab, width), lambda *idx: (piece(*idx) // ncb, piece(*idx) % ncb))
    return in_spec, out_spec, jax.ShapeDtypeStruct((rows, ncb * width), BF16)


def _cast_jobs(cast, steps, linear_step):
    jobs = [_cast_job(w, cbs, steps, linear_step) for w, cbs in cast]
    return [j[0] for j in jobs], [j[1] for j in jobs], [j[2] for j in jobs]


def _run_casts(srcs, dsts):
    for src, dst in zip(srcs, dsts):
        dst[...] = src[...].astype(dst.dtype)


def _round_kernel(src_ref, dst_ref):
    dst_ref[...] = src_ref[...].astype(dst_ref.dtype)


def _round_blocks(w, col_blocks, slab=512):
    rows = w.shape[0]
    slab = min(slab, rows)
    assert rows % slab == 0
    steps = (rows // slab) * len(col_blocks)
    in_spec, out_spec, out_shape = _cast_job(w, col_blocks, steps, lambda t: t)
    return pl.pallas_call(
        _round_kernel, out_shape=out_shape, grid=(steps,), in_specs=[in_spec], out_specs=out_spec,
        compiler_params=_params("parallel"), name="round_first_weights",
    )(w)


def _proj_kernel(x_ref, w_ref, *refs):
    n_cast = (len(refs) - 1) // 2
    o_ref = refs[n_cast]
    o_ref[...] = _mm(x_ref[...], w_ref[...]).astype(o_ref.dtype)
    _run_casts(refs[:n_cast], refs[n_cast + 1:])


def _proj(x_bf, w_bf, col0, width, out_dtype, name, cast=(), rows_per_step=1024):
    n, k = x_bf.shape
    tm, tn = _tile(n, rows_per_step), _tile(width, 1024)
    assert col0 % tn == 0
    off = col0 // tn
    n_i = n // tm
    steps = (width // tn) * n_i
    cast_in, cast_out, cast_shapes = _cast_jobs(cast, steps, lambda j, i: j * n_i + i)
    outs = pl.pallas_call(
        _proj_kernel,
        out_shape=[jax.ShapeDtypeStruct((n, width), out_dtype)] + cast_shapes,
        grid=(width // tn, n_i),
        in_specs=[pl.BlockSpec((tm, k), lambda j, i: (i, 0)),
                  pl.BlockSpec((k, tn), lambda j, i: (0, off + j))] + cast_in,
        out_specs=[pl.BlockSpec((tm, tn), lambda j, i: (i, j))] + cast_out,
        compiler_params=_params("arbitrary", "arbitrary"),
        name=name,
    )(x_bf, w_bf, *[w for w, _ in cast])
    return outs if cast else outs[0]


def _proj_rot_kernel(x_ref, w_ref, cos_ref, sin_ref, *refs, n_q_blocks, k_scale):
    n_cast = (len(refs) - 1) // 2
    o_ref = refs[n_cast]
    _run_casts(refs[:n_cast], refs[n_cast + 1:])
    j = pl.program_id(0)
    acc = _mm(x_ref[...], w_ref[...])
    scale = jnp.where(j >= n_q_blocks, k_scale, 1.0).astype(F32)
    cos = cos_ref[...]
    sin = sin_ref[...]
    half = RET_QK_DIM // 2
    for h in range(acc.shape[1] // RET_QK_DIM):
        lo = h * RET_QK_DIM
        t1 = acc[:, lo:lo + half]
        t2 = acc[:, lo + half:lo + RET_QK_DIM]
        o_ref[:, lo:lo + half] = (t1 * cos - t2 * sin) * scale
        o_ref[:, lo + half:lo + RET_QK_DIM] = (t1 * sin + t2 * cos) * scale


def _proj_rot(x_bf, w_bf, col0, cos, sin, seq, cast=()):
    n, k = x_bf.shape
    width = 2 * RET_QK_WIDTH
    tm, tn = _tile(seq, 1024), 1024
    assert col0 % tn == 0 and RET_QK_WIDTH % tn == 0 and n % seq == 0
    off = col0 // tn
    pos_blocks = seq // tm
    n_i = n // tm
    kern = functools.partial(_proj_rot_kernel, n_q_blocks=RET_QK_WIDTH // tn, k_scale=RET_QK_DIM ** -0.5)
    half = RET_QK_DIM // 2
    cast_in, cast_out, cast_shapes = _cast_jobs(cast, (width // tn) * n_i, lambda j, i: j * n_i + i)
    outs = pl.pallas_call(
        kern,
        out_shape=[jax.ShapeDtypeStruct((n, width), F32)] + cast_shapes,
        grid=(width // tn, n_i),
        in_specs=[pl.BlockSpec((tm, k), lambda j, i: (i, 0)),
                  pl.BlockSpec((k, tn), lambda j, i: (0, off + j)),
                  pl.BlockSpec((tm, half), lambda j, i: (i % pos_blocks, 0)),
                  pl.BlockSpec((tm, half), lambda j, i: (i % pos_blocks, 0))] + cast_in,
        out_specs=[pl.BlockSpec((tm, tn), lambda j, i: (i, j))] + cast_out,
        compiler_params=_params("arbitrary", "arbitrary"),
        name="proj_rot",
    )(x_bf, w_bf, cos, sin, *[w for w, _ in cast])
    return outs if cast else outs[0]


ATT_SPAN_BLOCKS = 16
ATT_HEADS_PER_STEP = 4
ATT_HEADS_PER_COMBINE_STEP = 2


def _proj_units_kernel(x_ref, w_ref, o_ref, acc_ref, *, dilation):
    if dilation == 1:
        o_ref[0, 0] = _mm(x_ref[...], w_ref[...]).astype(o_ref.dtype)
        return
    w = w_ref[...]
    slabs = w.shape[1] // LANE
    for tok in _row_halves(x_ref.shape[0]):
        acc = _mm(x_ref[tok, :], w)
        for s in range(slabs):
            acc_ref[s, tok, :] = acc[:, s * LANE:(s + 1) * LANE]
        rows = acc.shape[0] // dilation
        dst = slice(tok.start // dilation, tok.start // dilation + rows)
        for r in range(dilation):
            for s in range(slabs):
                piece = acc_ref[s, pl.ds(tok.start + r, rows, stride=dilation), :]
                o_ref[r, 0, dst, s * LANE:(s + 1) * LANE] = piece.astype(o_ref.dtype)


def _proj_units(x_bf, w_bf, first_block, dilation, batch, seq):
    n, k = x_bf.shape
    tm, tn = _tile(seq, 1024), ATT_GROUP_WIDTH
    assert (tm // ROW_SPLITS) % (dilation * BF16_SUBLANES) == 0 and seq % tm == 0
    rows = tm // dilation
    tiles = seq // tm
    kern = functools.partial(_proj_units_kernel, dilation=dilation)
    out = pl.pallas_call(
        kern,
        out_shape=jax.ShapeDtypeStruct((3, batch * dilation, tiles, rows, tn), BF16),
        grid=(3, n // tm),
        in_specs=[pl.BlockSpec((tm, k), lambda j, i: (i, 0)),
                  pl.BlockSpec((k, tn), lambda j, i: (0, first_block + j))],
        out_specs=pl.BlockSpec((None, dilation, 1, rows, tn), lambda j, i: (j, i // tiles, i % tiles, 0, 0)),
        scratch_shapes=[pltpu.VMEM((tn // LANE, tm, LANE), F32)],
        compiler_params=_params("parallel", "parallel"),
        name=f"proj_att_d{dilation}",
    )(x_bf, w_bf)
    n_blk = seq // (dilation * ATT_BLOCK)
    return out.reshape(3, batch * dilation, n_blk, ATT_BLOCK, tn)


def _proj_units_f32_kernel(x_ref, w_ref, *refs):
    n_cast = (len(refs) - 2) // 2
    o_ref, xb_ref = refs[n_cast], refs[n_cast + 1]
    xb = x_ref[...].astype(xb_ref.dtype)
    xb_ref[...] = xb
    tn = o_ref.shape[-1]
    for part in range(3):
        o_ref[part, 0, 0] = _mm(xb, w_ref[:, part * tn:(part + 1) * tn]).astype(o_ref.dtype)
    _run_casts(refs[:n_cast], refs[n_cast + 2:])


def _proj_units_from_f32(x_f32, w_qkv, batch, seq, cast=()):
    n, k = x_f32.shape
    tm, tn = _tile(seq, 256), ATT_GROUP_WIDTH
    assert w_qkv.shape == (k, 3 * tn)
    tiles = seq // tm
    cast_in, cast_out, cast_shapes = _cast_jobs(cast, n // tm, lambda i: i)
    outs = pl.pallas_call(
        _proj_units_f32_kernel,
        out_shape=[jax.ShapeDtypeStruct((3, batch, tiles, tm, tn), BF16),
                   jax.ShapeDtypeStruct((n, k), BF16)] + cast_shapes,
        grid=(n // tm,),
        in_specs=[pl.BlockSpec((tm, k), lambda i: (i, 0)),
                  pl.BlockSpec((k, 3 * tn), lambda i: (0, 0), pipeline_mode=pl.Buffered(1))] + cast_in,
        out_specs=[pl.BlockSpec((3, 1, 1, tm, tn), lambda i: (0, i // tiles, i % tiles, 0, 0)),
                   pl.BlockSpec((tm, k), lambda i: (i, 0))] + cast_out,
        compiler_params=_params("arbitrary"),
        name="proj_att_d1",
    )(x_f32, w_qkv, *[w for w, _ in cast])
    return [outs[0].reshape(3, batch, seq // ATT_BLOCK, ATT_BLOCK, tn)] + list(outs[1:])


def _attn_span(q_ref, k_ref, kp_ref, v_ref, vp_ref, o_ref, lse_ref, *, dilation, nb):
    qi = lax.broadcasted_iota(jnp.int32, (ATT_BLOCK, 2 * ATT_BLOCK), 0)
    kj = lax.broadcasted_iota(jnp.int32, (ATT_BLOCK, 2 * ATT_BLOCK), 1)
    in_window = kj <= qi + ATT_BLOCK
    valid_inner = jnp.logical_and(kj >= qi, in_window)
    lo = jnp.where(pl.program_id(1) > 0, 0, ATT_BLOCK)
    valid_first = jnp.logical_and(kj >= jnp.maximum(qi, lo), in_window)
    scale = ATT_HEAD_DIM ** -0.5
    dn = (((1,), (1,)), ((), ()))
    ones = jnp.ones((2 * ATT_BLOCK, ATT_HEAD_DIM), BF16)
    units = [(r, i, h) for r in range(dilation) for i in range(nb) for h in range(o_ref.shape[0])]

    def cols(h):
        return slice(h * ATT_HEAD_DIM, (h + 1) * ATT_HEAD_DIM)

    def with_prev(cur_ref, prev_ref, r, i, h):
        prev = prev_ref[r, 0, :, cols(h)] if i == 0 else cur_ref[r, i - 1, :, cols(h)]
        return jnp.concatenate([prev, cur_ref[r, i, :, cols(h)]], axis=0)

    scores = [lax.dot_general(q_ref[r, i, :, cols(h)], with_prev(k_ref, kp_ref, r, i, h), dn,
                              preferred_element_type=F32) * scale for r, i, h in units]
    probs, maxes = [], []
    for (r, i, h), s in zip(units, scores):
        s = jnp.where(valid_first if i == 0 else valid_inner, s, NEG_INF)
        m = jnp.max(s, axis=-1, keepdims=True)
        probs.append(jnp.exp(s - m).astype(BF16))
        maxes.append(m)
    for (r, i, h), p, m in zip(units, probs, maxes):
        v_aug = jnp.concatenate([with_prev(v_ref, vp_ref, r, i, h), ones], axis=1)
        acc = _mm(p, v_aug)
        den = acc[:, ATT_HEAD_DIM:]
        if dilation == 1:
            rows = pl.ds(i * ATT_BLOCK, ATT_BLOCK)
        else:
            rows = pl.ds(i * ATT_BLOCK * dilation + r, ATT_BLOCK, stride=dilation)
        o_ref[h, rows, :] = acc[:, :ATT_HEAD_DIM] / den
        lse_ref[h, rows, :] = m + jnp.log(den)


def _attn_kernel(q_ref, k_ref, kp_ref, v_ref, vp_ref, o_ref, lse_ref, *, dilation, nb):
    _attn_span(q_ref, k_ref, kp_ref, v_ref, vp_ref, o_ref, lse_ref, dilation=dilation, nb=nb)


ATT_COMBINE_ROWS = 256


def _attn_combine_kernel(q_ref, k_ref, kp_ref, v_ref, vp_ref, o1, l1, o2, l2, out_ref, o3, l3, *, dilation, nb):
    _attn_span(q_ref, k_ref, kp_ref, v_ref, vp_ref, o3, l3, dilation=dilation, nb=nb)

    def body(c, carry):
        rows = pl.ds(pl.multiple_of(c * ATT_COMBINE_ROWS, ATT_COMBINE_ROWS), ATT_COMBINE_ROWS)
        for h in range(o3.shape[0]):
            a, b, d = l1[h, rows, :], l2[h, rows, :], l3[h, rows, :]
            m = jnp.maximum(jnp.maximum(a, b), d)
            ea, eb, ed = jnp.exp(a - m), jnp.exp(b - m), jnp.exp(d - m)
            s = ea + eb + ed
            out = (ea / s) * o1[h, rows, :] + (eb / s) * o2[h, rows, :] + (ed / s) * o3[h, rows, :]
            out_ref[rows, h * ATT_HEAD_DIM:(h + 1) * ATT_HEAD_DIM] = out.astype(out_ref.dtype)
        return carry

    lax.fori_loop(0, out_ref.shape[0] // ATT_COMBINE_ROWS, body, 0)


def _attention(qkv, dilation, batch, seq, others=None):
    n_blk = qkv.shape[2]
    assert ATT_SPAN_BLOCKS % dilation == 0
    nb = ATT_SPAN_BLOCKS // dilation
    assert n_blk % nb == 0
    spans = n_blk // nb
    span_tokens = ATT_SPAN_BLOCKS * ATT_BLOCK
    heads = ATT_HEADS_PER_STEP if others is None else ATT_HEADS_PER_COMBINE_STEP
    hw = heads * ATT_HEAD_DIM
    cur = (None, dilation, nb, ATT_BLOCK, hw)
    one = (None, dilation, 1, ATT_BLOCK, hw)

    def at(part):
        return lambda b, s, hp: (part, b, s, 0, hp)

    def before(part):
        return lambda b, s, hp: (part, b, jnp.maximum(s * nb - 1, 0), 0, hp)

    qkv_specs = [pl.BlockSpec(cur, at(0)), pl.BlockSpec(cur, at(1)), pl.BlockSpec(one, before(1)),
                 pl.BlockSpec(cur, at(2)), pl.BlockSpec(one, before(2))]
    per_head = jax.ShapeDtypeStruct((ATT_HEADS, batch * seq, ATT_HEAD_DIM), F32)
    head_blk = (heads, span_tokens, ATT_HEAD_DIM)
    head_spec = pl.BlockSpec(head_blk, lambda b, s, hp: (hp, b * spans + s, 0))
    grid = (batch, spans, ATT_HEADS // heads)
    if others is None:
        return pl.pallas_call(
            functools.partial(_attn_kernel, dilation=dilation, nb=nb),
            out_shape=(per_head, per_head),
            grid=grid,
            in_specs=qkv_specs,
            out_specs=(head_spec, head_spec),
            compiler_params=_params("parallel", "parallel", "parallel"),
            name=f"attn_d{dilation}",
        )(qkv, qkv, qkv, qkv, qkv)
    return pl.pallas_call(
        functools.partial(_attn_combine_kernel, dilation=dilation, nb=nb),
        out_shape=jax.ShapeDtypeStruct((batch * seq, ATT_GROUP_WIDTH), BF16),
        grid=grid,
        in_specs=qkv_specs + [head_spec] * 4,
        out_specs=pl.BlockSpec((span_tokens, hw), lambda b, s, hp: (b * spans + s, hp)),
        scratch_shapes=[pltpu.VMEM(head_blk, F32), pltpu.VMEM(head_blk, F32)],
        compiler_params=_params("parallel", "parallel", "parallel"),
        name=f"attn_d{dilation}_combine",
    )(qkv, qkv, qkv, qkv, qkv, *others)


RET_CHUNKS_PER_STEP = 16


def _ret_kernel(q_ref, k_ref, v_ref, g_ref, gn_ref, intra_ref, cross_ref, sdec_ref, cdec_ref,
                z_ref, state_ref, *, cb):
    @pl.when(pl.program_id(2) == 0)
    def _():
        state_ref[...] = jnp.zeros_like(state_ref)

    c = RET_CHUNK
    intra, cross_decay, state_decay, chunk_decay = intra_ref[...], cross_ref[...], sdec_ref[...], cdec_ref[...]
    gn = gn_ref[...]
    qs, inners, updates = [], [], []
    for t in range(cb):
        rows = slice(t * c, (t + 1) * c)
        q = q_ref[rows, :].astype(BF16)
        k = k_ref[rows, :]
        v = v_ref[rows, :]
        att = lax.dot_general(q, k.astype(BF16), (((1,), (1,)), ((), ())), preferred_element_type=F32) * intra
        inners.append(_mm(att.astype(BF16), v))
        kd = (k * state_decay).astype(BF16)
        updates.append(lax.dot_general(kd, v, (((0,), (0,)), ((), ())), preferred_element_type=F32))
        qs.append(q)
    state = state_ref[...]
    for t in range(cb):
        rows = slice(t * c, (t + 1) * c)
        r = inners[t] + _mm(qs[t], state.astype(BF16)) * cross_decay
        state = chunk_decay * state + updates[t]
        mu = jnp.mean(r, axis=-1, keepdims=True)
        var = jnp.mean(jnp.square(r - mu), axis=-1, keepdims=True)
        rn = (r - mu) * lax.rsqrt(var + GN_EPS) * gn
        g = g_ref[rows, :]
        z_ref[rows, :] = ((g * _sigmoid(g)) * rn).astype(z_ref.dtype)
    state_ref[...] = state


def _retention(qk, v, gates, gn_g, batch, seq):
    n = qk.shape[0]
    c = RET_CHUNK
    nc = seq // c
    cb = min(RET_CHUNKS_PER_STEP, nc)
    assert nc % cb == 0
    steps = nc // cb
    hh = jnp.arange(RET_HEADS, dtype=F32)
    log_gamma = jnp.log(1.0 - 2.0 ** (-5.0 - hh))
    idx = jnp.arange(c, dtype=F32)
    diff = idx[:, None] - idx[None, :]
    intra = jnp.where(diff >= 0, jnp.exp(jnp.maximum(diff, 0.0)[None] * log_gamma[:, None, None]), 0.0)
    cross_decay = jnp.exp((idx + 1.0)[None, :] * log_gamma[:, None])[..., None]
    state_decay = jnp.exp((c - 1.0 - idx)[None, :] * log_gamma[:, None])[..., None]
    chunk_decay = jnp.exp(c * log_gamma)[:, None, None]

    def rows(b, h, s):
        return b * steps + s

    kern = functools.partial(_ret_kernel, cb=cb)
    return pl.pallas_call(
        kern,
        out_shape=jax.ShapeDtypeStruct((n, RET_V_WIDTH), BF16),
        grid=(batch, RET_HEADS, steps),
        in_specs=[
            pl.BlockSpec((cb * c, RET_QK_DIM), lambda b, h, s: (rows(b, h, s), h)),
            pl.BlockSpec((cb * c, RET_QK_DIM), lambda b, h, s: (rows(b, h, s), RET_HEADS + h)),
            pl.BlockSpec((cb * c, RET_V_DIM), lambda b, h, s: (rows(b, h, s), h)),
            pl.BlockSpec((cb * c, RET_V_DIM), lambda b, h, s: (rows(b, h, s), h)),
            pl.BlockSpec((1, RET_V_DIM), lambda b, h, s: (0, h)),
            pl.BlockSpec((None, c, c), lambda b, h, s: (h, 0, 0)),
            pl.BlockSpec((None, c, 1), lambda b, h, s: (h, 0, 0)),
            pl.BlockSpec((None, c, 1), lambda b, h, s: (h, 0, 0)),
            pl.BlockSpec((None, 1, 1), lambda b, h, s: (h, 0, 0)),
        ],
        out_specs=pl.BlockSpec((cb * c, RET_V_DIM), lambda b, h, s: (rows(b, h, s), h)),
        scratch_shapes=[pltpu.VMEM((RET_QK_DIM, RET_V_DIM), F32)],
        compiler_params=_params("parallel", "parallel", "arbitrary"),
        name="retention",
    )(qk, qk, v, gates, gn_g, intra, cross_decay, state_decay, chunk_decay)


def _mix_kernel(oa_ref, z_ref, wa_ref, wr_ref, ga_ref, gr_ref, o_ref):
    ya = _mm(oa_ref[...], wa_ref[...])
    yr = _mm(z_ref[...], wr_ref[...])
    o_ref[...] = (_sigmoid(ga_ref[...]) * ya + _sigmoid(gr_ref[...]) * yr).astype(o_ref.dtype)


def _mix(o_att, z, wa, wr, gates, gate_col0):
    n = o_att.shape[0]
    d = wa.shape[1]
    tm, tn = _tile(n, 512), _tile(d, 1024)
    assert gate_col0 % tn == 0
    ga_off = gate_col0 // tn
    gr_off = ga_off + d // tn
    return pl.pallas_call(
        _mix_kernel,
        out_shape=jax.ShapeDtypeStruct((n, d), BF16),
        grid=(d // tn, n // tm),
        in_specs=[pl.BlockSpec((tm, o_att.shape[1]), lambda j, i: (i, 0)),
                  pl.BlockSpec((tm, z.shape[1]), lambda j, i: (i, 0)),
                  pl.BlockSpec((wa.shape[0], tn), lambda j, i: (0, j)),
                  pl.BlockSpec((wr.shape[0], tn), lambda j, i: (0, j)),
                  pl.BlockSpec((tm, tn), lambda j, i: (i, ga_off + j)),
                  pl.BlockSpec((tm, tn), lambda j, i: (i, gr_off + j))],
        out_specs=pl.BlockSpec((tm, tn), lambda j, i: (i, j)),
        compiler_params=_params("parallel", "parallel"),
        name="mix",
    )(o_att, z, wa, wr, gates, gates)


def _layer_norm_rows(t, g, b):
    mu = jnp.mean(t, axis=-1, keepdims=True)
    var = jnp.mean(jnp.square(t - mu), axis=-1, keepdims=True)
    return (t - mu) * lax.rsqrt(var + LN_EPS) * g + b


LN_ROW_CHUNK = 256


def _layer_norm_inplace(o_ref, g_ref, b_ref, cast_ref=None):
    g, b = g_ref[...], b_ref[...]
    chunk = min(LN_ROW_CHUNK, o_ref.shape[0])

    def body(r, carry):
        rows = pl.ds(pl.multiple_of(r * chunk, chunk), chunk)
        y = _layer_norm_rows(o_ref[rows, :], g, b)
        o_ref[rows, :] = y
        if cast_ref is not None:
            cast_ref[rows, :] = y.astype(cast_ref.dtype)
        return carry

    lax.fori_loop(0, o_ref.shape[0] // chunk, body, 0)


def _proj_ln_kernel(a_ref, w_ref, res_ref, g_ref, b_ref, of_ref, ob_ref, *, alpha):
    of_ref[...] = alpha * res_ref[...] + _mm(a_ref[...], w_ref[...])
    _layer_norm_inplace(of_ref, g_ref, b_ref, ob_ref)


def _proj_ln(a, w, res, g, b, alpha):
    n, k = a.shape
    d = w.shape[1]
    tm = _tile(n, 128)
    kern = functools.partial(_proj_ln_kernel, alpha=alpha)
    row = pl.BlockSpec((tm, d), lambda i: (i, 0))
    vec = pl.BlockSpec((1, d), lambda i: (0, 0))
    return pl.pallas_call(
        kern,
        out_shape=(jax.ShapeDtypeStruct((n, d), F32), jax.ShapeDtypeStruct((n, d), BF16)),
        grid=(n // tm,),
        in_specs=[pl.BlockSpec((tm, k), lambda i: (i, 0)),
                  pl.BlockSpec((k, d), lambda i: (0, 0), pipeline_mode=pl.Buffered(1)),
                  row, vec, vec],
        out_specs=(row, row),
        compiler_params=_params("parallel"),
        name="proj_ln",
    )(a, w, res, g, b)


SWIGLU_ROW_SPLITS = 4


def _swiglu_kernel(h_ref, wg_ref, wu_ref, o_ref):
    wg, wu = wg_ref[...], wu_ref[...]
    for rows in _row_halves(o_ref.shape[0], SWIGLU_ROW_SPLITS):
        h = h_ref[rows, :]
        gate = _mm(h, wg)
        up = _mm(h, wu)
        o_ref[rows, :] = ((gate * _sigmoid(gate)) * up).astype(o_ref.dtype)


def _swiglu(h_bf, wg, wu):
    n, k = h_bf.shape
    f = wg.shape[1]
    tm, tn = _tile(n, 2048), _tile(f, 256)
    w_spec = pl.BlockSpec((k, tn), lambda i, j: (0, j))
    return pl.pallas_call(
        _swiglu_kernel,
        out_shape=jax.ShapeDtypeStruct((n, f), BF16),
        grid=(n // tm, f // tn),
        in_specs=[pl.BlockSpec((tm, k), lambda i, j: (i, 0)), w_spec, w_spec],
        out_specs=pl.BlockSpec((tm, tn), lambda i, j: (i, j)),
        compiler_params=_params("parallel", "parallel"),
        name="swiglu",
    )(h_bf, wg, wu)


def _ple_kernel(hb_ref, wg_ref, p_ref, wu_ref, hf_ref, o_ref, *, alpha):
    gate = _sigmoid(_mm(hb_ref[...], wg_ref[...]))
    up = _mm(p_ref[...], wu_ref[...])
    o_ref[...] = alpha * hf_ref[...] + gate * up


def _ple_base(h_bf, h_f32, p_bf, wpg, wpu, alpha):
    n, k = h_bf.shape
    d = wpg.shape[1]
    tm, tn = _tile(n, 512), _tile(d, 1024)
    kern = functools.partial(_ple_kernel, alpha=alpha)
    return pl.pallas_call(
        kern,
        out_shape=jax.ShapeDtypeStruct((n, d), F32),
        grid=(d // tn, n // tm),
        in_specs=[pl.BlockSpec((tm, k), lambda j, i: (i, 0)),
                  pl.BlockSpec((k, tn), lambda j, i: (0, j)),
                  pl.BlockSpec((tm, p_bf.shape[1]), lambda j, i: (i, 0)),
                  pl.BlockSpec((p_bf.shape[1], tn), lambda j, i: (0, j)),
                  pl.BlockSpec((tm, tn), lambda j, i: (i, j))],
        out_specs=pl.BlockSpec((tm, tn), lambda j, i: (i, j)),
        compiler_params=_params("parallel", "parallel"),
        name="ple_base",
    )(h_bf, wpg, p_bf, wpu, h_f32)


def _down_ln_kernel(a_ref, w_ref, base_ref, g_ref, b_ref, o_ref, *, tn):
    j = pl.program_id(1)
    col = pl.multiple_of(j * tn, tn)
    o_ref[:, pl.ds(col, tn)] = base_ref[...] + _mm(a_ref[...], w_ref[...])

    @pl.when(j == pl.num_programs(1) - 1)
    def _():
        _layer_norm_inplace(o_ref, g_ref, b_ref)


def _down_ln(hidden, wd, base, g, b):
    n, f = hidden.shape
    d = wd.shape[1]
    tm, tn = _tile(n, 512), _tile(d, 256)
    kern = functools.partial(_down_ln_kernel, tn=tn)
    vec = pl.BlockSpec((1, d), lambda i, j: (0, 0))
    return pl.pallas_call(
        kern,
        out_shape=jax.ShapeDtypeStruct((n, d), F32),
        grid=(n // tm, d // tn),
        in_specs=[pl.BlockSpec((tm, f), lambda i, j: (i, 0)),
                  pl.BlockSpec((f, tn), lambda i, j: (0, j)),
                  pl.BlockSpec((tm, tn), lambda i, j: (i, j)),
                  vec, vec],
        out_specs=pl.BlockSpec((tm, d), lambda i, j: (i, 0)),
        compiler_params=_params("parallel", "arbitrary"),
        name="down_ln",
    )(hidden, wd, base, g, b)


def _layer(h_f32, p_bf, w_in, w_attn_out, w_ret_out, ret_gn_g, w_o, ln1_g, ln1_b,
           w_ffn_gate, w_ffn_up, w_ffn_down, w_ple_gate, w_ple_up, ln2_g, ln2_b,
           cos, sin, batch, seq, alpha):
    d_model = h_f32.shape[1]
    c_qr = 3 * ATT_QKV_WIDTH
    c_vr = c_qr + 2 * RET_QK_WIDTH
    c_gr = c_vr + RET_V_WIDTH
    c_end = c_gr + RET_V_WIDTH + 2 * d_model
    assert w_in.shape[1] == c_end and c_end % COL_BLOCK == 0
    assert ATT_DILATIONS[0] == 1 and ATT_GROUP_WIDTH == COL_BLOCK
    n_groups = len(ATT_DILATIONS)

    def att_blocks(g):
        return [part * n_groups + g for part in range(3)]

    def blocks(c0, c1):
        return list(range(c0 // COL_BLOCK, c1 // COL_BLOCK))

    w_first = _round_blocks(w_in, att_blocks(0))
    qkv_first, h_bf, w_rot, w_o, w_ple_gate = _proj_units_from_f32(
        h_f32, w_first, batch, seq, cast=((w_in, blocks(c_qr, c_vr)), (w_o, None), (w_ple_gate, None)))
    qk_r, w_vr, w_attn_out = _proj_rot(
        h_bf, w_rot, 0, cos, sin, seq, cast=((w_in, blocks(c_vr, c_gr)), (w_attn_out, None)))
    v_r, w_gates, w_ret_out = _proj(h_bf, w_vr, 0, RET_V_WIDTH, BF16, "proj_vr", rows_per_step=512,
                                    cast=((w_in, blocks(c_gr, c_end)), (w_ret_out, None)))
    later_groups = list(range(1, n_groups))
    gates, w_att_later, w_ffn_down = _proj(
        h_bf, w_gates, 0, RET_V_WIDTH + 2 * d_model, F32, "proj_gates",
        cast=((w_in, sum((att_blocks(g) for g in later_groups), [])), (w_ffn_down, None)))

    others = list(_attention(qkv_first, 1, batch, seq))
    for pos, g in enumerate(later_groups):
        qkv = _proj_units(h_bf, w_att_later, 3 * pos, ATT_DILATIONS[g], batch, seq)
        if g < n_groups - 1:
            others.extend(_attention(qkv, ATT_DILATIONS[g], batch, seq))
        else:
            o_att = _attention(qkv, ATT_DILATIONS[g], batch, seq, others=others)

    z = _retention(qk_r, v_r, gates, ret_gn_g, batch, seq)

    mixed = _mix(o_att, z, w_attn_out, w_ret_out, gates, RET_V_WIDTH)
    h1_f32, h1_bf = _proj_ln(mixed, w_o, h_f32, ln1_g, ln1_b, alpha)

    hidden = _swiglu(h1_bf, w_ffn_gate, w_ffn_up)
    base = _ple_base(h1_bf, h1_f32, p_bf, w_ple_gate, w_ple_up, alpha)
    return _down_ln(hidden, w_ffn_down, base, ln2_g, ln2_b)


def kernel(x, p, w_in, w_attn_out, w_ret_out, ret_gn_g, w_o, ln1_g, ln1_b, w_ffn_gate, w_ffn_up,
           w_ffn_down, w_ple_gate, w_ple_up, ln2_g, ln2_b):
    batch, seq, d_model = x.shape
    depth = w_in.shape[0]
    n = batch * seq
    alpha = (2 * depth) ** 0.25

    half = RET_QK_DIM // 2
    pos = jnp.arange(seq, dtype=F32)
    inv_freq = RET_ROPE_BASE ** (-jnp.arange(half, dtype=F32) / half)
    ang = pos[:, None] * inv_freq[None, :]
    cos, sin = jnp.cos(ang), jnp.sin(ang)

    h = x.reshape(n, d_model)
    for i in range(depth):
        bf = lambda w: w[i].astype(BF16)
        row = lambda v: v[i].reshape(1, -1)
        h = _layer(h, p[i].reshape(n, -1).astype(BF16),
                   w_in[i], w_attn_out[i], w_ret_out[i], row(ret_gn_g), w_o[i], row(ln1_g), row(ln1_b),
                   w_ffn_gate[i], w_ffn_up[i], w_ffn_down[i], w_ple_gate[i], bf(w_ple_up),
                   row(ln2_g), row(ln2_b), cos, sin, batch, seq, alpha)
    return h.reshape(batch, seq, d_model).astype(x.dtype)
```

```python
import functools

import jax
import jax.numpy as jnp
from jax import lax
from jax.experimental import pallas as pl
from jax.experimental.pallas import tpu as pltpu

ATT_HEAD_DIM = 128
ATT_HEADS = 8
ATT_DILATIONS = (1, 4, 16)
ATT_BLOCK = 128
ATT_GROUP_WIDTH = ATT_HEADS * ATT_HEAD_DIM
ATT_QKV_WIDTH = len(ATT_DILATIONS) * ATT_GROUP_WIDTH
RET_HEADS = 8
RET_QK_DIM = 256
RET_V_DIM = 512
RET_QK_WIDTH = RET_HEADS * RET_QK_DIM
RET_V_WIDTH = RET_HEADS * RET_V_DIM
RET_CHUNK = 128
RET_ROPE_BASE = 10000.0
LN_EPS = 1e-5
GN_EPS = 1e-6
NEG_INF = -1e30

V7X_VMEM_LIMIT_BYTES = 56 * 1024 * 1024
LANE = 128

BF16 = jnp.bfloat16
F32 = jnp.float32


def _params(*semantics):
    return pltpu.CompilerParams(dimension_semantics=semantics, vmem_limit_bytes=V7X_VMEM_LIMIT_BYTES)


def _tile(dim, pref):
    if dim <= pref:
        return dim
    t = (pref // LANE) * LANE
    while t > LANE and dim % t:
        t -= LANE
    assert dim % t == 0, (dim, pref)
    return t


def _sigmoid(v):
    return 1.0 / (1.0 + jnp.exp(-v))


def _mm(a, b):
    return lax.dot_general(a, b, (((1,), (0,)), ((), ())), preferred_element_type=F32)


ROW_SPLITS = 2


def _row_halves(n_rows, splits=ROW_SPLITS):
    rows = n_rows // splits
    return [slice(c * rows, (c + 1) * rows) for c in range(splits)]


BF16_SUBLANES = 16


COL_BLOCK = 1024


def _cast_job(w, col_blocks, steps, linear_step):
    rows, cols = w.shape
    ncb, width = (1, cols) if col_blocks is None else (len(col_blocks), COL_BLOCK)
    slab = -(-(-(-rows * ncb // steps)) // BF16_SUBLANES) * BF16_SUBLANES
    while rows % slab:
        slab += BF16_SUBLANES
    pieces = (rows // slab) * ncb
    assert pieces <= steps

    def piece(*idx):
        return jnp.minimum(linear_step(*idx), pieces - 1)

    def source_col(local):
        if col_blocks is None:
            return 0
        return sum(jnp.where(local == i, cb, 0) for i, cb in enumerate(col_blocks))

    in_spec = pl.BlockSpec((slab, width), lambda *idx: (piece(*idx) // ncb, source_col(piece(*idx) % ncb)))
    out_spec = pl.BlockSpec((slab, width), lambda *idx: (piece(*idx) // ncb, piece(*idx) % ncb))
    return in_spec, out_spec, jax.ShapeDtypeStruct((rows, ncb * width), BF16)


def _cast_jobs(cast, steps, linear_step):
    jobs = [_cast_job(w, cbs, steps, linear_step) for w, cbs in cast]
    return [j[0] for j in jobs], [j[1] for j in jobs], [j[2] for j in jobs]


def _run_casts(srcs, dsts):
    for src, dst in zip(srcs, dsts):
        dst[...] = src[...].astype(dst.dtype)


def _round_kernel(src_ref, dst_ref):
    dst_ref[...] = src_ref[...].astype(dst_ref.dtype)


def _round_blocks(w, col_blocks, slab=512):
    rows = w.shape[0]
    slab = min(slab, rows)
    assert rows % slab == 0
    steps = (rows // slab) * len(col_blocks)
    in_spec, out_spec, out_shape = _cast_job(w, col_blocks, steps, lambda t: t)
    return pl.pallas_call(
        _round_kernel, out_shape=out_shape, grid=(steps,), in_specs=[in_spec], out_specs=out_spec,
        compiler_params=_params("parallel"), name="round_first_weights",
    )(w)


def _proj_kernel(x_ref, w_ref, *refs, silu_blocks):
    n_cast = (len(refs) - 1) // 2
    o_ref = refs[n_cast]
    if silu_blocks is None:
        o_ref[...] = _mm(x_ref[...], w_ref[...]).astype(o_ref.dtype)
    else:
        j = pl.program_id(0)

        @pl.when(j < silu_blocks)
        def _():
            acc = _mm(x_ref[...], w_ref[...])
            o_ref[...] = (acc * _sigmoid(acc)).astype(o_ref.dtype)

        @pl.when(j >= silu_blocks)
        def _():
            o_ref[...] = _mm(x_ref[...], w_ref[...]).astype(o_ref.dtype)

    _run_casts(refs[:n_cast], refs[n_cast + 1:])


def _proj(x_bf, w_bf, col0, width, out_dtype, name, cast=(), rows_per_step=1024, silu_width=None):
    n, k = x_bf.shape
    tm, tn = _tile(n, rows_per_step), _tile(width, 1024)
    assert col0 % tn == 0 and (silu_width is None or silu_width % tn == 0)
    off = col0 // tn
    n_i = n // tm
    steps = (width // tn) * n_i
    cast_in, cast_out, cast_shapes = _cast_jobs(cast, steps, lambda j, i: j * n_i + i)
    kern = functools.partial(_proj_kernel, silu_blocks=None if silu_width is None else silu_width // tn)
    outs = pl.pallas_call(
        kern,
        out_shape=[jax.ShapeDtypeStruct((n, width), out_dtype)] + cast_shapes,
        grid=(width // tn, n_i),
        in_specs=[pl.BlockSpec((tm, k), lambda j, i: (i, 0)),
                  pl.BlockSpec((k, tn), lambda j, i: (0, off + j))] + cast_in,
        out_specs=[pl.BlockSpec((tm, tn), lambda j, i: (i, j))] + cast_out,
        compiler_params=_params("arbitrary", "arbitrary"),
        name=name,
    )(x_bf, w_bf, *[w for w, _ in cast])
    return outs if cast else outs[0]


def _proj_rot_kernel(x_ref, w_ref, cos_ref, sin_ref, *refs, n_q_blocks, k_scale):
    n_cast = (len(refs) - 1) // 2
    o_ref = refs[n_cast]
    _run_casts(refs[:n_cast], refs[n_cast + 1:])
    j = pl.program_id(0)
    acc = _mm(x_ref[...], w_ref[...])
    scale = jnp.where(j >= n_q_blocks, k_scale, 1.0).astype(F32)
    cos = cos_ref[...]
    sin = sin_ref[...]
    half = RET_QK_DIM // 2
    for h in range(acc.shape[1] // RET_QK_DIM):
        lo = h * RET_QK_DIM
        t1 = acc[:, lo:lo + half]
        t2 = acc[:, lo + half:lo + RET_QK_DIM]
        o_ref[:, lo:lo + half] = (t1 * cos - t2 * sin) * scale
        o_ref[:, lo + half:lo + RET_QK_DIM] = (t1 * sin + t2 * cos) * scale


def _proj_rot(x_bf, w_bf, col0, cos, sin, seq, cast=()):
    n, k = x_bf.shape
    width = 2 * RET_QK_WIDTH
    tm, tn = _tile(seq, 1024), 1024
    assert col0 % tn == 0 and RET_QK_WIDTH % tn == 0 and n % seq == 0
    off = col0 // tn
    pos_blocks = seq // tm
    n_i = n // tm
    kern = functools.partial(_proj_rot_kernel, n_q_blocks=RET_QK_WIDTH // tn, k_scale=RET_QK_DIM ** -0.5)
    half = RET_QK_DIM // 2
    cast_in, cast_out, cast_shapes = _cast_jobs(cast, (width // tn) * n_i, lambda j, i: j * n_i + i)
    outs = pl.pallas_call(
        kern,
        out_shape=[jax.ShapeDtypeStruct((n, width), F32)] + cast_shapes,
        grid=(width // tn, n_i),
        in_specs=[pl.BlockSpec((tm, k), lambda j, i: (i, 0)),
                  pl.BlockSpec((k, tn), lambda j, i: (0, off + j)),
                  pl.BlockSpec((tm, half), lambda j, i: (i % pos_blocks, 0)),
                  pl.BlockSpec((tm, half), lambda j, i: (i % pos_blocks, 0))] + cast_in,
        out_specs=[pl.BlockSpec((tm, tn), lambda j, i: (i, j))] + cast_out,
        compiler_params=_params("arbitrary", "arbitrary"),
        name="proj_rot",
    )(x_bf, w_bf, cos, sin, *[w for w, _ in cast])
    return outs if cast else outs[0]


ATT_SPAN_BLOCKS = 16
ATT_HEADS_PER_STEP = 4
ATT_HEADS_PER_COMBINE_STEP = 2


def _proj_units_kernel(x_ref, w_ref, o_ref, acc_ref, *, dilation):
    if dilation == 1:
        o_ref[0, 0] = _mm(x_ref[...], w_ref[...]).astype(o_ref.dtype)
        return
    w = w_ref[...]
    slabs = w.shape[1] // LANE
    for tok in _row_halves(x_ref.shape[0]):
        acc = _mm(x_ref[tok, :], w)
        for s in range(slabs):
            acc_ref[s, tok, :] = acc[:, s * LANE:(s + 1) * LANE]
        rows = acc.shape[0] // dilation
        dst = slice(tok.start // dilation, tok.start // dilation + rows)
        for r in range(dilation):
            for s in range(slabs):
                piece = acc_ref[s, pl.ds(tok.start + r, rows, stride=dilation), :]
                o_ref[r, 0, dst, s * LANE:(s + 1) * LANE] = piece.astype(o_ref.dtype)


def _proj_units(x_bf, w_bf, first_block, dilation, batch, seq):
    n, k = x_bf.shape
    tm, tn = _tile(seq, 1024), ATT_GROUP_WIDTH
    assert (tm // ROW_SPLITS) % (dilation * BF16_SUBLANES) == 0 and seq % tm == 0
    rows = tm // dilation
    tiles = seq // tm
    kern = functools.partial(_proj_units_kernel, dilation=dilation)
    out = pl.pallas_call(
        kern,
        out_shape=jax.ShapeDtypeStruct((3, batch * dilation, tiles, rows, tn), BF16),
        grid=(3, n // tm),
        in_specs=[pl.BlockSpec((tm, k), lambda j, i: (i, 0)),
                  pl.BlockSpec((k, tn), lambda j, i: (0, first_block + j))],
        out_specs=pl.BlockSpec((None, dilation, 1, rows, tn), lambda j, i: (j, i // tiles, i % tiles, 0, 0)),
        scratch_shapes=[pltpu.VMEM((tn // LANE, tm, LANE), F32)],
        compiler_params=_params("parallel", "parallel"),
        name=f"proj_att_d{dilation}",
    )(x_bf, w_bf)
    n_blk = seq // (dilation * ATT_BLOCK)
    return out.reshape(3, batch * dilation, n_blk, ATT_BLOCK, tn)


def _proj_units_f32_kernel(x_ref, w_ref, *refs):
    n_cast = (len(refs) - 2) // 2
    o_ref, xb_ref = refs[n_cast], refs[n_cast + 1]
    xb = x_ref[...].astype(xb_ref.dtype)
    xb_ref[...] = xb
    tn = o_ref.shape[-1]
    for part in range(3):
        o_ref[part, 0, 0] = _mm(xb, w_ref[:, part * tn:(part + 1) * tn]).astype(o_ref.dtype)
    _run_casts(refs[:n_cast], refs[n_cast + 2:])


def _proj_units_from_f32(x_f32, w_qkv, batch, seq, cast=()):
    n, k = x_f32.shape
    tm, tn = _tile(seq, 256), ATT_GROUP_WIDTH
    assert w_qkv.shape == (k, 3 * tn)
    tiles = seq // tm
    cast_in, cast_out, cast_shapes = _cast_jobs(cast, n // tm, lambda i: i)
    outs = pl.pallas_call(
        _proj_units_f32_kernel,
        out_shape=[jax.ShapeDtypeStruct((3, batch, tiles, tm, tn), BF16),
                   jax.ShapeDtypeStruct((n, k), BF16)] + cast_shapes,
        grid=(n // tm,),
        in_specs=[pl.BlockSpec((tm, k), lambda i: (i, 0)),
                  pl.BlockSpec((k, 3 * tn), lambda i: (0, 0), pipeline_mode=pl.Buffered(1))] + cast_in,
        out_specs=[pl.BlockSpec((3, 1, 1, tm, tn), lambda i: (0, i // tiles, i % tiles, 0, 0)),
                   pl.BlockSpec((tm, k), lambda i: (i, 0))] + cast_out,
        compiler_params=_params("arbitrary"),
        name="proj_att_d1",
    )(x_f32, w_qkv, *[w for w, _ in cast])
    return [outs[0].reshape(3, batch, seq // ATT_BLOCK, ATT_BLOCK, tn)] + list(outs[1:])


def _attn_span(q_ref, k_ref, kp_ref, v_ref, vp_ref, o_ref, lse_ref, *, dilation, nb):
    qi = lax.broadcasted_iota(jnp.int32, (ATT_BLOCK, 2 * ATT_BLOCK), 0)
    kj = lax.broadcasted_iota(jnp.int32, (ATT_BLOCK, 2 * ATT_BLOCK), 1)
    in_window = kj <= qi + ATT_BLOCK
    valid_inner = jnp.logical_and(kj >= qi, in_window)
    lo = jnp.where(pl.program_id(1) > 0, 0, ATT_BLOCK)
    valid_first = jnp.logical_and(kj >= jnp.maximum(qi, lo), in_window)
    scale = ATT_HEAD_DIM ** -0.5
    dn = (((1,), (1,)), ((), ()))
    ones = jnp.ones((2 * ATT_BLOCK, ATT_HEAD_DIM), BF16)
    units = [(r, i, h) for r in range(dilation) for i in range(nb) for h in range(o_ref.shape[0])]

    def cols(h):
        return slice(h * ATT_HEAD_DIM, (h + 1) * ATT_HEAD_DIM)

    def with_prev(cur_ref, prev_ref, r, i, h):
        prev = prev_ref[r, 0, :, cols(h)] if i == 0 else cur_ref[r, i - 1, :, cols(h)]
        return jnp.concatenate([prev, cur_ref[r, i, :, cols(h)]], axis=0)

    scores = [lax.dot_general(q_ref[r, i, :, cols(h)], with_prev(k_ref, kp_ref, r, i, h), dn,
                              preferred_element_type=F32) * scale for r, i, h in units]
    probs, maxes = [], []
    for (r, i, h), s in zip(units, scores):
        s = jnp.where(valid_first if i == 0 else valid_inner, s, NEG_INF)
        m = jnp.max(s, axis=-1, keepdims=True)
        probs.append(jnp.exp(s - m).astype(BF16))
        maxes.append(m)
    for (r, i, h), p, m in zip(units, probs, maxes):
        v_aug = jnp.concatenate([with_prev(v_ref, vp_ref, r, i, h), ones], axis=1)
        acc = _mm(p, v_aug)
        den = acc[:, ATT_HEAD_DIM:]
        if dilation == 1:
            rows = pl.ds(i * ATT_BLOCK, ATT_BLOCK)
        else:
            rows = pl.ds(i * ATT_BLOCK * dilation + r, ATT_BLOCK, stride=dilation)
        o_ref[h, rows, :] = acc[:, :ATT_HEAD_DIM] / den
        lse_ref[h, rows, :] = m + jnp.log(den)


def _attn_kernel(q_ref, k_ref, kp_ref, v_ref, vp_ref, o_ref, lse_ref, *, dilation, nb):
    _attn_span(q_ref, k_ref, kp_ref, v_ref, vp_ref, o_ref, lse_ref, dilation=dilation, nb=nb)


ATT_COMBINE_ROWS = 256


def _attn_combine_kernel(q_ref, k_ref, kp_ref, v_ref, vp_ref, o1, l1, o2, l2, out_ref, o3, l3, *, dilation, nb):
    _attn_span(q_ref, k_ref, kp_ref, v_ref, vp_ref, o3, l3, dilation=dilation, nb=nb)

    def body(c, carry):
        rows = pl.ds(pl.multiple_of(c * ATT_COMBINE_ROWS, ATT_COMBINE_ROWS), ATT_COMBINE_ROWS)
        for h in range(o3.shape[0]):
            a, b, d = l1[h, rows, :], l2[h, rows, :], l3[h, rows, :]
            m = jnp.maximum(jnp.maximum(a, b), d)
            ea, eb, ed = jnp.exp(a - m), jnp.exp(b - m), jnp.exp(d - m)
            s = ea + eb + ed
            out = (ea / s) * o1[h, rows, :] + (eb / s) * o2[h, rows, :] + (ed / s) * o3[h, rows, :]
            out_ref[rows, h * ATT_HEAD_DIM:(h + 1) * ATT_HEAD_DIM] = out.astype(out_ref.dtype)
        return carry

    lax.fori_loop(0, out_ref.shape[0] // ATT_COMBINE_ROWS, body, 0)


def _attention(qkv, dilation, batch, seq, others=None):
    n_blk = qkv.shape[2]
    assert ATT_SPAN_BLOCKS % dilation == 0
    nb = ATT_SPAN_BLOCKS // dilation
    assert n_blk % nb == 0
    spans = n_blk // nb
    span_tokens = ATT_SPAN_BLOCKS * ATT_BLOCK
    heads = ATT_HEADS_PER_STEP if others is None else ATT_HEADS_PER_COMBINE_STEP
    hw = heads * ATT_HEAD_DIM
    cur = (None, dilation, nb, ATT_BLOCK, hw)
    one = (None, dilation, 1, ATT_BLOCK, hw)

    def at(part):
        return lambda b, s, hp: (part, b, s, 0, hp)

    def before(part):
        return lambda b, s, hp: (part, b, jnp.maximum(s * nb - 1, 0), 0, hp)

    qkv_specs = [pl.BlockSpec(cur, at(0)), pl.BlockSpec(cur, at(1)), pl.BlockSpec(one, before(1)),
                 pl.BlockSpec(cur, at(2)), pl.BlockSpec(one, before(2))]
    per_head = jax.ShapeDtypeStruct((ATT_HEADS, batch * seq, ATT_HEAD_DIM), F32)
    head_blk = (heads, span_tokens, ATT_HEAD_DIM)
    head_spec = pl.BlockSpec(head_blk, lambda b, s, hp: (hp, b * spans + s, 0))
    grid = (batch, spans, ATT_HEADS // heads)
    if others is None:
        return pl.pallas_call(
            functools.partial(_attn_kernel, dilation=dilation, nb=nb),
            out_shape=(per_head, per_head),
            grid=grid,
            in_specs=qkv_specs,
            out_specs=(head_spec, head_spec),
            compiler_params=_params("parallel", "parallel", "parallel"),
            name=f"attn_d{dilation}",
        )(qkv, qkv, qkv, qkv, qkv)
    return pl.pallas_call(
        functools.partial(_attn_combine_kernel, dilation=dilation, nb=nb),
        out_shape=jax.ShapeDtypeStruct((batch * seq, ATT_GROUP_WIDTH), BF16),
        grid=grid,
        in_specs=qkv_specs + [head_spec] * 4,
        out_specs=pl.BlockSpec((span_tokens, hw), lambda b, s, hp: (b * spans + s, hp)),
        scratch_shapes=[pltpu.VMEM(head_blk, F32), pltpu.VMEM(head_blk, F32)],
        compiler_params=_params("parallel", "parallel", "parallel"),
        name=f"attn_d{dilation}_combine",
    )(qkv, qkv, qkv, qkv, qkv, *others)


RET_CHUNKS_PER_STEP = 16


def _ret_kernel(q_ref, k_ref, v_ref, g_ref, gn_ref, intra_ref, cross_ref, sdec_ref, cdec_ref,
                z_ref, state_ref, *, cb):
    @pl.when(pl.program_id(2) == 0)
    def _():
        state_ref[...] = jnp.zeros_like(state_ref)

    c = RET_CHUNK
    intra, cross_decay, state_decay, chunk_decay = intra_ref[...], cross_ref[...], sdec_ref[...], cdec_ref[...]
    gn = gn_ref[...]
    qs, inners, updates = [], [], []
    for t in range(cb):
        rows = slice(t * c, (t + 1) * c)
        q = q_ref[rows, :].astype(BF16)
        k = k_ref[rows, :]
        v = v_ref[rows, :]
        att = lax.dot_general(q, k.astype(BF16), (((1,), (1,)), ((), ())), preferred_element_type=F32) * intra
        inners.append(_mm(att.astype(BF16), v))
        kd = (k * state_decay).astype(BF16)
        updates.append(lax.dot_general(kd, v, (((0,), (0,)), ((), ())), preferred_element_type=F32))
        qs.append(q)
    state = state_ref[...]
    for t in range(cb):
        rows = slice(t * c, (t + 1) * c)
        r = inners[t] + _mm(qs[t], state.astype(BF16)) * cross_decay
        state = chunk_decay * state + updates[t]
        mu = jnp.mean(r, axis=-1, keepdims=True)
        var = jnp.mean(jnp.square(r - mu), axis=-1, keepdims=True)
        rn = (r - mu) * lax.rsqrt(var + GN_EPS) * gn
        z_ref[rows, :] = (g_ref[rows, :] * rn).astype(z_ref.dtype)
    state_ref[...] = state


def _retention(qk, v, gates, gn_g, batch, seq):
    n = qk.shape[0]
    c = RET_CHUNK
    nc = seq // c
    cb = min(RET_CHUNKS_PER_STEP, nc)
    assert nc % cb == 0
    steps = nc // cb
    hh = jnp.arange(RET_HEADS, dtype=F32)
    log_gamma = jnp.log(1.0 - 2.0 ** (-5.0 - hh))
    idx = jnp.arange(c, dtype=F32)
    diff = idx[:, None] - idx[None, :]
    intra = jnp.where(diff >= 0, jnp.exp(jnp.maximum(diff, 0.0)[None] * log_gamma[:, None, None]), 0.0)
    cross_decay = jnp.exp((idx + 1.0)[None, :] * log_gamma[:, None])[..., None]
    state_decay = jnp.exp((c - 1.0 - idx)[None, :] * log_gamma[:, None])[..., None]
    chunk_decay = jnp.exp(c * log_gamma)[:, None, None]

    def rows(b, h, s):
        return b * steps + s

    kern = functools.partial(_ret_kernel, cb=cb)
    return pl.pallas_call(
        kern,
        out_shape=jax.ShapeDtypeStruct((n, RET_V_WIDTH), BF16),
        grid=(batch, RET_HEADS, steps),
        in_specs=[
            pl.BlockSpec((cb * c, RET_QK_DIM), lambda b, h, s: (rows(b, h, s), h)),
            pl.BlockSpec((cb * c, RET_QK_DIM), lambda b, h, s: (rows(b, h, s), RET_HEADS + h)),
            pl.BlockSpec((cb * c, RET_V_DIM), lambda b, h, s: (rows(b, h, s), h)),
            pl.BlockSpec((cb * c, RET_V_DIM), lambda b, h, s: (rows(b, h, s), h)),
            pl.BlockSpec((1, RET_V_DIM), lambda b, h, s: (0, h)),
            pl.BlockSpec((None, c, c), lambda b, h, s: (h, 0, 0)),
            pl.BlockSpec((None, c, 1), lambda b, h, s: (h, 0, 0)),
            pl.BlockSpec((None, c, 1), lambda b, h, s: (h, 0, 0)),
            pl.BlockSpec((None, 1, 1), lambda b, h, s: (h, 0, 0)),
        ],
        out_specs=pl.BlockSpec((cb * c, RET_V_DIM), lambda b, h, s: (rows(b, h, s), h)),
        scratch_shapes=[pltpu.VMEM((RET_QK_DIM, RET_V_DIM), F32)],
        compiler_params=_params("parallel", "parallel", "arbitrary"),
        name="retention",
    )(qk, qk, v, gates, gn_g, intra, cross_decay, state_decay, chunk_decay)


def _mix_kernel(oa_ref, z_ref, wa_ref, wr_ref, ga_ref, gr_ref, o_ref):
    ya = _mm(oa_ref[...], wa_ref[...])
    yr = _mm(z_ref[...], wr_ref[...])
    o_ref[...] = (_sigmoid(ga_ref[...]) * ya + _sigmoid(gr_ref[...]) * yr).astype(o_ref.dtype)


def _mix(o_att, z, wa, wr, gates, gate_col0):
    n = o_att.shape[0]
    d = wa.shape[1]
    tm, tn = _tile(n, 512), _tile(d, 1024)
    assert gate_col0 % tn == 0
    ga_off = gate_col0 // tn
    gr_off = ga_off + d // tn
    return pl.pallas_call(
        _mix_kernel,
        out_shape=jax.ShapeDtypeStruct((n, d), BF16),
        grid=(d // tn, n // tm),
        in_specs=[pl.BlockSpec((tm, o_att.shape[1]), lambda j, i: (i, 0)),
                  pl.BlockSpec((tm, z.shape[1]), lambda j, i: (i, 0)),
                  pl.BlockSpec((wa.shape[0], tn), lambda j, i: (0, j)),
                  pl.BlockSpec((wr.shape[0], tn), lambda j, i: (0, j)),
                  pl.BlockSpec((tm, tn), lambda j, i: (i, ga_off + j)),
                  pl.BlockSpec((tm, tn), lambda j, i: (i, gr_off + j))],
        out_specs=pl.BlockSpec((tm, tn), lambda j, i: (i, j)),
        compiler_params=_params("parallel", "parallel"),
        name="mix",
    )(o_att, z, wa, wr, gates, gates)


def _layer_norm_rows(t, g, b):
    mu = jnp.mean(t, axis=-1, keepdims=True)
    var = jnp.mean(jnp.square(t - mu), axis=-1, keepdims=True)
    return (t - mu) * lax.rsqrt(var + LN_EPS) * g + b


LN_ROW_CHUNK = 256


def _layer_norm_inplace(o_ref, g_ref, b_ref, cast_ref=None):
    g, b = g_ref[...], b_ref[...]
    chunk = min(LN_ROW_CHUNK, o_ref.shape[0])

    def body(r, carry):
        rows = pl.ds(pl.multiple_of(r * chunk, chunk), chunk)
        y = _layer_norm_rows(o_ref[rows, :], g, b)
        o_ref[rows, :] = y
        if cast_ref is not None:
            cast_ref[rows, :] = y.astype(cast_ref.dtype)
        return carry

    lax.fori_loop(0, o_ref.shape[0] // chunk, body, 0)


def _proj_ln_kernel(a_ref, w_ref, res_ref, g_ref, b_ref, of_ref, ob_ref, *, alpha):
    of_ref[...] = alpha * res_ref[...] + _mm(a_ref[...], w_ref[...])
    _layer_norm_inplace(of_ref, g_ref, b_ref, ob_ref)


def _proj_ln(a, w, res, g, b, alpha):
    n, k = a.shape
    d = w.shape[1]
    tm = _tile(n, 128)
    kern = functools.partial(_proj_ln_kernel, alpha=alpha)
    row = pl.BlockSpec((tm, d), lambda i: (i, 0))
    vec = pl.BlockSpec((1, d), lambda i: (0, 0))
    return pl.pallas_call(
        kern,
        out_shape=(jax.ShapeDtypeStruct((n, d), F32), jax.ShapeDtypeStruct((n, d), BF16)),
        grid=(n // tm,),
        in_specs=[pl.BlockSpec((tm, k), lambda i: (i, 0)),
                  pl.BlockSpec((k, d), lambda i: (0, 0), pipeline_mode=pl.Buffered(1)),
                  row, vec, vec],
        out_specs=(row, row),
        compiler_params=_params("parallel"),
        name="proj_ln",
    )(a, w, res, g, b)


SWIGLU_ROW_SPLITS = 4


def _swiglu_kernel(h_ref, wg_ref, wu_ref, o_ref):
    wg, wu = wg_ref[...], wu_ref[...]
    for rows in _row_halves(o_ref.shape[0], SWIGLU_ROW_SPLITS):
        h = h_ref[rows, :]
        gate = _mm(h, wg)
        up = _mm(h, wu)
        o_ref[rows, :] = ((gate * _sigmoid(gate)) * up).astype(o_ref.dtype)


def _swiglu(h_bf, wg, wu):
    n, k = h_bf.shape
    f = wg.shape[1]
    tm, tn = _tile(n, 2048), _tile(f, 256)
    w_spec = pl.BlockSpec((k, tn), lambda i, j: (0, j))
    return pl.pallas_call(
        _swiglu_kernel,
        out_shape=jax.ShapeDtypeStruct((n, f), BF16),
        grid=(n // tm, f // tn),
        in_specs=[pl.BlockSpec((tm, k), lambda i, j: (i, 0)), w_spec, w_spec],
        out_specs=pl.BlockSpec((tm, tn), lambda i, j: (i, j)),
        compiler_params=_params("parallel", "parallel"),
        name="swiglu",
    )(h_bf, wg, wu)


def _ple_kernel(hb_ref, wg_ref, p_ref, wu_ref, hf_ref, o_ref, *, alpha):
    gate = _sigmoid(_mm(hb_ref[...], wg_ref[...]))
    up = _mm(p_ref[...], wu_ref[...])
    o_ref[...] = alpha * hf_ref[...] + gate * up


def _ple_base(h_bf, h_f32, p_bf, wpg, wpu, alpha):
    n, k = h_bf.shape
    d = wpg.shape[1]
    tm, tn = _tile(n, 512), _tile(d, 1024)
    kern = functools.partial(_ple_kernel, alpha=alpha)
    return pl.pallas_call(
        kern,
        out_shape=jax.ShapeDtypeStruct((n, d), F32),
        grid=(d // tn, n // tm),
        in_specs=[pl.BlockSpec((tm, k), lambda j, i: (i, 0)),
                  pl.BlockSpec((k, tn), lambda j, i: (0, j)),
                  pl.BlockSpec((tm, p_bf.shape[1]), lambda j, i: (i, 0)),
                  pl.BlockSpec((p_bf.shape[1], tn), lambda j, i: (0, j)),
                  pl.BlockSpec((tm, tn), lambda j, i: (i, j))],
        out_specs=pl.BlockSpec((tm, tn), lambda j, i: (i, j)),
        compiler_params=_params("parallel", "parallel"),
        name="ple_base",
    )(h_bf, wpg, p_bf, wpu, h_f32)


def _down_ln_kernel(a_ref, w_ref, base_ref, g_ref, b_ref, o_ref, *, tn):
    j = pl.program_id(1)
    col = pl.multiple_of(j * tn, tn)
    o_ref[:, pl.ds(col, tn)] = base_ref[...] + _mm(a_ref[...], w_ref[...])

    @pl.when(j == pl.num_programs(1) - 1)
    def _():
        _layer_norm_inplace(o_ref, g_ref, b_ref)


def _down_ln(hidden, wd, base, g, b):
    n, f = hidden.shape
    d = wd.shape[1]
    tm, tn = _tile(n, 512), _tile(d, 256)
    kern = functools.partial(_down_ln_kernel, tn=tn)
    vec = pl.BlockSpec((1, d), lambda i, j: (0, 0))
    return pl.pallas_call(
        kern,
        out_shape=jax.ShapeDtypeStruct((n, d), F32),
        grid=(n // tm, d // tn),
        in_specs=[pl.BlockSpec((tm, f), lambda i, j: (i, 0)),
                  pl.BlockSpec((f, tn), lambda i, j: (0, j)),
                  pl.BlockSpec((tm, tn), lambda i, j: (i, j)),
                  vec, vec],
        out_specs=pl.BlockSpec((tm, d), lambda i, j: (i, 0)),
        compiler_params=_params("parallel", "arbitrary"),
        name="down_ln",
    )(hidden, wd, base, g, b)


def _layer(h_f32, p_bf, w_in, w_attn_out, w_ret_out, ret_gn_g, w_o, ln1_g, ln1_b,
           w_ffn_gate, w_ffn_up, w_ffn_down, w_ple_gate, w_ple_up, ln2_g, ln2_b,
           cos, sin, batch, seq, alpha):
    d_model = h_f32.shape[1]
    c_qr = 3 * ATT_QKV_WIDTH
    c_vr = c_qr + 2 * RET_QK_WIDTH
    c_gr = c_vr + RET_V_WIDTH
    c_end = c_gr + RET_V_WIDTH + 2 * d_model
    assert w_in.shape[1] == c_end and c_end % COL_BLOCK == 0
    assert ATT_DILATIONS[0] == 1 and ATT_GROUP_WIDTH == COL_BLOCK
    n_groups = len(ATT_DILATIONS)

    def att_blocks(g):
        return [part * n_groups + g for part in range(3)]

    def blocks(c0, c1):
        return list(range(c0 // COL_BLOCK, c1 // COL_BLOCK))

    w_first = _round_blocks(w_in, att_blocks(0))
    qkv_first, h_bf, w_rot, w_o, w_ple_gate = _proj_units_from_f32(
        h_f32, w_first, batch, seq, cast=((w_in, blocks(c_qr, c_vr)), (w_o, None), (w_ple_gate, None)))
    qk_r, w_vr, w_attn_out = _proj_rot(
        h_bf, w_rot, 0, cos, sin, seq, cast=((w_in, blocks(c_vr, c_gr)), (w_attn_out, None)))
    v_r, w_gates, w_ret_out = _proj(h_bf, w_vr, 0, RET_V_WIDTH, BF16, "proj_vr", rows_per_step=512,
                                    cast=((w_in, blocks(c_gr, c_end)), (w_ret_out, None)))
    later_groups = list(range(1, n_groups))
    gates, w_att_later, w_ffn_down = _proj(
        h_bf, w_gates, 0, RET_V_WIDTH + 2 * d_model, F32, "proj_gates", silu_width=RET_V_WIDTH,
        cast=((w_in, sum((att_blocks(g) for g in later_groups), [])), (w_ffn_down, None)))

    others = list(_attention(qkv_first, 1, batch, seq))
    for pos, g in enumerate(later_groups):
        qkv = _proj_units(h_bf, w_att_later, 3 * pos, ATT_DILATIONS[g], batch, seq)
        if g < n_groups - 1:
            others.extend(_attention(qkv, ATT_DILATIONS[g], batch, seq))
        else:
            o_att = _attention(qkv, ATT_DILATIONS[g], batch, seq, others=others)

    z = _retention(qk_r, v_r, gates, ret_gn_g, batch, seq)

    mixed = _mix(o_att, z, w_attn_out, w_ret_out, gates, RET_V_WIDTH)
    h1_f32, h1_bf = _proj_ln(mixed, w_o, h_f32, ln1_g, ln1_b, alpha)

    hidden = _swiglu(h1_bf, w_ffn_gate, w_ffn_up)
    base = _ple_base(h1_bf, h1_f32, p_bf, w_ple_gate, w_ple_up, alpha)
    return _down_ln(hidden, w_ffn_down, base, ln2_g, ln2_b)


def kernel(x, p, w_in, w_attn_out, w_ret_out, ret_gn_g, w_o, ln1_g, ln1_b, w_ffn_gate, w_ffn_up,
           w_ffn_down, w_ple_gate, w_ple_up, ln2_g, ln2_b):
    batch, seq, d_model = x.shape
    depth = w_in.shape[0]
    n = batch * seq
    alpha = (2 * depth) ** 0.25

    half = RET_QK_DIM // 2
    pos = jnp.arange(seq, dtype=F32)
    inv_freq = RET_ROPE_BASE ** (-jnp.arange(half, dtype=F32) / half)
    ang = pos[:, None] * inv_freq[None, :]
    cos, sin = jnp.cos(ang), jnp.sin(ang)

    h = x.reshape(n, d_model)
    for i in range(depth):
        bf = lambda w: w[i].astype(BF16)
        row = lambda v: v[i].reshape(1, -1)
        h = _layer(h, p[i].reshape(n, -1).astype(BF16),
                   w_in[i], w_attn_out[i], w_ret_out[i], row(ret_gn_g), w_o[i], row(ln1_g), row(ln1_b),
                   w_ffn_gate[i], w_ffn_up[i], w_ffn_down[i], w_ple_gate[i], bf(w_ple_up),
                   row(ln2_g), row(ln2_b), cos, sin, batch, seq, alpha)
    return h.reshape(batch, seq, d_model).astype(x.dtype)
```

```python
import functools

import jax
import jax.numpy as jnp
from jax import lax
from jax.experimental import pallas as pl
from jax.experimental.pallas import tpu as pltpu

ATT_HEAD_DIM = 128
ATT_HEADS = 8
ATT_DILATIONS = (1, 4, 16)
ATT_BLOCK = 128
ATT_GROUP_WIDTH = ATT_HEADS * ATT_HEAD_DIM
ATT_QKV_WIDTH = len(ATT_DILATIONS) * ATT_GROUP_WIDTH
RET_HEADS = 8
RET_QK_DIM = 256
RET_V_DIM = 512
RET_QK_WIDTH = RET_HEADS * RET_QK_DIM
RET_V_WIDTH = RET_HEADS * RET_V_DIM
RET_CHUNK = 128
RET_ROPE_BASE = 10000.0
LN_EPS = 1e-5
GN_EPS = 1e-6
NEG_INF = -1e30

V7X_VMEM_LIMIT_BYTES = 56 * 1024 * 1024
LANE = 128

BF16 = jnp.bfloat16
F32 = jnp.float32


def _params(*semantics):
    return pltpu.CompilerParams(dimension_semantics=semantics, vmem_limit_bytes=V7X_VMEM_LIMIT_BYTES)


def _tile(dim, pref):
    if dim <= pref:
        return dim
    t = (pref // LANE) * LANE
    while t > LANE and dim % t:
        t -= LANE
    assert dim % t == 0, (dim, pref)
    return t


def _sigmoid(v):
    return 1.0 / (1.0 + jnp.exp(-v))


def _mm(a, b):
    return lax.dot_general(a, b, (((1,), (0,)), ((), ())), preferred_element_type=F32)


ROW_SPLITS = 2


def _row_halves(n_rows, splits=ROW_SPLITS):
    rows = n_rows // splits
    return [slice(c * rows, (c + 1) * rows) for c in range(splits)]


BF16_SUBLANES = 16


COL_BLOCK = 1024


def _cast_job(w, col_blocks, steps, linear_step):
    rows, cols = w.shape
    ncb, width = (1, cols) if col_blocks is None else (len(col_blocks), COL_BLOCK)
    slab = -(-(-(-rows * ncb // steps)) // BF16_SUBLANES) * BF16_SUBLANES
    while rows % slab:
        slab += BF16_SUBLANES
    pieces = (rows // slab) * ncb
    assert pieces <= steps

    def piece(*idx):
        return jnp.minimum(linear_step(*idx), pieces - 1)

    def source_col(local):
        if col_blocks is None:
            return 0
        return sum(jnp.where(local == i, cb, 0) for i, cb in enumerate(col_blocks))

    in_spec = pl.BlockSpec((slab, width), lambda *idx: (piece(*idx) // ncb, source_col(piece(*idx) % ncb)))
    out_spec = pl.BlockSpec((slab, width), lambda *idx: (piece(*idx) // ncb, piece(*idx) % ncb))
    return in_spec, out_spec, jax.ShapeDtypeStruct((rows, ncb * width), BF16)


def _cast_jobs(cast, steps, linear_step):
    jobs = [_cast_job(w, cbs, steps, linear_step) for w, cbs in cast]
    return [j[0] for j in jobs], [j[1] for j in jobs], [j[2] for j in jobs]


def _run_casts(srcs, dsts):
    for src, dst in zip(srcs, dsts):
        dst[...] = src[...].astype(dst.dtype)


def _round_kernel(src_ref, dst_ref):
    dst_ref[...] = src_ref[...].astype(dst_ref.dtype)


def _round_blocks(w, col_blocks, slab=512):
    rows = w.shape[0]
    slab = min(slab, rows)
    assert rows % slab == 0
    steps = (rows // slab) * len(col_blocks)
    in_spec, out_spec, out_shape = _cast_job(w, col_blocks, steps, lambda t: t)
    return pl.pallas_call(
        _round_kernel, out_shape=out_shape, grid=(steps,), in_specs=[in_spec], out_specs=out_spec,
        compiler_params=_params("parallel"), name="round_first_weights",
    )(w)


def _proj_kernel(x_ref, w_ref, *refs):
    n_cast = (len(refs) - 1) // 2
    o_ref = refs[n_cast]
    o_ref[...] = _mm(x_ref[...], w_ref[...]).astype(o_ref.dtype)
    _run_casts(refs[:n_cast], refs[n_cast + 1:])


def _proj(x_bf, w_bf, col0, width, out_dtype, name, cast=(), rows_per_step=1024):
    n, k = x_bf.shape
    tm, tn = _tile(n, rows_per_step), _tile(width, 1024)
    assert col0 % tn == 0
    off = col0 // tn
    n_i = n // tm
    steps = (width // tn) * n_i
    cast_in, cast_out, cast_shapes = _cast_jobs(cast, steps, lambda j, i: j * n_i + i)
    outs = pl.pallas_call(
        _proj_kernel,
        out_shape=[jax.ShapeDtypeStruct((n, width), out_dtype)] + cast_shapes,
        grid=(width // tn, n_i),
        in_specs=[pl.BlockSpec((tm, k), lambda j, i: (i, 0)),
                  pl.BlockSpec((k, tn), lambda j, i: (0, off + j))] + cast_in,
        out_specs=[pl.BlockSpec((tm, tn), lambda j, i: (i, j))] + cast_out,
        compiler_params=_params("arbitrary", "arbitrary"),
        name=name,
    )(x_bf, w_bf, *[w for w, _ in cast])
    return outs if cast else outs[0]


def _proj_rot_kernel(x_ref, w_ref, cos_ref, sin_ref, *refs, n_q_blocks, k_scale):
    n_cast = (len(refs) - 1) // 2
    o_ref = refs[n_cast]
    _run_casts(refs[:n_cast], refs[n_cast + 1:])
    j = pl.program_id(0)
    acc = _mm(x_ref[...], w_ref[...])
    scale = jnp.where(j >= n_q_blocks, k_scale, 1.0).astype(F32)
    cos = cos_ref[...]
    sin = sin_ref[...]
    half = RET_QK_DIM // 2
    for h in range(acc.shape[1] // RET_QK_DIM):
        lo = h * RET_QK_DIM
        t1 = acc[:, lo:lo + half]
        t2 = acc[:, lo + half:lo + RET_QK_DIM]
        o_ref[:, lo:lo + half] = (t1 * cos - t2 * sin) * scale
        o_ref[:, lo + half:lo + RET_QK_DIM] = (t1 * sin + t2 * cos) * scale


def _proj_rot(x_bf, w_bf, col0, cos, sin, seq, cast=()):
    n, k = x_bf.shape
    width = 2 * RET_QK_WIDTH
    tm, tn = _tile(seq, 1024), 1024
    assert col0 % tn == 0 and RET_QK_WIDTH % tn == 0 and n % seq == 0
    off = col0 // tn
    pos_blocks = seq // tm
    n_i = n // tm
    kern = functools.partial(_proj_rot_kernel, n_q_blocks=RET_QK_WIDTH // tn, k_scale=RET_QK_DIM ** -0.5)
    half = RET_QK_DIM // 2
    cast_in, cast_out, cast_shapes = _cast_jobs(cast, (width // tn) * n_i, lambda j, i: j * n_i + i)
    outs = pl.pallas_call(
        kern,
        out_shape=[jax.ShapeDtypeStruct((n, width), F32)] + cast_shapes,
        grid=(width // tn, n_i),
        in_specs=[pl.BlockSpec((tm, k), lambda j, i: (i, 0)),
                  pl.BlockSpec((k, tn), lambda j, i: (0, off + j)),
                  pl.BlockSpec((tm, half), lambda j, i: (i % pos_blocks, 0)),
                  pl.BlockSpec((tm, half), lambda j, i: (i % pos_blocks, 0))] + cast_in,
        out_specs=[pl.BlockSpec((tm, tn), lambda j, i: (i, j))] + cast_out,
        compiler_params=_params("arbitrary", "arbitrary"),
        name="proj_rot",
    )(x_bf, w_bf, cos, sin, *[w for w, _ in cast])
    return outs if cast else outs[0]


ATT_SPAN_BLOCKS = 16
ATT_HEADS_PER_STEP = 4
ATT_HEADS_PER_COMBINE_STEP = 2


def _proj_units_kernel(x_ref, w_ref, o_ref, acc_ref, *, dilation):
    if dilation == 1:
        o_ref[0, 0] = _mm(x_ref[...], w_ref[...]).astype(o_ref.dtype)
        return
    w = w_ref[...]
    slabs = w.shape[1] // LANE
    for tok in _row_halves(x_ref.shape[0]):
        acc = _mm(x_ref[tok, :], w)
        for s in range(slabs):
            acc_ref[s, tok, :] = acc[:, s * LANE:(s + 1) * LANE]
        rows = acc.shape[0] // dilation
        dst = slice(tok.start // dilation, tok.start // dilation + rows)
        for r in range(dilation):
            for s in range(slabs):
                piece = acc_ref[s, pl.ds(tok.start + r, rows, stride=dilation), :]
                o_ref[r, 0, dst, s * LANE:(s + 1) * LANE] = piece.astype(o_ref.dtype)


def _proj_units(x_bf, w_bf, first_block, dilation, batch, seq):
    n, k = x_bf.shape
    tm, tn = _tile(seq, 1024), ATT_GROUP_WIDTH
    assert (tm // ROW_SPLITS) % (dilation * BF16_SUBLANES) == 0 and seq % tm == 0
    rows = tm // dilation
    tiles = seq // tm
    kern = functools.partial(_proj_units_kernel, dilation=dilation)
    out = pl.pallas_call(
        kern,
        out_shape=jax.ShapeDtypeStruct((3, batch * dilation, tiles, rows, tn), BF16),
        grid=(3, n // tm),
        in_specs=[pl.BlockSpec((tm, k), lambda j, i: (i, 0)),
                  pl.BlockSpec((k, tn), lambda j, i: (0, first_block + j))],
        out_specs=pl.BlockSpec((None, dilation, 1, rows, tn), lambda j, i: (j, i // tiles, i % tiles, 0, 0)),
        scratch_shapes=[pltpu.VMEM((tn // LANE, tm, LANE), F32)],
        compiler_params=_params("parallel", "parallel"),
        name=f"proj_att_d{dilation}",
    )(x_bf, w_bf)
    n_blk = seq // (dilation * ATT_BLOCK)
    return out.reshape(3, batch * dilation, n_blk, ATT_BLOCK, tn)


def _proj_units_f32_kernel(x_ref, w_ref, *refs):
    n_cast = (len(refs) - 2) // 2
    o_ref, xb_ref = refs[n_cast], refs[n_cast + 1]
    xb = x_ref[...].astype(xb_ref.dtype)
    xb_ref[...] = xb
    tn = o_ref.shape[-1]
    for part in range(3):
        o_ref[part, 0, 0] = _mm(xb, w_ref[:, part * tn:(part + 1) * tn]).astype(o_ref.dtype)
    _run_casts(refs[:n_cast], refs[n_cast + 2:])


def _proj_units_from_f32(x_f32, w_qkv, batch, seq, cast=()):
    n, k = x_f32.shape
    tm, tn = _tile(seq, 256), ATT_GROUP_WIDTH
    assert w_qkv.shape == (k, 3 * tn)
    tiles = seq // tm
    cast_in, cast_out, cast_shapes = _cast_jobs(cast, n // tm, lambda i: i)
    outs = pl.pallas_call(
        _proj_units_f32_kernel,
        out_shape=[jax.ShapeDtypeStruct((3, batch, tiles, tm, tn), BF16),
                   jax.ShapeDtypeStruct((n, k), BF16)] + cast_shapes,
        grid=(n // tm,),
        in_specs=[pl.BlockSpec((tm, k), lambda i: (i, 0)),
                  pl.BlockSpec((k, 3 * tn), lambda i: (0, 0), pipeline_mode=pl.Buffered(1))] + cast_in,
        out_specs=[pl.BlockSpec((3, 1, 1, tm, tn), lambda i: (0, i // tiles, i % tiles, 0, 0)),
                   pl.BlockSpec((tm, k), lambda i: (i, 0))] + cast_out,
        compiler_params=_params("arbitrary"),
        name="proj_att_d1",
    )(x_f32, w_qkv, *[w for w, _ in cast])
    return [outs[0].reshape(3, batch, seq // ATT_BLOCK, ATT_BLOCK, tn)] + list(outs[1:])


def _attn_span(q_ref, k_ref, kp_ref, v_ref, vp_ref, o_ref, lse_ref, *, dilation, nb):
    qi = lax.broadcasted_iota(jnp.int32, (ATT_BLOCK, 2 * ATT_BLOCK), 0)
    kj = lax.broadcasted_iota(jnp.int32, (ATT_BLOCK, 2 * ATT_BLOCK), 1)
    in_window = kj <= qi + ATT_BLOCK
    valid_inner = jnp.logical_and(kj >= qi, in_window)
    lo = jnp.where(pl.program_id(1) > 0, 0, ATT_BLOCK)
    valid_first = jnp.logical_and(kj >= jnp.maximum(qi, lo), in_window)
    scale = ATT_HEAD_DIM ** -0.5
    dn = (((1,), (1,)), ((), ()))
    ones = jnp.ones((2 * ATT_BLOCK, ATT_HEAD_DIM), BF16)
    units = [(r, i, h) for r in range(dilation) for i in range(nb) for h in range(o_ref.shape[0])]

    def cols(h):
        return slice(h * ATT_HEAD_DIM, (h + 1) * ATT_HEAD_DIM)

    def with_prev(cur_ref, prev_ref, r, i, h):
        prev = prev_ref[r, 0, :, cols(h)] if i == 0 else cur_ref[r, i - 1, :, cols(h)]
        return jnp.concatenate([prev, cur_ref[r, i, :, cols(h)]], axis=0)

    scores = [lax.dot_general(q_ref[r, i, :, cols(h)], with_prev(k_ref, kp_ref, r, i, h), dn,
                              preferred_element_type=F32) * scale for r, i, h in units]
    probs, maxes = [], []
    for (r, i, h), s in zip(units, scores):
        s = jnp.where(valid_first if i == 0 else valid_inner, s, NEG_INF)
        m = jnp.max(s, axis=-1, keepdims=True)
        probs.append(jnp.exp(s - m).astype(BF16))
        maxes.append(m)
    for (r, i, h), p, m in zip(units, probs, maxes):
        v_aug = jnp.concatenate([with_prev(v_ref, vp_ref, r, i, h), ones], axis=1)
        acc = _mm(p, v_aug)
        den = acc[:, ATT_HEAD_DIM:]
        if dilation == 1:
            rows = pl.ds(i * ATT_BLOCK, ATT_BLOCK)
        else:
            rows = pl.ds(i * ATT_BLOCK * dilation + r, ATT_BLOCK, stride=dilation)
        o_ref[h, rows, :] = acc[:, :ATT_HEAD_DIM] / den
        lse_ref[h, rows, :] = m + jnp.log(den)


def _attn_kernel(q_ref, k_ref, kp_ref, v_ref, vp_ref, o_ref, lse_ref, *, dilation, nb):
    _attn_span(q_ref, k_ref, kp_ref, v_ref, vp_ref, o_ref, lse_ref, dilation=dilation, nb=nb)


ATT_COMBINE_ROWS = 256


def _attn_combine_kernel(q_ref, k_ref, kp_ref, v_ref, vp_ref, o1, l1, o2, l2, out_ref, o3, l3, *, dilation, nb):
    _attn_span(q_ref, k_ref, kp_ref, v_ref, vp_ref, o3, l3, dilation=dilation, nb=nb)

    def body(c, carry):
        rows = pl.ds(pl.multiple_of(c * ATT_COMBINE_ROWS, ATT_COMBINE_ROWS), ATT_COMBINE_ROWS)
        for h in range(o3.shape[0]):
            a, b, d = l1[h, rows, :], l2[h, rows, :], l3[h, rows, :]
            m = jnp.maximum(jnp.maximum(a, b), d)
            ea, eb, ed = jnp.exp(a - m), jnp.exp(b - m), jnp.exp(d - m)
            s = ea + eb + ed
            out = (ea / s) * o1[h, rows, :] + (eb / s) * o2[h, rows, :] + (ed / s) * o3[h, rows, :]
            out_ref[rows, h * ATT_HEAD_DIM:(h + 1) * ATT_HEAD_DIM] = out.astype(out_ref.dtype)
        return carry

    lax.fori_loop(0, out_ref.shape[0] // ATT_COMBINE_ROWS, body, 0)


def _attention(qkv, dilation, batch, seq, others=None):
    n_blk = qkv.shape[2]
    assert ATT_SPAN_BLOCKS % dilation == 0
    nb = ATT_SPAN_BLOCKS // dilation
    assert n_blk % nb == 0
    spans = n_blk // nb
    span_tokens = ATT_SPAN_BLOCKS * ATT_BLOCK
    heads = ATT_HEADS_PER_STEP if others is None else ATT_HEADS_PER_COMBINE_STEP
    hw = heads * ATT_HEAD_DIM
    cur = (None, dilation, nb, ATT_BLOCK, hw)
    one = (None, dilation, 1, ATT_BLOCK, hw)

    def at(part):
        return lambda b, s, hp: (part, b, s, 0, hp)

    def before(part):
        return lambda b, s, hp: (part, b, jnp.maximum(s * nb - 1, 0), 0, hp)

    qkv_specs = [pl.BlockSpec(cur, at(0)), pl.BlockSpec(cur, at(1)), pl.BlockSpec(one, before(1)),
                 pl.BlockSpec(cur, at(2)), pl.BlockSpec(one, before(2))]
    per_head = jax.ShapeDtypeStruct((ATT_HEADS, batch * seq, ATT_HEAD_DIM), F32)
    head_blk = (heads, span_tokens, ATT_HEAD_DIM)
    head_spec = pl.BlockSpec(head_blk, lambda b, s, hp: (hp, b * spans + s, 0))
    grid = (batch, spans, ATT_HEADS // heads)
    if others is None:
        return pl.pallas_call(
            functools.partial(_attn_kernel, dilation=dilation, nb=nb),
            out_shape=(per_head, per_head),
            grid=grid,
            in_specs=qkv_specs,
            out_specs=(head_spec, head_spec),
            compiler_params=_params("parallel", "parallel", "parallel"),
            name=f"attn_d{dilation}",
        )(qkv, qkv, qkv, qkv, qkv)
    return pl.pallas_call(
        functools.partial(_attn_combine_kernel, dilation=dilation, nb=nb),
        out_shape=jax.ShapeDtypeStruct((batch * seq, ATT_GROUP_WIDTH), BF16),
        grid=grid,
        in_specs=qkv_specs + [head_spec] * 4,
        out_specs=pl.BlockSpec((span_tokens, hw), lambda b, s, hp: (b * spans + s, hp)),
        scratch_shapes=[pltpu.VMEM(head_blk, F32), pltpu.VMEM(head_blk, F32)],
        compiler_params=_params("parallel", "parallel", "parallel"),
        name=f"attn_d{dilation}_combine",
    )(qkv, qkv, qkv, qkv, qkv, *others)


RET_CHUNKS_PER_STEP = 16


def _ret_kernel(q_ref, k_ref, v_ref, g_ref, gn_ref, intra_ref, cross_ref, sdec_ref, cdec_ref,
                z_ref, state_ref, *, cb):
    @pl.when(pl.program_id(2) == 0)
    def _():
        state_ref[...] = jnp.zeros_like(state_ref)

    c = RET_CHUNK
    intra, cross_decay, state_decay, chunk_decay = intra_ref[...], cross_ref[...], sdec_ref[...], cdec_ref[...]
    gn = gn_ref[...]
    qs, inners, updates = [], [], []
    for t in range(cb):
        rows = slice(t * c, (t + 1) * c)
        q = q_ref[rows, :].astype(BF16)
        k = k_ref[rows, :]
        v = v_ref[rows, :]
        att = lax.dot_general(q, k.astype(BF16), (((1,), (1,)), ((), ())), preferred_element_type=F32) * intra
        inners.append(_mm(att.astype(BF16), v))
        kd = (k * state_decay).astype(BF16)
        updates.append(lax.dot_general(kd, v, (((0,), (0,)), ((), ())), preferred_element_type=F32))
        qs.append(q)
    state = state_ref[...]
    for t in range(cb):
        rows = slice(t * c, (t + 1) * c)
        r = inners[t] + _mm(qs[t], state.astype(BF16)) * cross_decay
        state = chunk_decay * state + updates[t]
        mu = jnp.mean(r, axis=-1, keepdims=True)
        var = jnp.mean(jnp.square(r - mu), axis=-1, keepdims=True)
        rn = (r - mu) * lax.rsqrt(var + GN_EPS) * gn
        g = g_ref[rows, :]
        z_ref[rows, :] = ((g * _sigmoid(g)) * rn).astype(z_ref.dtype)
    state_ref[...] = state


def _retention(qk, v, gates, gn_g, batch, seq):
    n = qk.shape[0]
    c = RET_CHUNK
    nc = seq // c
    cb = min(RET_CHUNKS_PER_STEP, nc)
    assert nc % cb == 0
    steps = nc // cb
    hh = jnp.arange(RET_HEADS, dtype=F32)
    log_gamma = jnp.log(1.0 - 2.0 ** (-5.0 - hh))
    idx = jnp.arange(c, dtype=F32)
    diff = idx[:, None] - idx[None, :]
    intra = jnp.where(diff >= 0, jnp.exp(jnp.maximum(diff, 0.0)[None] * log_gamma[:, None, None]), 0.0)
    cross_decay = jnp.exp((idx + 1.0)[None, :] * log_gamma[:, None])[..., None]
    state_decay = jnp.exp((c - 1.0 - idx)[None, :] * log_gamma[:, None])[..., None]
    chunk_decay = jnp.exp(c * log_gamma)[:, None, None]

    def rows(b, h, s):
        return b * steps + s

    kern = functools.partial(_ret_kernel, cb=cb)
    return pl.pallas_call(
        kern,
        out_shape=jax.ShapeDtypeStruct((n, RET_V_WIDTH), BF16),
        grid=(batch, RET_HEADS, steps),
        in_specs=[
            pl.BlockSpec((cb * c, RET_QK_DIM), lambda b, h, s: (rows(b, h, s), h)),
            pl.BlockSpec((cb * c, RET_QK_DIM), lambda b, h, s: (rows(b, h, s), RET_HEADS + h)),
            pl.BlockSpec((cb * c, RET_V_DIM), lambda b, h, s: (rows(b, h, s), h)),
            pl.BlockSpec((cb * c, RET_V_DIM), lambda b, h, s: (rows(b, h, s), h)),
            pl.BlockSpec((1, RET_V_DIM), lambda b, h, s: (0, h)),
            pl.BlockSpec((None, c, c), lambda b, h, s: (h, 0, 0)),
            pl.BlockSpec((None, c, 1), lambda b, h, s: (h, 0, 0)),
            pl.BlockSpec((None, c, 1), lambda b, h, s: (h, 0, 0)),
            pl.BlockSpec((None, 1, 1), lambda b, h, s: (h, 0, 0)),
        ],
        out_specs=pl.BlockSpec((cb * c, RET_V_DIM), lambda b, h, s: (rows(b, h, s), h)),
        scratch_shapes=[pltpu.VMEM((RET_QK_DIM, RET_V_DIM), F32)],
        compiler_params=_params("parallel", "parallel", "arbitrary"),
        name="retention",
    )(qk, qk, v, gates, gn_g, intra, cross_decay, state_decay, chunk_decay)


def _mix_kernel(oa_ref, z_ref, wa_ref, wr_ref, ga_ref, gr_ref, o_ref):
    ya = _mm(oa_ref[...], wa_ref[...])
    yr = _mm(z_ref[...], wr_ref[...])
    o_ref[...] = (_sigmoid(ga_ref[...]) * ya + _sigmoid(gr_ref[...]) * yr).astype(o_ref.dtype)


def _mix(o_att, z, wa, wr, gates_att, att_col0, gates_ret, ret_col0):
    n = o_att.shape[0]
    d = wa.shape[1]
    tm, tn = _tile(n, 512), _tile(d, 1024)
    assert att_col0 % tn == 0 and ret_col0 % tn == 0
    ga_off = att_col0 // tn
    gr_off = ret_col0 // tn
    return pl.pallas_call(
        _mix_kernel,
        out_shape=jax.ShapeDtypeStruct((n, d), BF16),
        grid=(d // tn, n // tm),
        in_specs=[pl.BlockSpec((tm, o_att.shape[1]), lambda j, i: (i, 0)),
                  pl.BlockSpec((tm, z.shape[1]), lambda j, i: (i, 0)),
                  pl.BlockSpec((wa.shape[0], tn), lambda j, i: (0, j)),
                  pl.BlockSpec((wr.shape[0], tn), lambda j, i: (0, j)),
                  pl.BlockSpec((tm, tn), lambda j, i: (i, ga_off + j)),
                  pl.BlockSpec((tm, tn), lambda j, i: (i, gr_off + j))],
        out_specs=pl.BlockSpec((tm, tn), lambda j, i: (i, j)),
        compiler_params=_params("parallel", "parallel"),
        name="mix",
    )(o_att, z, wa, wr, gates_att, gates_ret)


def _layer_norm_rows(t, g, b):
    mu = jnp.mean(t, axis=-1, keepdims=True)
    var = jnp.mean(jnp.square(t - mu), axis=-1, keepdims=True)
    return (t - mu) * lax.rsqrt(var + LN_EPS) * g + b


LN_ROW_CHUNK = 256


def _layer_norm_inplace(o_ref, g_ref, b_ref, cast_ref=None):
    g, b = g_ref[...], b_ref[...]
    chunk = min(LN_ROW_CHUNK, o_ref.shape[0])

    def body(r, carry):
        rows = pl.ds(pl.multiple_of(r * chunk, chunk), chunk)
        y = _layer_norm_rows(o_ref[rows, :], g, b)
        o_ref[rows, :] = y
        if cast_ref is not None:
            cast_ref[rows, :] = y.astype(cast_ref.dtype)
        return carry

    lax.fori_loop(0, o_ref.shape[0] // chunk, body, 0)


def _proj_ln_kernel(a_ref, w_ref, res_ref, g_ref, b_ref, of_ref, ob_ref, *, alpha):
    of_ref[...] = alpha * res_ref[...] + _mm(a_ref[...], w_ref[...])
    _layer_norm_inplace(of_ref, g_ref, b_ref, ob_ref)


def _proj_ln(a, w, res, g, b, alpha):
    n, k = a.shape
    d = w.shape[1]
    tm = _tile(n, 128)
    kern = functools.partial(_proj_ln_kernel, alpha=alpha)
    row = pl.BlockSpec((tm, d), lambda i: (i, 0))
    vec = pl.BlockSpec((1, d), lambda i: (0, 0))
    return pl.pallas_call(
        kern,
        out_shape=(jax.ShapeDtypeStruct((n, d), F32), jax.ShapeDtypeStruct((n, d), BF16)),
        grid=(n // tm,),
        in_specs=[pl.BlockSpec((tm, k), lambda i: (i, 0)),
                  pl.BlockSpec((k, d), lambda i: (0, 0), pipeline_mode=pl.Buffered(1)),
                  row, vec, vec],
        out_specs=(row, row),
        compiler_params=_params("parallel"),
        name="proj_ln",
    )(a, w, res, g, b)


SWIGLU_ROW_SPLITS = 4


def _swiglu_kernel(h_ref, wg_ref, wu_ref, *refs):
    n_cast = (len(refs) - 1) // 2
    o_ref = refs[n_cast]
    wg, wu = wg_ref[...], wu_ref[...]
    for rows in _row_halves(o_ref.shape[0], SWIGLU_ROW_SPLITS):
        h = h_ref[rows, :]
        gate = _mm(h, wg)
        up = _mm(h, wu)
        o_ref[rows, :] = ((gate * _sigmoid(gate)) * up).astype(o_ref.dtype)
    _run_casts(refs[:n_cast], refs[n_cast + 1:])


def _swiglu(h_bf, wg, wu, cast=()):
    n, k = h_bf.shape
    f = wg.shape[1]
    tm, tn = _tile(n, 2048), _tile(f, 256)
    n_j = f // tn
    w_spec = pl.BlockSpec((k, tn), lambda i, j: (0, j))
    cast_in, cast_out, cast_shapes = _cast_jobs(cast, (n // tm) * n_j, lambda i, j: i * n_j + j)
    outs = pl.pallas_call(
        _swiglu_kernel,
        out_shape=[jax.ShapeDtypeStruct((n, f), BF16)] + cast_shapes,
        grid=(n // tm, n_j),
        in_specs=[pl.BlockSpec((tm, k), lambda i, j: (i, 0)), w_spec, w_spec] + cast_in,
        out_specs=[pl.BlockSpec((tm, tn), lambda i, j: (i, j))] + cast_out,
        compiler_params=_params("arbitrary", "arbitrary"),
        name="swiglu",
    )(h_bf, wg, wu, *[w for w, _ in cast])
    return outs if cast else outs[0]


def _ple_kernel(hb_ref, wg_ref, p_ref, wu_ref, hf_ref, o_ref, *, alpha):
    gate = _sigmoid(_mm(hb_ref[...], wg_ref[...]))
    up = _mm(p_ref[...], wu_ref[...])
    o_ref[...] = alpha * hf_ref[...] + gate * up


def _ple_base(h_bf, h_f32, p_bf, wpg, wpu, alpha):
    n, k = h_bf.shape
    d = wpg.shape[1]
    tm, tn = _tile(n, 512), _tile(d, 1024)
    kern = functools.partial(_ple_kernel, alpha=alpha)
    return pl.pallas_call(
        kern,
        out_shape=jax.ShapeDtypeStruct((n, d), F32),
        grid=(d // tn, n // tm),
        in_specs=[pl.BlockSpec((tm, k), lambda j, i: (i, 0)),
                  pl.BlockSpec((k, tn), lambda j, i: (0, j)),
                  pl.BlockSpec((tm, p_bf.shape[1]), lambda j, i: (i, 0)),
                  pl.BlockSpec((p_bf.shape[1], tn), lambda j, i: (0, j)),
                  pl.BlockSpec((tm, tn), lambda j, i: (i, j))],
        out_specs=pl.BlockSpec((tm, tn), lambda j, i: (i, j)),
        compiler_params=_params("parallel", "parallel"),
        name="ple_base",
    )(h_bf, wpg, p_bf, wpu, h_f32)


def _down_ln_kernel(a_ref, w_ref, base_ref, g_ref, b_ref, o_ref, *, tn):
    j = pl.program_id(1)
    col = pl.multiple_of(j * tn, tn)
    o_ref[:, pl.ds(col, tn)] = base_ref[...] + _mm(a_ref[...], w_ref[...])

    @pl.when(j == pl.num_programs(1) - 1)
    def _():
        _layer_norm_inplace(o_ref, g_ref, b_ref)


def _down_ln(hidden, wd, base, g, b):
    n, f = hidden.shape
    d = wd.shape[1]
    tm, tn = _tile(n, 512), _tile(d, 256)
    kern = functools.partial(_down_ln_kernel, tn=tn)
    vec = pl.BlockSpec((1, d), lambda i, j: (0, 0))
    return pl.pallas_call(
        kern,
        out_shape=jax.ShapeDtypeStruct((n, d), F32),
        grid=(n // tm, d // tn),
        in_specs=[pl.BlockSpec((tm, f), lambda i, j: (i, 0)),
                  pl.BlockSpec((f, tn), lambda i, j: (0, j)),
                  pl.BlockSpec((tm, tn), lambda i, j: (i, j)),
                  vec, vec],
        out_specs=pl.BlockSpec((tm, d), lambda i, j: (i, 0)),
        compiler_params=_params("parallel", "arbitrary"),
        name="down_ln",
    )(hidden, wd, base, g, b)


def _layer(h_f32, p_bf, w_in, w_attn_out, w_ret_out, ret_gn_g, w_o, ln1_g, ln1_b,
           w_ffn_gate, w_ffn_up, w_ffn_down, w_ple_gate, w_ple_up, ln2_g, ln2_b,
           cos, sin, batch, seq, alpha):
    d_model = h_f32.shape[1]
    c_qr = 3 * ATT_QKV_WIDTH
    c_vr = c_qr + 2 * RET_QK_WIDTH
    c_gr = c_vr + RET_V_WIDTH
    c_end = c_gr + RET_V_WIDTH + 2 * d_model
    assert w_in.shape[1] == c_end and c_end % COL_BLOCK == 0
    assert ATT_DILATIONS[0] == 1 and ATT_GROUP_WIDTH == COL_BLOCK
    n_groups = len(ATT_DILATIONS)

    def att_blocks(g):
        return [part * n_groups + g for part in range(3)]

    def blocks(c0, c1):
        return list(range(c0 // COL_BLOCK, c1 // COL_BLOCK))

    w_first = _round_blocks(w_in, att_blocks(0))
    qkv_first, h_bf, w_rot, w_o, w_ple_gate = _proj_units_from_f32(
        h_f32, w_first, batch, seq, cast=((w_in, blocks(c_qr, c_vr)), (w_o, None), (w_ple_gate, None)))
    qk_r, w_vr, w_attn_out = _proj_rot(
        h_bf, w_rot, 0, cos, sin, seq, cast=((w_in, blocks(c_vr, c_gr)), (w_attn_out, None)))
    c_gate_ret = c_gr + RET_V_WIDTH + d_model
    assert c_gate_ret % COL_BLOCK == 0
    v_r, w_gates_a, w_ret_out = _proj(h_bf, w_vr, 0, RET_V_WIDTH, BF16, "proj_vr",
                                      cast=((w_in, blocks(c_gr, c_gate_ret)), (w_ret_out, None)))
    later_groups = list(range(1, n_groups))
    gates_a, w_gates_b, w_att_later = _proj(
        h_bf, w_gates_a, 0, RET_V_WIDTH + d_model, F32, "proj_gates",
        cast=((w_in, blocks(c_gate_ret, c_end)), (w_in, sum((att_blocks(g) for g in later_groups), []))))
    gates_b = _proj(h_bf, w_gates_b, 0, d_model, F32, "proj_gate_ret")

    others = list(_attention(qkv_first, 1, batch, seq))
    for pos, g in enumerate(later_groups):
        qkv = _proj_units(h_bf, w_att_later, 3 * pos, ATT_DILATIONS[g], batch, seq)
        if g < n_groups - 1:
            others.extend(_attention(qkv, ATT_DILATIONS[g], batch, seq))
        else:
            o_att = _attention(qkv, ATT_DILATIONS[g], batch, seq, others=others)

    z = _retention(qk_r, v_r, gates_a, ret_gn_g, batch, seq)

    mixed = _mix(o_att, z, w_attn_out, w_ret_out, gates_a, RET_V_WIDTH, gates_b, 0)
    h1_f32, h1_bf = _proj_ln(mixed, w_o, h_f32, ln1_g, ln1_b, alpha)

    hidden, w_ffn_down = _swiglu(h1_bf, w_ffn_gate, w_ffn_up, cast=((w_ffn_down, None),))
    base = _ple_base(h1_bf, h1_f32, p_bf, w_ple_gate, w_ple_up, alpha)
    return _down_ln(hidden, w_ffn_down, base, ln2_g, ln2_b)


def kernel(x, p, w_in, w_attn_out, w_ret_out, ret_gn_g, w_o, ln1_g, ln1_b, w_ffn_gate, w_ffn_up,
           w_ffn_down, w_ple_gate, w_ple_up, ln2_g, ln2_b):
    batch, seq, d_model = x.shape
    depth = w_in.shape[0]
    n = batch * seq
    alpha = (2 * depth) ** 0.25

    half = RET_QK_DIM // 2
    pos = jnp.arange(seq, dtype=F32)
    inv_freq = RET_ROPE_BASE ** (-jnp.arange(half, dtype=F32) / half)
    ang = pos[:, None] * inv_freq[None, :]
    cos, sin = jnp.cos(ang), jnp.sin(ang)

    h = x.reshape(n, d_model)
    for i in range(depth):
        bf = lambda w: w[i].astype(BF16)
        row = lambda v: v[i].reshape(1, -1)
        h = _layer(h, p[i].reshape(n, -1).astype(BF16),
                   w_in[i], w_attn_out[i], w_ret_out[i], row(ret_gn_g), w_o[i], row(ln1_g), row(ln1_b),
                   w_ffn_gate[i], w_ffn_up[i], w_ffn_down[i], w_ple_gate[i], bf(w_ple_up),
                   row(ln2_g), row(ln2_b), cos, sin, batch, seq, alpha)
    return h.reshape(batch, seq, d_model).astype(x.dtype)
```

```python
import functools

import jax
import jax.numpy as jnp
from jax import lax
from jax.experimental import pallas as pl
from jax.experimental.pallas import tpu as pltpu

ATT_HEAD_DIM = 128
ATT_HEADS = 8
ATT_DILATIONS = (1, 4, 16)
ATT_BLOCK = 128
ATT_GROUP_WIDTH = ATT_HEADS * ATT_HEAD_DIM
ATT_QKV_WIDTH = len(ATT_DILATIONS) * ATT_GROUP_WIDTH
RET_HEADS = 8
RET_QK_DIM = 256
RET_V_DIM = 512
RET_QK_WIDTH = RET_HEADS * RET_QK_DIM
RET_V_WIDTH = RET_HEADS * RET_V_DIM
RET_CHUNK = 128
RET_ROPE_BASE = 10000.0
LN_EPS = 1e-5
GN_EPS = 1e-6
NEG_INF = -1e30

V7X_VMEM_LIMIT_BYTES = 56 * 1024 * 1024
LANE = 128

BF16 = jnp.bfloat16
F32 = jnp.float32


V7X_VMEM_LIMIT_LARGE_BYTES = 61 * 1024 * 1024


def _params(*semantics, vmem_limit_bytes=V7X_VMEM_LIMIT_BYTES):
    return pltpu.CompilerParams(dimension_semantics=semantics, vmem_limit_bytes=vmem_limit_bytes)


def _tile(dim, pref):
    if dim <= pref:
        return dim
    t = (pref // LANE) * LANE
    while t > LANE and dim % t:
        t -= LANE
    assert dim % t == 0, (dim, pref)
    return t


def _sigmoid(v):
    return 1.0 / (1.0 + jnp.exp(-v))


def _mm(a, b):
    return lax.dot_general(a, b, (((1,), (0,)), ((), ())), preferred_element_type=F32)


ROW_SPLITS = 2


def _row_halves(n_rows, splits=ROW_SPLITS):
    rows = n_rows // splits
    return [slice(c * rows, (c + 1) * rows) for c in range(splits)]


BF16_SUBLANES = 16


COL_BLOCK = 1024


def _cast_job(w, col_blocks, steps, linear_step):
    rows, cols = w.shape
    ncb, width = (1, cols) if col_blocks is None else (len(col_blocks), COL_BLOCK)
    slab = -(-(-(-rows * ncb // steps)) // BF16_SUBLANES) * BF16_SUBLANES
    while rows % slab:
        slab += BF16_SUBLANES
    pieces = (rows // slab) * ncb
    assert pieces <= steps

    def piece(*idx):
        return jnp.minimum(linear_step(*idx), pieces - 1)

    def source_col(local):
        if col_blocks is None:
            return 0
        return sum(jnp.where(local == i, cb, 0) for i, cb in enumerate(col_blocks))

    in_spec = pl.BlockSpec((slab, width), lambda *idx: (piece(*idx) // ncb, source_col(piece(*idx) % ncb)))
    out_spec = pl.BlockSpec((slab, width), lambda *idx: (piece(*idx) // ncb, piece(*idx) % ncb))
    return in_spec, out_spec, jax.ShapeDtypeStruct((rows, ncb * width), BF16)


def _cast_jobs(cast, steps, linear_step):
    jobs = [_cast_job(w, cbs, steps, linear_step) for w, cbs in cast]
    return [j[0] for j in jobs], [j[1] for j in jobs], [j[2] for j in jobs]


def _run_casts(srcs, dsts):
    for src, dst in zip(srcs, dsts):
        dst[...] = src[...].astype(dst.dtype)


def _round_kernel(src_ref, dst_ref):
    dst_ref[...] = src_ref[...].astype(dst_ref.dtype)


def _round_blocks(w, col_blocks, slab=512):
    rows = w.shape[0]
    slab = min(slab, rows)
    assert rows % slab == 0
    steps = (rows // slab) * len(col_blocks)
    in_spec, out_spec, out_shape = _cast_job(w, col_blocks, steps, lambda t: t)
    return pl.pallas_call(
        _round_kernel, out_shape=out_shape, grid=(steps,), in_specs=[in_spec], out_specs=out_spec,
        compiler_params=_params("parallel"), name="round_first_weights",
    )(w)


def _proj_kernel(x_ref, w_ref, *refs):
    n_cast = (len(refs) - 1) // 2
    o_ref = refs[n_cast]
    o_ref[...] = _mm(x_ref[...], w_ref[...]).astype(o_ref.dtype)
    _run_casts(refs[:n_cast], refs[n_cast + 1:])


def _proj(x_bf, w_bf, col0, width, out_dtype, name, cast=(), rows_per_step=1024):
    n, k = x_bf.shape
    tm, tn = _tile(n, rows_per_step), _tile(width, 1024)
    assert col0 % tn == 0
    off = col0 // tn
    n_i = n // tm
    steps = (width // tn) * n_i
    cast_in, cast_out, cast_shapes = _cast_jobs(cast, steps, lambda j, i: j * n_i + i)
    outs = pl.pallas_call(
        _proj_kernel,
        out_shape=[jax.ShapeDtypeStruct((n, width), out_dtype)] + cast_shapes,
        grid=(width // tn, n_i),
        in_specs=[pl.BlockSpec((tm, k), lambda j, i: (i, 0)),
                  pl.BlockSpec((k, tn), lambda j, i: (0, off + j))] + cast_in,
        out_specs=[pl.BlockSpec((tm, tn), lambda j, i: (i, j))] + cast_out,
        compiler_params=_params("arbitrary", "arbitrary"),
        name=name,
    )(x_bf, w_bf, *[w for w, _ in cast])
    return outs if cast else outs[0]


def _proj_rot_kernel(x_ref, w_ref, cos_ref, sin_ref, *refs, n_q_blocks, k_scale):
    n_cast = (len(refs) - 1) // 2
    o_ref = refs[n_cast]
    _run_casts(refs[:n_cast], refs[n_cast + 1:])
    j = pl.program_id(0)
    acc = _mm(x_ref[...], w_ref[...])
    scale = jnp.where(j >= n_q_blocks, k_scale, 1.0).astype(F32)
    cos = cos_ref[...]
    sin = sin_ref[...]
    half = RET_QK_DIM // 2
    for h in range(acc.shape[1] // RET_QK_DIM):
        lo = h * RET_QK_DIM
        t1 = acc[:, lo:lo + half]
        t2 = acc[:, lo + half:lo + RET_QK_DIM]
        o_ref[:, lo:lo + half] = (t1 * cos - t2 * sin) * scale
        o_ref[:, lo + half:lo + RET_QK_DIM] = (t1 * sin + t2 * cos) * scale


def _proj_rot(x_bf, w_bf, col0, cos, sin, seq, cast=()):
    n, k = x_bf.shape
    width = 2 * RET_QK_WIDTH
    tm, tn = _tile(seq, 1024), 1024
    assert col0 % tn == 0 and RET_QK_WIDTH % tn == 0 and n % seq == 0
    off = col0 // tn
    pos_blocks = seq // tm
    n_i = n // tm
    kern = functools.partial(_proj_rot_kernel, n_q_blocks=RET_QK_WIDTH // tn, k_scale=RET_QK_DIM ** -0.5)
    half = RET_QK_DIM // 2
    cast_in, cast_out, cast_shapes = _cast_jobs(cast, (width // tn) * n_i, lambda j, i: j * n_i + i)
    outs = pl.pallas_call(
        kern,
        out_shape=[jax.ShapeDtypeStruct((n, width), F32)] + cast_shapes,
        grid=(width // tn, n_i),
        in_specs=[pl.BlockSpec((tm, k), lambda j, i: (i, 0)),
                  pl.BlockSpec((k, tn), lambda j, i: (0, off + j)),
                  pl.BlockSpec((tm, half), lambda j, i: (i % pos_blocks, 0)),
                  pl.BlockSpec((tm, half), lambda j, i: (i % pos_blocks, 0))] + cast_in,
        out_specs=[pl.BlockSpec((tm, tn), lambda j, i: (i, j))] + cast_out,
        compiler_params=_params("arbitrary", "arbitrary"),
        name="proj_rot",
    )(x_bf, w_bf, cos, sin, *[w for w, _ in cast])
    return outs if cast else outs[0]


ATT_SPAN_BLOCKS = 16
ATT_HEADS_PER_STEP = 4
ATT_HEADS_PER_COMBINE_STEP = 2


def _proj_units_kernel(x_ref, w_ref, o_ref, acc_ref, *, dilation):
    if dilation == 1:
        o_ref[0, 0] = _mm(x_ref[...], w_ref[...]).astype(o_ref.dtype)
        return
    w = w_ref[...]
    slabs = w.shape[1] // LANE
    for tok in _row_halves(x_ref.shape[0]):
        acc = _mm(x_ref[tok, :], w)
        for s in range(slabs):
            acc_ref[s, tok, :] = acc[:, s * LANE:(s + 1) * LANE]
        rows = acc.shape[0] // dilation
        dst = slice(tok.start // dilation, tok.start // dilation + rows)
        for r in range(dilation):
            for s in range(slabs):
                piece = acc_ref[s, pl.ds(tok.start + r, rows, stride=dilation), :]
                o_ref[r, 0, dst, s * LANE:(s + 1) * LANE] = piece.astype(o_ref.dtype)


def _proj_units(x_bf, w_bf, first_block, dilation, batch, seq):
    n, k = x_bf.shape
    tm, tn = _tile(seq, 1024), ATT_GROUP_WIDTH
    assert (tm // ROW_SPLITS) % (dilation * BF16_SUBLANES) == 0 and seq % tm == 0
    rows = tm // dilation
    tiles = seq // tm
    kern = functools.partial(_proj_units_kernel, dilation=dilation)
    out = pl.pallas_call(
        kern,
        out_shape=jax.ShapeDtypeStruct((3, batch * dilation, tiles, rows, tn), BF16),
        grid=(3, n // tm),
        in_specs=[pl.BlockSpec((tm, k), lambda j, i: (i, 0)),
                  pl.BlockSpec((k, tn), lambda j, i: (0, first_block + j))],
        out_specs=pl.BlockSpec((None, dilation, 1, rows, tn), lambda j, i: (j, i // tiles, i % tiles, 0, 0)),
        scratch_shapes=[pltpu.VMEM((tn // LANE, tm, LANE), F32)],
        compiler_params=_params("parallel", "parallel"),
        name=f"proj_att_d{dilation}",
    )(x_bf, w_bf)
    n_blk = seq // (dilation * ATT_BLOCK)
    return out.reshape(3, batch * dilation, n_blk, ATT_BLOCK, tn)


def _proj_units_f32_kernel(x_ref, w_ref, *refs):
    n_cast = (len(refs) - 2) // 2
    o_ref, xb_ref = refs[n_cast], refs[n_cast + 1]
    xb = x_ref[...].astype(xb_ref.dtype)
    xb_ref[...] = xb
    tn = o_ref.shape[-1]
    for part in range(3):
        o_ref[part, 0, 0] = _mm(xb, w_ref[:, part * tn:(part + 1) * tn]).astype(o_ref.dtype)
    _run_casts(refs[:n_cast], refs[n_cast + 2:])


def _proj_units_from_f32(x_f32, w_qkv, batch, seq, cast=()):
    n, k = x_f32.shape
    tm, tn = _tile(seq, 256), ATT_GROUP_WIDTH
    assert w_qkv.shape == (k, 3 * tn)
    tiles = seq // tm
    cast_in, cast_out, cast_shapes = _cast_jobs(cast, n // tm, lambda i: i)
    outs = pl.pallas_call(
        _proj_units_f32_kernel,
        out_shape=[jax.ShapeDtypeStruct((3, batch, tiles, tm, tn), BF16),
                   jax.ShapeDtypeStruct((n, k), BF16)] + cast_shapes,
        grid=(n // tm,),
        in_specs=[pl.BlockSpec((tm, k), lambda i: (i, 0)),
                  pl.BlockSpec((k, 3 * tn), lambda i: (0, 0), pipeline_mode=pl.Buffered(1))] + cast_in,
        out_specs=[pl.BlockSpec((3, 1, 1, tm, tn), lambda i: (0, i // tiles, i % tiles, 0, 0)),
                   pl.BlockSpec((tm, k), lambda i: (i, 0))] + cast_out,
        compiler_params=_params("arbitrary"),
        name="proj_att_d1",
    )(x_f32, w_qkv, *[w for w, _ in cast])
    return [outs[0].reshape(3, batch, seq // ATT_BLOCK, ATT_BLOCK, tn)] + list(outs[1:])


def _attn_span(q_ref, k_ref, kp_ref, v_ref, vp_ref, o_ref, lse_ref, *, dilation, nb):
    qi = lax.broadcasted_iota(jnp.int32, (ATT_BLOCK, 2 * ATT_BLOCK), 0)
    kj = lax.broadcasted_iota(jnp.int32, (ATT_BLOCK, 2 * ATT_BLOCK), 1)
    in_window = kj <= qi + ATT_BLOCK
    valid_inner = jnp.logical_and(kj >= qi, in_window)
    lo = jnp.where(pl.program_id(1) > 0, 0, ATT_BLOCK)
    valid_first = jnp.logical_and(kj >= jnp.maximum(qi, lo), in_window)
    scale = ATT_HEAD_DIM ** -0.5
    dn = (((1,), (1,)), ((), ()))
    ones = jnp.ones((2 * ATT_BLOCK, ATT_HEAD_DIM), BF16)
    units = [(r, i, h) for r in range(dilation) for i in range(nb) for h in range(o_ref.shape[0])]

    def cols(h):
        return slice(h * ATT_HEAD_DIM, (h + 1) * ATT_HEAD_DIM)

    def with_prev(cur_ref, prev_ref, r, i, h):
        prev = prev_ref[r, 0, :, cols(h)] if i == 0 else cur_ref[r, i - 1, :, cols(h)]
        return jnp.concatenate([prev, cur_ref[r, i, :, cols(h)]], axis=0)

    scores = [lax.dot_general(q_ref[r, i, :, cols(h)], with_prev(k_ref, kp_ref, r, i, h), dn,
                              preferred_element_type=F32) * scale for r, i, h in units]
    probs, maxes = [], []
    for (r, i, h), s in zip(units, scores):
        s = jnp.where(valid_first if i == 0 else valid_inner, s, NEG_INF)
        m = jnp.max(s, axis=-1, keepdims=True)
        probs.append(jnp.exp(s - m).astype(BF16))
        maxes.append(m)
    for (r, i, h), p, m in zip(units, probs, maxes):
        v_aug = jnp.concatenate([with_prev(v_ref, vp_ref, r, i, h), ones], axis=1)
        acc = _mm(p, v_aug)
        den = acc[:, ATT_HEAD_DIM:]
        if dilation == 1:
            rows = pl.ds(i * ATT_BLOCK, ATT_BLOCK)
        else:
            rows = pl.ds(i * ATT_BLOCK * dilation + r, ATT_BLOCK, stride=dilation)
        o_ref[h, rows, :] = acc[:, :ATT_HEAD_DIM] / den
        lse_ref[h, rows, :] = m + jnp.log(den)


def _attn_kernel(q_ref, k_ref, kp_ref, v_ref, vp_ref, o_ref, lse_ref, *, dilation, nb):
    _attn_span(q_ref, k_ref, kp_ref, v_ref, vp_ref, o_ref, lse_ref, dilation=dilation, nb=nb)


ATT_COMBINE_ROWS = 256


def _attn_combine_kernel(q_ref, k_ref, kp_ref, v_ref, vp_ref, o1, l1, o2, l2, out_ref, o3, l3, *, dilation, nb):
    _attn_span(q_ref, k_ref, kp_ref, v_ref, vp_ref, o3, l3, dilation=dilation, nb=nb)

    def body(c, carry):
        rows = pl.ds(pl.multiple_of(c * ATT_COMBINE_ROWS, ATT_COMBINE_ROWS), ATT_COMBINE_ROWS)
        for h in range(o3.shape[0]):
            a, b, d = l1[h, rows, :], l2[h, rows, :], l3[h, rows, :]
            m = jnp.maximum(jnp.maximum(a, b), d)
            ea, eb, ed = jnp.exp(a - m), jnp.exp(b - m), jnp.exp(d - m)
            s = ea + eb + ed
            out = (ea / s) * o1[h, rows, :] + (eb / s) * o2[h, rows, :] + (ed / s) * o3[h, rows, :]
            out_ref[rows, h * ATT_HEAD_DIM:(h + 1) * ATT_HEAD_DIM] = out.astype(out_ref.dtype)
        return carry

    lax.fori_loop(0, out_ref.shape[0] // ATT_COMBINE_ROWS, body, 0)


def _attention(qkv, dilation, batch, seq, others=None):
    n_blk = qkv.shape[2]
    assert ATT_SPAN_BLOCKS % dilation == 0
    nb = ATT_SPAN_BLOCKS // dilation
    assert n_blk % nb == 0
    spans = n_blk // nb
    span_tokens = ATT_SPAN_BLOCKS * ATT_BLOCK
    heads = ATT_HEADS_PER_STEP if others is None else ATT_HEADS_PER_COMBINE_STEP
    hw = heads * ATT_HEAD_DIM
    cur = (None, dilation, nb, ATT_BLOCK, hw)
    one = (None, dilation, 1, ATT_BLOCK, hw)

    def at(part):
        return lambda b, s, hp: (part, b, s, 0, hp)

    def before(part):
        return lambda b, s, hp: (part, b, jnp.maximum(s * nb - 1, 0), 0, hp)

    qkv_specs = [pl.BlockSpec(cur, at(0)), pl.BlockSpec(cur, at(1)), pl.BlockSpec(one, before(1)),
                 pl.BlockSpec(cur, at(2)), pl.BlockSpec(one, before(2))]
    per_head = jax.ShapeDtypeStruct((ATT_HEADS, batch * seq, ATT_HEAD_DIM), F32)
    head_blk = (heads, span_tokens, ATT_HEAD_DIM)
    head_spec = pl.BlockSpec(head_blk, lambda b, s, hp: (hp, b * spans + s, 0))
    grid = (batch, spans, ATT_HEADS // heads)
    if others is None:
        return pl.pallas_call(
            functools.partial(_attn_kernel, dilation=dilation, nb=nb),
            out_shape=(per_head, per_head),
            grid=grid,
            in_specs=qkv_specs,
            out_specs=(head_spec, head_spec),
            compiler_params=_params("parallel", "parallel", "parallel"),
            name=f"attn_d{dilation}",
        )(qkv, qkv, qkv, qkv, qkv)
    return pl.pallas_call(
        functools.partial(_attn_combine_kernel, dilation=dilation, nb=nb),
        out_shape=jax.ShapeDtypeStruct((batch * seq, ATT_GROUP_WIDTH), BF16),
        grid=grid,
        in_specs=qkv_specs + [head_spec] * 4,
        out_specs=pl.BlockSpec((span_tokens, hw), lambda b, s, hp: (b * spans + s, hp)),
        scratch_shapes=[pltpu.VMEM(head_blk, F32), pltpu.VMEM(head_blk, F32)],
        compiler_params=_params("parallel", "parallel", "parallel"),
        name=f"attn_d{dilation}_combine",
    )(qkv, qkv, qkv, qkv, qkv, *others)


RET_CHUNKS_PER_STEP = 16


def _ret_kernel(q_ref, k_ref, v_ref, g_ref, gn_ref, intra_ref, cross_ref, sdec_ref, cdec_ref,
                z_ref, state_ref, *, cb):
    @pl.when(pl.program_id(2) == 0)
    def _():
        state_ref[...] = jnp.zeros_like(state_ref)

    c = RET_CHUNK
    intra, cross_decay, state_decay, chunk_decay = intra_ref[...], cross_ref[...], sdec_ref[...], cdec_ref[...]
    gn = gn_ref[...]
    qs, inners, updates = [], [], []
    for t in range(cb):
        rows = slice(t * c, (t + 1) * c)
        q = q_ref[rows, :].astype(BF16)
        k = k_ref[rows, :]
        v = v_ref[rows, :]
        att = lax.dot_general(q, k.astype(BF16), (((1,), (1,)), ((), ())), preferred_element_type=F32) * intra
        inners.append(_mm(att.astype(BF16), v))
        kd = (k * state_decay).astype(BF16)
        updates.append(lax.dot_general(kd, v, (((0,), (0,)), ((), ())), preferred_element_type=F32))
        qs.append(q)
    state = state_ref[...]
    for t in range(cb):
        rows = slice(t * c, (t + 1) * c)
        r = inners[t] + _mm(qs[t], state.astype(BF16)) * cross_decay
        state = chunk_decay * state + updates[t]
        mu = jnp.mean(r, axis=-1, keepdims=True)
        var = jnp.mean(jnp.square(r - mu), axis=-1, keepdims=True)
        rn = (r - mu) * lax.rsqrt(var + GN_EPS) * gn
        g = g_ref[rows, :]
        z_ref[rows, :] = ((g * _sigmoid(g)) * rn).astype(z_ref.dtype)
    state_ref[...] = state


def _retention(qk, v, gates, gn_g, batch, seq):
    n = qk.shape[0]
    c = RET_CHUNK
    nc = seq // c
    cb = min(RET_CHUNKS_PER_STEP, nc)
    assert nc % cb == 0
    steps = nc // cb
    hh = jnp.arange(RET_HEADS, dtype=F32)
    log_gamma = jnp.log(1.0 - 2.0 ** (-5.0 - hh))
    idx = jnp.arange(c, dtype=F32)
    diff = idx[:, None] - idx[None, :]
    intra = jnp.where(diff >= 0, jnp.exp(jnp.maximum(diff, 0.0)[None] * log_gamma[:, None, None]), 0.0)
    cross_decay = jnp.exp((idx + 1.0)[None, :] * log_gamma[:, None])[..., None]
    state_decay = jnp.exp((c - 1.0 - idx)[None, :] * log_gamma[:, None])[..., None]
    chunk_decay = jnp.exp(c * log_gamma)[:, None, None]

    def rows(b, h, s):
        return b * steps + s

    kern = functools.partial(_ret_kernel, cb=cb)
    return pl.pallas_call(
        kern,
        out_shape=jax.ShapeDtypeStruct((n, RET_V_WIDTH), BF16),
        grid=(batch, RET_HEADS, steps),
        in_specs=[
            pl.BlockSpec((cb * c, RET_QK_DIM), lambda b, h, s: (rows(b, h, s), h)),
            pl.BlockSpec((cb * c, RET_QK_DIM), lambda b, h, s: (rows(b, h, s), RET_HEADS + h)),
            pl.BlockSpec((cb * c, RET_V_DIM), lambda b, h, s: (rows(b, h, s), h)),
            pl.BlockSpec((cb * c, RET_V_DIM), lambda b, h, s: (rows(b, h, s), h)),
            pl.BlockSpec((1, RET_V_DIM), lambda b, h, s: (0, h)),
            pl.BlockSpec((None, c, c), lambda b, h, s: (h, 0, 0)),
            pl.BlockSpec((None, c, 1), lambda b, h, s: (h, 0, 0)),
            pl.BlockSpec((None, c, 1), lambda b, h, s: (h, 0, 0)),
            pl.BlockSpec((None, 1, 1), lambda b, h, s: (h, 0, 0)),
        ],
        out_specs=pl.BlockSpec((cb * c, RET_V_DIM), lambda b, h, s: (rows(b, h, s), h)),
        scratch_shapes=[pltpu.VMEM((RET_QK_DIM, RET_V_DIM), F32)],
        compiler_params=_params("parallel", "parallel", "arbitrary"),
        name="retention",
    )(qk, qk, v, gates, gn_g, intra, cross_decay, state_decay, chunk_decay)


def _mix_kernel(oa_ref, z_ref, wa_ref, wr_ref, ga_ref, gr_ref, o_ref):
    ya = _mm(oa_ref[...], wa_ref[...])
    yr = _mm(z_ref[...], wr_ref[...])
    o_ref[...] = (_sigmoid(ga_ref[...]) * ya + _sigmoid(gr_ref[...]) * yr).astype(o_ref.dtype)


def _mix(o_att, z, wa, wr, gates_att, att_col0, gates_ret, ret_col0):
    n = o_att.shape[0]
    d = wa.shape[1]
    tm, tn = _tile(n, 512), _tile(d, 1024)
    assert att_col0 % tn == 0 and ret_col0 % tn == 0
    ga_off = att_col0 // tn
    gr_off = ret_col0 // tn
    return pl.pallas_call(
        _mix_kernel,
        out_shape=jax.ShapeDtypeStruct((n, d), BF16),
        grid=(d // tn, n // tm),
        in_specs=[pl.BlockSpec((tm, o_att.shape[1]), lambda j, i: (i, 0)),
                  pl.BlockSpec((tm, z.shape[1]), lambda j, i: (i, 0)),
                  pl.BlockSpec((wa.shape[0], tn), lambda j, i: (0, j)),
                  pl.BlockSpec((wr.shape[0], tn), lambda j, i: (0, j)),
                  pl.BlockSpec((tm, tn), lambda j, i: (i, ga_off + j)),
                  pl.BlockSpec((tm, tn), lambda j, i: (i, gr_off + j))],
        out_specs=pl.BlockSpec((tm, tn), lambda j, i: (i, j)),
        compiler_params=_params("parallel", "parallel"),
        name="mix",
    )(o_att, z, wa, wr, gates_att, gates_ret)


def _layer_norm_rows(t, g, b):
    mu = jnp.mean(t, axis=-1, keepdims=True)
    var = jnp.mean(jnp.square(t - mu), axis=-1, keepdims=True)
    return (t - mu) * lax.rsqrt(var + LN_EPS) * g + b


LN_ROW_CHUNK = 256


def _layer_norm_inplace(o_ref, g_ref, b_ref, cast_ref=None):
    g, b = g_ref[...], b_ref[...]
    chunk = min(LN_ROW_CHUNK, o_ref.shape[0])

    def body(r, carry):
        rows = pl.ds(pl.multiple_of(r * chunk, chunk), chunk)
        y = _layer_norm_rows(o_ref[rows, :], g, b)
        o_ref[rows, :] = y
        if cast_ref is not None:
            cast_ref[rows, :] = y.astype(cast_ref.dtype)
        return carry

    lax.fori_loop(0, o_ref.shape[0] // chunk, body, 0)


def _proj_ln_kernel(a_ref, w_ref, res_ref, g_ref, b_ref, of_ref, ob_ref, *, alpha):
    of_ref[...] = alpha * res_ref[...] + _mm(a_ref[...], w_ref[...])
    _layer_norm_inplace(of_ref, g_ref, b_ref, ob_ref)


def _proj_ln(a, w, res, g, b, alpha):
    n, k = a.shape
    d = w.shape[1]
    tm = _tile(n, 128)
    kern = functools.partial(_proj_ln_kernel, alpha=alpha)
    row = pl.BlockSpec((tm, d), lambda i: (i, 0))
    vec = pl.BlockSpec((1, d), lambda i: (0, 0))
    return pl.pallas_call(
        kern,
        out_shape=(jax.ShapeDtypeStruct((n, d), F32), jax.ShapeDtypeStruct((n, d), BF16)),
        grid=(n // tm,),
        in_specs=[pl.BlockSpec((tm, k), lambda i: (i, 0)),
                  pl.BlockSpec((k, d), lambda i: (0, 0), pipeline_mode=pl.Buffered(1)),
                  row, vec, vec],
        out_specs=(row, row),
        compiler_params=_params("parallel"),
        name="proj_ln",
    )(a, w, res, g, b)


SWIGLU_ROW_SPLITS = 4


def _swiglu_kernel(h_ref, wg_ref, wu_ref, *refs):
    n_cast = (len(refs) - 1) // 2
    o_ref = refs[n_cast]
    wg, wu = wg_ref[...], wu_ref[...]
    for rows in _row_halves(o_ref.shape[0], SWIGLU_ROW_SPLITS):
        h = h_ref[rows, :]
        gate = _mm(h, wg)
        up = _mm(h, wu)
        o_ref[rows, :] = ((gate * _sigmoid(gate)) * up).astype(o_ref.dtype)
    _run_casts(refs[:n_cast], refs[n_cast + 1:])


def _swiglu(h_bf, wg, wu, cast=()):
    n, k = h_bf.shape
    f = wg.shape[1]
    tm, tn = _tile(n, 2048), _tile(f, 256)
    n_j = f // tn
    w_spec = pl.BlockSpec((k, tn), lambda i, j: (0, j))
    cast_in, cast_out, cast_shapes = _cast_jobs(cast, (n // tm) * n_j, lambda i, j: i * n_j + j)
    outs = pl.pallas_call(
        _swiglu_kernel,
        out_shape=[jax.ShapeDtypeStruct((n, f), BF16)] + cast_shapes,
        grid=(n // tm, n_j),
        in_specs=[pl.BlockSpec((tm, k), lambda i, j: (i, 0)), w_spec, w_spec] + cast_in,
        out_specs=[pl.BlockSpec((tm, tn), lambda i, j: (i, j))] + cast_out,
        compiler_params=_params("arbitrary", "arbitrary"),
        name="swiglu",
    )(h_bf, wg, wu, *[w for w, _ in cast])
    return outs if cast else outs[0]


def _ple_kernel(hb_ref, wg_ref, p_ref, wu_ref, hf_ref, o_ref, *, alpha):
    gate = _sigmoid(_mm(hb_ref[...], wg_ref[...]))
    up = _mm(p_ref[...], wu_ref[...])
    o_ref[...] = alpha * hf_ref[...] + gate * up


def _ple_base(h_bf, h_f32, p_bf, wpg, wpu, alpha):
    n, k = h_bf.shape
    d = wpg.shape[1]
    tm, tn = _tile(n, 1024), _tile(d, 1024)
    kern = functools.partial(_ple_kernel, alpha=alpha)
    return pl.pallas_call(
        kern,
        out_shape=jax.ShapeDtypeStruct((n, d), F32),
        grid=(d // tn, n // tm),
        in_specs=[pl.BlockSpec((tm, k), lambda j, i: (i, 0)),
                  pl.BlockSpec((k, tn), lambda j, i: (0, j)),
                  pl.BlockSpec((tm, p_bf.shape[1]), lambda j, i: (i, 0)),
                  pl.BlockSpec((p_bf.shape[1], tn), lambda j, i: (0, j)),
                  pl.BlockSpec((tm, tn), lambda j, i: (i, j))],
        out_specs=pl.BlockSpec((tm, tn), lambda j, i: (i, j)),
        compiler_params=_params("parallel", "parallel", vmem_limit_bytes=V7X_VMEM_LIMIT_LARGE_BYTES),
        name="ple_base",
    )(h_bf, wpg, p_bf, wpu, h_f32)


def _down_ln_kernel(a_ref, w_ref, base_ref, g_ref, b_ref, o_ref, *, tn):
    j = pl.program_id(1)
    col = pl.multiple_of(j * tn, tn)
    o_ref[:, pl.ds(col, tn)] = base_ref[...] + _mm(a_ref[...], w_ref[...])

    @pl.when(j == pl.num_programs(1) - 1)
    def _():
        _layer_norm_inplace(o_ref, g_ref, b_ref)


def _down_ln(hidden, wd, base, g, b):
    n, f = hidden.shape
    d = wd.shape[1]
    tm, tn = _tile(n, 512), _tile(d, 256)
    kern = functools.partial(_down_ln_kernel, tn=tn)
    vec = pl.BlockSpec((1, d), lambda i, j: (0, 0))
    return pl.pallas_call(
        kern,
        out_shape=jax.ShapeDtypeStruct((n, d), F32),
        grid=(n // tm, d // tn),
        in_specs=[pl.BlockSpec((tm, f), lambda i, j: (i, 0)),
                  pl.BlockSpec((f, tn), lambda i, j: (0, j)),
                  pl.BlockSpec((tm, tn), lambda i, j: (i, j)),
                  vec, vec],
        out_specs=pl.BlockSpec((tm, d), lambda i, j: (i, 0)),
        compiler_params=_params("parallel", "arbitrary"),
        name="down_ln",
    )(hidden, wd, base, g, b)


def _layer(h_f32, p_bf, w_in, w_attn_out, w_ret_out, ret_gn_g, w_o, ln1_g, ln1_b,
           w_ffn_gate, w_ffn_up, w_ffn_down, w_ple_gate, w_ple_up, ln2_g, ln2_b,
           cos, sin, batch, seq, alpha):
    d_model = h_f32.shape[1]
    c_qr = 3 * ATT_QKV_WIDTH
    c_vr = c_qr + 2 * RET_QK_WIDTH
    c_gr = c_vr + RET_V_WIDTH
    c_end = c_gr + RET_V_WIDTH + 2 * d_model
    assert w_in.shape[1] == c_end and c_end % COL_BLOCK == 0
    assert ATT_DILATIONS[0] == 1 and ATT_GROUP_WIDTH == COL_BLOCK
    n_groups = len(ATT_DILATIONS)

    def att_blocks(g):
        return [part * n_groups + g for part in range(3)]

    def blocks(c0, c1):
        return list(range(c0 // COL_BLOCK, c1 // COL_BLOCK))

    w_first = _round_blocks(w_in, att_blocks(0))
    qkv_first, h_bf, w_rot, w_o, w_ple_gate = _proj_units_from_f32(
        h_f32, w_first, batch, seq, cast=((w_in, blocks(c_qr, c_vr)), (w_o, None), (w_ple_gate, None)))
    qk_r, w_vr, w_attn_out = _proj_rot(
        h_bf, w_rot, 0, cos, sin, seq, cast=((w_in, blocks(c_vr, c_gr)), (w_attn_out, None)))
    c_gate_ret = c_gr + RET_V_WIDTH + d_model
    assert c_gate_ret % COL_BLOCK == 0
    v_r, w_gates_a, w_ret_out = _proj(h_bf, w_vr, 0, RET_V_WIDTH, BF16, "proj_vr",
                                      cast=((w_in, blocks(c_gr, c_gate_ret)), (w_ret_out, None)))
    later_groups = list(range(1, n_groups))
    gates_a, w_gates_b, w_att_later = _proj(
        h_bf, w_gates_a, 0, RET_V_WIDTH + d_model, F32, "proj_gates",
        cast=((w_in, blocks(c_gate_ret, c_end)), (w_in, sum((att_blocks(g) for g in later_groups), []))))
    gates_b = _proj(h_bf, w_gates_b, 0, d_model, F32, "proj_gate_ret")

    others = list(_attention(qkv_first, 1, batch, seq))
    for pos, g in enumerate(later_groups):
        qkv = _proj_units(h_bf, w_att_later, 3 * pos, ATT_DILATIONS[g], batch, seq)
        if g < n_groups - 1:
            others.extend(_attention(qkv, ATT_DILATIONS[g], batch, seq))
        else:
            o_att = _attention(qkv, ATT_DILATIONS[g], batch, seq, others=others)

    z = _retention(qk_r, v_r, gates_a, ret_gn_g, batch, seq)

    mixed = _mix(o_att, z, w_attn_out, w_ret_out, gates_a, RET_V_WIDTH, gates_b, 0)
    h1_f32, h1_bf = _proj_ln(mixed, w_o, h_f32, ln1_g, ln1_b, alpha)

    hidden, w_ffn_down = _swiglu(h1_bf, w_ffn_gate, w_ffn_up, cast=((w_ffn_down, None),))
    base = _ple_base(h1_bf, h1_f32, p_bf, w_ple_gate, w_ple_up, alpha)
    return _down_ln(hidden, w_ffn_down, base, ln2_g, ln2_b)


def kernel(x, p, w_in, w_attn_out, w_ret_out, ret_gn_g, w_o, ln1_g, ln1_b, w_ffn_gate, w_ffn_up,
           w_ffn_down, w_ple_gate, w_ple_up, ln2_g, ln2_b):
    batch, seq, d_model = x.shape
    depth = w_in.shape[0]
    n = batch * seq
    alpha = (2 * depth) ** 0.25

    half = RET_QK_DIM // 2
    pos = jnp.arange(seq, dtype=F32)
    inv_freq = RET_ROPE_BASE ** (-jnp.arange(half, dtype=F32) / half)
    ang = pos[:, None] * inv_freq[None, :]
    cos, sin = jnp.cos(ang), jnp.sin(ang)

    h = x.reshape(n, d_model)
    for i in range(depth):
        bf = lambda w: w[i].astype(BF16)
        row = lambda v: v[i].reshape(1, -1)
        h = _layer(h, p[i].reshape(n, -1).astype(BF16),
                   w_in[i], w_attn_out[i], w_ret_out[i], row(ret_gn_g), w_o[i], row(ln1_g), row(ln1_b),
                   w_ffn_gate[i], w_ffn_up[i], w_ffn_down[i], w_ple_gate[i], bf(w_ple_up),
                   row(ln2_g), row(ln2_b), cos, sin, batch, seq, alpha)
    return h.reshape(batch, seq, d_model).astype(x.dtype)
```
